```python
import jax, jax.numpy as jnp
from jax import lax
import numpy as np

D_MODEL = 1024
BATCH = 2
SEQ = 8192
DEPTH = 2
DEC_BATCH = 32
DEC_SEQ = 64
PAST_LEN = 1024

CHUNK = 64
Q_BLOCK = 128
A_HEADS = 4
A_HEAD_DIM = 128
A_WIDTH = A_HEADS * A_HEAD_DIM
B_HEADS = 4
B_HEAD_DIM = 64
B_WIDTH = B_HEADS * B_HEAD_DIM
C_GROUPS = 4
C_GROUP_DIM = 64
C_WIDTH = C_GROUPS * C_GROUP_DIM
POOL_WINDOWS = (2, 4, 8, 16)
POOL_STATE = 15
D_MIX = A_WIDTH + B_WIDTH + C_WIDTH
SPLIT_SIZES = (A_WIDTH, A_WIDTH, A_WIDTH, A_WIDTH, B_WIDTH, B_WIDTH, B_WIDTH, B_WIDTH, C_WIDTH, C_WIDTH)
D_IN = 4 * A_WIDTH + 4 * B_WIDTH + 2 * C_WIDTH
N_MEM = 256
X_HEADS = 4
X_HEAD_DIM = D_MODEL // X_HEADS
EPS = 1e-6

kernel_name = 'hybrid_stream_encoder_step'

F32 = jnp.float32


def rms_norm(x, g):
    xf = x.astype(F32)
    y = xf * lax.rsqrt(jnp.mean(xf * xf, axis=-1, keepdims=True) + EPS)
    return (y * g.astype(F32)).astype(x.dtype)


def hgrn2_scan(q, k, v, g, s0):
    bsz, L, H, _ = q.shape
    c = L if L <= CHUNK else CHUNK
    n = L // c
    causal = jnp.tril(jnp.ones((c, c), dtype=bool))[None, :, :, None, None]

    def to_chunks(t):
        return jnp.moveaxis(t.reshape(bsz, n, c, H, t.shape[-1]), 1, 0)

    def step(s, inp):
        qc, kc, vc, gc = inp
        G = jnp.cumsum(gc, axis=1)
        o = jnp.einsum('bthk,bhkv->bthv', qc * jnp.exp(G), s)
        diff = G[:, :, None] - G[:, None, :]
        decay = jnp.exp(jnp.where(causal, diff, -jnp.inf))
        att = jnp.einsum('bthk,btshk,bshk->bhts', qc, decay, kc)
        o = o + jnp.einsum('bhts,bshv->bthv', att, vc)
        GL = G[:, -1]
        s_new = jnp.exp(GL)[..., None] * s + jnp.einsum('bshk,bshv->bhkv', kc * jnp.exp(GL[:, None] - G), vc)
        return s_new, o

    s_fin, o = lax.scan(step, s0, (to_chunks(q), to_chunks(k), to_chunks(v), to_chunks(g)))
    return jnp.moveaxis(o, 0, 1).reshape(bsz, L, H, -1), s_fin


def sb_block(q, q_pos, k, v, k_pos):
    z = jnp.einsum('bqhd,bkhd->bhqk', q, k).astype(F32) * (B_HEAD_DIM ** -0.5)
    mask = (k_pos[None, :] < q_pos[:, None])[None, None]
    log_1m = jnp.where(mask, jax.nn.log_sigmoid(-z), 0.0)
    after = lax.cumsum(log_1m, axis=3, reverse=True) - log_1m
    w = jnp.where(mask, jnp.exp(jax.nn.log_sigmoid(z) + after), 0.0)
    return jnp.einsum('bhqk,bkhd->bqhd', w.astype(v.dtype), v)


def sb_attention(q, k, v, q_pos, k_pos):
    bsz, Lq, H, D = q.shape
    if Lq <= Q_BLOCK:
        return sb_block(q, q_pos, k, v, k_pos)
    nb = Lq // Q_BLOCK
    qb = jnp.moveaxis(q.reshape(bsz, nb, Q_BLOCK, H, D), 1, 0)
    pb = q_pos.reshape(nb, Q_BLOCK)
    ob = lax.map(lambda a: sb_block(a[0], a[1], k, v, k_pos), (qb, pb))
    return jnp.moveaxis(ob, 0, 1).reshape(bsz, Lq, H, D)


def pool_mix(u, prev, pos0, w_pool, pool_scale):
    bsz, L, _ = u.shape
    ext = jnp.concatenate([prev, u], axis=1)
    ef = ext.astype(F32)
    P = jnp.concatenate([jnp.zeros((bsz, 1, C_WIDTH), F32), jnp.cumsum(ef, axis=1)], axis=1)
    pos = pos0 + jnp.arange(L)
    outs = []
    for gi, w in enumerate(POOL_WINDOWS):
        lo_c, hi_c = gi * C_GROUP_DIM, (gi + 1) * C_GROUP_DIM
        hi = P[:, POOL_STATE + 1:POOL_STATE + 1 + L, lo_c:hi_c]
        lo = P[:, POOL_STATE + 1 - w:POOL_STATE + 1 - w + L, lo_c:hi_c]
        cnt = jnp.minimum(pos + 1, w).astype(F32)[None, :, None]
        outs.append((hi - lo) / cnt - ef[:, POOL_STATE:, lo_c:hi_c])
    d = jnp.concatenate(outs, axis=-1).reshape(bsz, L, C_GROUPS, C_GROUP_DIM)
    y = jnp.einsum('blgc,gcd->blgd', d, w_pool.astype(F32)).reshape(bsz, L, C_WIDTH) * pool_scale.astype(F32)
    return y.astype(u.dtype), ext[:, -POOL_STATE:]


def mem_kv(mem, g, w_k, w_v):
    bsz = mem.shape[0]
    m = rms_norm(mem, g)
    return ((m @ w_k).reshape(bsz, N_MEM, X_HEADS, X_HEAD_DIM),
            (m @ w_v).reshape(bsz, N_MEM, X_HEADS, X_HEAD_DIM))


def cross_attn(h, mem_k, mem_v, w_q, w_o):
    bsz, L, _ = h.shape
    q = (h @ w_q).reshape(bsz, L, X_HEADS, X_HEAD_DIM)
    s = jnp.einsum('blhd,bmhd->bhlm', q, mem_k.astype(h.dtype)).astype(F32) * (X_HEAD_DIM ** -0.5)
    p = jax.nn.softmax(s, axis=-1).astype(h.dtype)
    o = jnp.einsum('bhlm,bmhd->blhd', p, mem_v.astype(h.dtype)).reshape(bsz, L, D_MODEL)
    return o @ w_o


def trunk_layer(x, mem_k, mem_v, sb_k_past, sb_v_past, s0, pool_prev,
                n_pre, n_post, w_in, lb, onorm_g, w_pool, pool_scale, w_out,
                nx_pre, nx_post, w_xq, w_xo):
    bsz, L, _ = x.shape
    past = sb_k_past.shape[1]
    h = rms_norm(x, n_pre)
    split_idx = [int(i) for i in np.cumsum(SPLIT_SIZES)[:-1]]
    aq, af, ai, ag, bq, bk, bv, bg, cu, cg = jnp.split(h @ w_in, split_idx, axis=-1)

    fl = af.astype(F32)
    log_f = jnp.logaddexp(jnp.log(lb), jnp.log1p(-lb) + jax.nn.log_sigmoid(fl))
    k_a = (1.0 - lb) * jax.nn.sigmoid(-fl)
    q_a = jax.nn.silu(aq.astype(F32)) * (A_HEAD_DIM ** -0.5)
    ah = lambda t: t.reshape(bsz, L, A_HEADS, A_HEAD_DIM)
    o_a, s_new = hgrn2_scan(ah(q_a), ah(k_a), ah(ai.astype(F32)), ah(log_f), s0.astype(F32))
    o_a = o_a * lax.rsqrt(jnp.mean(o_a * o_a, axis=-1, keepdims=True) + EPS) * onorm_g.astype(F32).reshape(A_HEADS, A_HEAD_DIM)
    a_out = (o_a.reshape(bsz, L, A_WIDTH) * jax.nn.silu(ag.astype(F32))).astype(x.dtype)

    bh = lambda t: t.reshape(bsz, L, B_HEADS, B_HEAD_DIM)
    k_b, v_b = bh(bk), bh(bv)
    k_all = jnp.concatenate([sb_k_past.astype(x.dtype), k_b], axis=1)
    v_all = jnp.concatenate([sb_v_past.astype(x.dtype), v_b], axis=1)
    o_b = sb_attention(bh(bq), k_all, v_all, past + jnp.arange(L), jnp.arange(past + L))
    b_out = o_b.reshape(bsz, L, B_WIDTH) * jax.nn.silu(bg)

    c_pool, pool_new = pool_mix(cu, pool_prev.astype(x.dtype), past, w_pool, pool_scale)
    c_out = c_pool * jax.nn.silu(cg)

    mix = jnp.concatenate([a_out, b_out, c_out], axis=-1) @ w_out
    x = x + rms_norm(mix, n_post)
    x = x + rms_norm(cross_attn(rms_norm(x, nx_pre), mem_k, mem_v, w_xq, w_xo), nx_post)
    return x, k_b, v_b, s_new.astype(x.dtype), pool_new


def setup_inputs(seed: int = 0) -> dict:
    key = jax.random.key(seed)
    ks = jax.random.split(key, 24)

    def nrm(k, shape, scale=1.0):
        return jax.random.normal(k, shape, F32) * scale

    def gain(k, shape, s=0.05):
        return 1.0 + s * jax.random.normal(k, shape, F32)

    return {
        'x_prompt': nrm(ks[0], (BATCH, SEQ, D_MODEL)),
        'x_sample': nrm(ks[1], (DEC_BATCH, DEC_SEQ, D_MODEL)),
        'mem_prompt': nrm(ks[2], (BATCH, N_MEM, D_MODEL)),
        'cache_sb_k': nrm(ks[3], (DEPTH, DEC_BATCH, PAST_LEN, B_HEADS, B_HEAD_DIM)),
        'cache_sb_v': nrm(ks[4], (DEPTH, DEC_BATCH, PAST_LEN, B_HEADS, B_HEAD_DIM)),
        'state_hgrn': nrm(ks[5], (DEPTH, DEC_BATCH, A_HEADS, A_HEAD_DIM, A_HEAD_DIM), 0.3),
        'state_pool': nrm(ks[6], (DEPTH, DEC_BATCH, POOL_STATE, C_WIDTH)),
        'cache_mem_k': nrm(ks[7], (DEPTH, DEC_BATCH, N_MEM, X_HEADS, X_HEAD_DIM)),
        'cache_mem_v': nrm(ks[8], (DEPTH, DEC_BATCH, N_MEM, X_HEADS, X_HEAD_DIM)),
        'norm_mix_pre': gain(ks[9], (DEPTH, D_MODEL)),
        'norm_mix_post': gain(ks[10], (DEPTH, D_MODEL)),
        'w_in': nrm(ks[11], (DEPTH, D_MODEL, D_IN), D_MODEL ** -0.5),
        'hgrn_lb_logits': nrm(ks[12], (DEPTH, A_WIDTH), 0.5),
        'hgrn_onorm_g': gain(ks[13], (DEPTH, A_WIDTH)),
        'w_pool': nrm(ks[14], (DEPTH, C_GROUPS, C_GROUP_DIM, C_GROUP_DIM), C_GROUP_DIM ** -0.5),
        'pool_scale': gain(ks[15], (DEPTH, C_WIDTH), 0.1),
        'w_out': nrm(ks[16], (DEPTH, D_MIX, D_MODEL), D_MIX ** -0.5),
        'norm_x_pre': gain(ks[17], (DEPTH, D_MODEL)),
        'norm_x_post': gain(ks[18], (DEPTH, D_MODEL)),
        'norm_mem': gain(ks[19], (DEPTH, D_MODEL)),
        'w_xq': nrm(ks[20], (DEPTH, D_MODEL, D_MODEL), D_MODEL ** -0.5),
        'w_xk': nrm(ks[21], (DEPTH, D_MODEL, D_MODEL), D_MODEL ** -0.5),
        'w_xv': nrm(ks[22], (DEPTH, D_MODEL, D_MODEL), D_MODEL ** -0.5),
        'w_xo': nrm(ks[23], (DEPTH, D_MODEL, D_MODEL), D_MODEL ** -0.5),
    }


def reference(x_prompt, x_sample, mem_prompt, cache_sb_k, cache_sb_v, state_hgrn, state_pool,
              cache_mem_k, cache_mem_v, norm_mix_pre, norm_mix_post, w_in, hgrn_lb_logits,
              hgrn_onorm_g, w_pool, pool_scale, w_out, norm_x_pre, norm_x_post, norm_mem,
              w_xq, w_xk, w_xv, w_xo):
    lbs = jax.nn.softmax(hgrn_lb_logits.astype(F32), axis=0)
    lower_bounds = jnp.maximum(jnp.cumsum(lbs, axis=0) - lbs[0], 0.0)

    bsz_p = x_prompt.shape[0]
    empty_kv = jnp.zeros((bsz_p, 0, B_HEADS, B_HEAD_DIM), x_prompt.dtype)
    s_zero = jnp.zeros((bsz_p, A_HEADS, A_HEAD_DIM, A_HEAD_DIM), F32)
    pool_zero = jnp.zeros((bsz_p, POOL_STATE, C_WIDTH), x_prompt.dtype)

    xp, xs = x_prompt, x_sample
    p_k, p_v, p_s, p_pool, p_mk, p_mv = [], [], [], [], [], []
    s_k, s_v, s_s, s_pool = [], [], [], []
    for l in range(DEPTH):
        lw = (norm_mix_pre[l], norm_mix_post[l], w_in[l], lower_bounds[l], hgrn_onorm_g[l],
              w_pool[l], pool_scale[l], w_out[l], norm_x_pre[l], norm_x_post[l], w_xq[l], w_xo[l])
        mk, mv = mem_kv(mem_prompt, norm_mem[l], w_xk[l], w_xv[l])
        xp, kb, vb, sn, pn = trunk_layer(xp, mk, mv, empty_kv, empty_kv, s_zero, pool_zero, *lw)
        p_k.append(kb); p_v.append(vb); p_s.append(sn); p_pool.append(pn); p_mk.append(mk); p_mv.append(mv)
        xs, kb, vb, sn, pn = trunk_layer(xs, cache_mem_k[l], cache_mem_v[l], cache_sb_k[l], cache_sb_v[l],
                                         state_hgrn[l], state_pool[l], *lw)
        s_k.append(kb); s_v.append(vb); s_s.append(sn); s_pool.append(pn)

    return (xp, xs,
            jnp.stack(p_k), jnp.stack(p_v), jnp.stack(p_s), jnp.stack(p_pool), jnp.stack(p_mk), jnp.stack(p_mv),
            jnp.stack(s_k), jnp.stack(s_v), jnp.stack(s_s), jnp.stack(s_pool))
```

```python
import functools

import jax
import jax.numpy as jnp
from jax import lax
from jax.experimental import pallas as pl
from jax.experimental.pallas import tpu as pltpu

F32 = jnp.float32
BF16 = jnp.bfloat16
EPS = 1e-6

A_HEADS, A_HEAD_DIM = 4, 128
A_WIDTH = A_HEADS * A_HEAD_DIM
B_HEADS, B_HEAD_DIM = 4, 64
B_WIDTH = B_HEADS * B_HEAD_DIM
C_GROUPS, C_GROUP_DIM = 4, 64
C_WIDTH = C_GROUPS * C_GROUP_DIM
POOL_WINDOWS = (2, 4, 8, 16)
POOL_STATE = 15
X_HEADS = 4
CHUNK = 64
SUB = 16
KEY_BLOCK = 128
SUBLANES = 8
LANES = 128
VMEM_LIMIT = 56 * 1024 * 1024

NT_DIMS = (((1,), (1,)), ((), ()))
TN_DIMS = (((0,), (0,)), ((), ()))


def _rms(x, g):
    ms = jnp.mean(x * x, axis=-1, keepdims=True)
    return x * lax.rsqrt(ms + EPS) * g


def _silu(x):
    return x * (1.0 / (1.0 + jnp.exp(-x)))


def _dot(a, b):
    return jnp.dot(a, b, preferred_element_type=F32)


def _params(n_grid):
    return pltpu.CompilerParams(dimension_semantics=("arbitrary",) * n_grid,
                                vmem_limit_bytes=VMEM_LIMIT)


def _memkv_kernel(mem_ref, g_ref, wk_ref, wv_ref, k_ref, v_ref):
    m = _rms(mem_ref[...], g_ref[0]).astype(BF16)
    k_ref[0] = _dot(m, wk_ref[0])
    v_ref[0] = _dot(m, wv_ref[0])


def _memkv(mem2d, norm_mem, wk, wv):
    depth, d = norm_mem.shape
    n = mem2d.shape[0]
    out = jax.ShapeDtypeStruct((depth, n, d), F32)
    return pl.pallas_call(
        _memkv_kernel,
        grid=(depth,),
        in_specs=[pl.BlockSpec((n, d), lambda l: (0, 0)),
                  pl.BlockSpec((1, 1, d), lambda l: (l, 0, 0)),
                  pl.BlockSpec((1, d, d), lambda l: (l, 0, 0)),
                  pl.BlockSpec((1, d, d), lambda l: (l, 0, 0))],
        out_specs=[pl.BlockSpec((1, n, d), lambda l: (l, 0, 0))] * 2,
        out_shape=[out, out],
        compiler_params=_params(1),
        name="mem_kv",
    )(mem2d, norm_mem.reshape(depth, 1, d), wk, wv)


def _proj_kernel(x_ref, g_ref, w_ref, pa_ref, q_ref, k_ref, v_ref, bg_ref, pc_ref):
    h = _rms(x_ref[...], g_ref[...]).astype(BF16)
    a_end = 4 * A_WIDTH
    b_end = a_end + 4 * B_WIDTH
    pa_ref[...] = _dot(h, w_ref[:, 0:a_end])
    pb = _dot(h, w_ref[:, a_end:b_end])
    q_ref[...] = pb[:, 0:B_WIDTH]
    k_ref[...] = pb[:, B_WIDTH:2 * B_WIDTH]
    v_ref[...] = pb[:, 2 * B_WIDTH:3 * B_WIDTH]
    bg_ref[...] = pb[:, 3 * B_WIDTH:4 * B_WIDTH]
    pc_ref[...] = _dot(h, w_ref[:, b_end:])


def _proj(x2d, g, w_in):
    n, d = x2d.shape
    d_in = w_in.shape[1]
    tm = 512 if n % 512 == 0 else CHUNK
    row = lambda width: pl.BlockSpec((tm, width), lambda i: (i, 0))
    shapes = [4 * A_WIDTH, B_WIDTH, B_WIDTH, B_WIDTH, B_WIDTH, 2 * C_WIDTH]
    return pl.pallas_call(
        _proj_kernel,
        grid=(n // tm,),
        in_specs=[row(d), pl.BlockSpec((1, d), lambda i: (0, 0)),
                  pl.BlockSpec((d, d_in), lambda i: (0, 0))],
        out_specs=[row(w) for w in shapes],
        out_shape=[jax.ShapeDtypeStruct((n, w), F32) for w in shapes],
        compiler_params=_params(1),
        name="in_proj",
    )(x2d, g.reshape(1, d), w_in)


def _hgrn_kernel(pa_ref, loglb_ref, log1mlb_ref, omlb_ref, og_ref, s0_ref, a_ref, sout_ref, st_ref, *, nch):
    j = pl.program_id(1)

    @pl.when(j == 0)
    def _():
        for h in range(A_HEADS):
            st_ref[h] = s0_ref[0, h].T

    rowc = lax.broadcasted_iota(jnp.int32, (CHUNK, CHUNK), 0)
    colc = lax.broadcasted_iota(jnp.int32, (CHUNK, CHUNK), 1)
    tri = (rowc >= colc).astype(F32)
    col_sub = colc // SUB
    row_k = lax.broadcasted_iota(jnp.int32, (CHUNK, A_HEAD_DIM), 0)
    lane_d = lax.broadcasted_iota(jnp.int32, (SUB, CHUNK), 1)
    n_sub = CHUNK // SUB

    def chunk(c, carry):
        r0 = pl.multiple_of(c * CHUNK, CHUNK)
        aq = pa_ref[0, pl.ds(r0, CHUNK), 0:A_WIDTH]
        af = pa_ref[0, pl.ds(r0, CHUNK), A_WIDTH:2 * A_WIDTH]
        ai = pa_ref[0, pl.ds(r0, CHUNK), 2 * A_WIDTH:3 * A_WIDTH]
        ag = pa_ref[0, pl.ds(r0, CHUNK), 3 * A_WIDTH:4 * A_WIDTH]

        e = jnp.exp(-jnp.abs(af))
        logsig = jnp.minimum(af, 0.0) - jnp.log1p(e)
        r = 1.0 / (1.0 + e)
        sig_neg = jnp.where(af >= 0, e * r, r)
        la = loglb_ref[...]
        lb_ = log1mlb_ref[...] + logsig
        log_f = jnp.maximum(la, lb_) + jnp.log1p(jnp.exp(-jnp.abs(la - lb_)))
        k_all = omlb_ref[...] * sig_neg
        q_all = _silu(aq) * (A_HEAD_DIM ** -0.5)
        g_all = jnp.dot(tri, log_f, precision=lax.Precision.HIGHEST, preferred_element_type=F32)

        for h in range(A_HEADS):
            sl = slice(h * A_HEAD_DIM, (h + 1) * A_HEAD_DIM)
            gh, qh, kh = g_all[:, sl], q_all[:, sl], k_all[:, sl]
            vh = ai[:, sl].astype(BF16)
            gl = gh[CHUNK - 1:CHUNK, :]
            st = st_ref[h]

            o = lax.dot_general((qh * jnp.exp(gh)).astype(BF16), st.astype(BF16), NT_DIMS,
                                preferred_element_type=F32)

            parts, refrows = [], []
            for jsub in range(n_sub):
                ref = gh[SUB * jsub + SUB - 1:SUB * jsub + SUB, :]
                refrows.append(jnp.broadcast_to(ref, (SUB, A_HEAD_DIM)))
                if jsub < n_sub - 1:
                    later = row_k >= SUB * (jsub + 1)
                    parts.append(jnp.where(later, qh * jnp.exp(jnp.minimum(gh - ref, 0.0)), 0.0))
            q_rel = jnp.concatenate(parts, axis=0).astype(BF16)
            k_rel = (kh * jnp.exp(jnp.minimum(jnp.concatenate(refrows, axis=0) - gh, 0.0))).astype(BF16)
            rr = lax.dot_general(q_rel, k_rel, NT_DIMS, preferred_element_type=F32)
            att = jnp.zeros((CHUNK, CHUNK), F32)
            for jsub in range(n_sub - 1):
                att = jnp.where(col_sub == jsub, rr[jsub * CHUNK:(jsub + 1) * CHUNK, :], att)

            dparts = []
            for csub in range(n_sub):
                rs = slice(SUB * csub, SUB * (csub + 1))
                gc, qc, kc = gh[rs], qh[rs], kh[rs]
                accd = jnp.zeros((SUB, CHUNK), F32)
                for s in range(SUB):
                    dec = jnp.exp(jnp.minimum(gc - gc[s:s + 1, :], 0.0))
                    col = jnp.sum(dec * qc * kc[s:s + 1, :], axis=-1, keepdims=True)
                    accd = jnp.where(lane_d == SUB * csub + s, col, accd)
                dparts.append(accd)
            attd = jnp.concatenate(dparts, axis=0)
            att = att + jnp.where(rowc >= colc, attd, 0.0)
            o = o + _dot(att.astype(BF16), vh)

            on = o * lax.rsqrt(jnp.mean(o * o, axis=-1, keepdims=True) + EPS) * og_ref[:, sl]
            a_ref[0, pl.ds(r0, CHUNK), sl] = (on * _silu(ag[:, sl])).astype(BF16)

            kd = (kh * jnp.exp(gl - gh)).astype(BF16)
            st_ref[h] = st * jnp.exp(gl) + lax.dot_general(vh, kd, TN_DIMS, preferred_element_type=F32)
        return carry

    lax.fori_loop(0, nch, chunk, 0)

    @pl.when(j == pl.num_programs(1) - 1)
    def _():
        for h in range(A_HEADS):
            sout_ref[0, h] = st_ref[h].T


def _hgrn(pa, lb, onorm_g, s0):
    b, l, wa = pa.shape
    tl = 512 if l % 512 == 0 else CHUNK
    vec = lambda: pl.BlockSpec((1, A_WIDTH), lambda i, j: (0, 0))
    st_spec = pl.BlockSpec((1, A_HEADS, A_HEAD_DIM, A_HEAD_DIM), lambda i, j: (i, 0, 0, 0))
    lb = lb.reshape(1, A_WIDTH)
    return pl.pallas_call(
        functools.partial(_hgrn_kernel, nch=tl // CHUNK),
        grid=(b, l // tl),
        in_specs=[pl.BlockSpec((1, tl, wa), lambda i, j: (i, j, 0)), vec(), vec(), vec(), vec(), st_spec],
        out_specs=[pl.BlockSpec((1, tl, A_WIDTH), lambda i, j: (i, j, 0)), st_spec],
        out_shape=[jax.ShapeDtypeStruct((b, l, A_WIDTH), BF16),
                   jax.ShapeDtypeStruct((b, A_HEADS, A_HEAD_DIM, A_HEAD_DIM), F32)],
        scratch_shapes=[pltpu.VMEM((A_HEADS, A_HEAD_DIM, A_HEAD_DIM), F32)],
        compiler_params=_params(2),
        name="hgrn2",
    )(pa, jnp.log(lb), jnp.log1p(-lb), 1.0 - lb, onorm_g.reshape(1, A_WIDTH), s0)


def _sb_kernel(*refs, lq, n_past):
    if n_past:
        q_ref, k_ref, v_ref, g_ref, pk_ref, pv_ref, o_ref = refs[:7]
        kc_ref, vtc_ref, stk_ref, stv_ref, qtm_ref, carry_ref, acc_ref = refs[7:]
    else:
        q_ref, k_ref, v_ref, g_ref, o_ref = refs[:5]
        pk_ref = pv_ref = None
        kc_ref, vtc_ref, stk_ref, stv_ref, qtm_ref, carry_ref, acc_ref = refs[5:]
    j = pl.program_id(1)
    kb_rows = KEY_BLOCK
    run_len = kb_rows // SUBLANES
    width = B_WIDTH

    n_halves = width // LANES

    def permuted(ref):
        return jnp.concatenate(
            [jnp.concatenate([ref[c, pl.ds(jj, SUBLANES, stride=run_len), :] for jj in range(run_len)], axis=0)
             for c in range(n_halves)], axis=1)

    def stage(dst_ref, rows):
        for c in range(n_halves):
            if lq < kb_rows:
                dst_ref[c, lq:kb_rows, :] = jnp.zeros((kb_rows - lq, LANES), F32)
            dst_ref[c, 0:lq, :] = rows[:, c * LANES:(c + 1) * LANES]

    def staged(ref):
        return jnp.concatenate([ref[c] for c in range(n_halves)], axis=1)

    stage(stk_ref, k_ref[0])
    stage(stv_ref, v_ref[0])
    kc_ref[j] = permuted(stk_ref).astype(BF16)
    vtc_ref[j] = permuted(stv_ref).T.astype(BF16)

    stage(stk_ref, q_ref[0] * (B_HEAD_DIM ** -0.5))
    qt = staged(stk_ref).T
    row_head = lax.broadcasted_iota(jnp.int32, (width, kb_rows), 0) // B_HEAD_DIM
    qtm_ref[...] = jnp.concatenate([jnp.where(row_head == h, qt, 0.0) for h in range(B_HEADS)],
                                   axis=1).astype(BF16)

    row = lax.broadcasted_iota(jnp.int32, (kb_rows, kb_rows), 0)
    lane = lax.broadcasted_iota(jnp.int32, (kb_rows, kb_rows), 1)
    causal = (row % SUBLANES) * run_len + row // SUBLANES < lane
    sub = lax.broadcasted_iota(jnp.int32, (SUBLANES, kb_rows), 0)

    carry_ref[...] = jnp.ones(carry_ref.shape, F32)
    acc_ref[...] = jnp.zeros(acc_ref.shape, F32)

    def process(kb, vtb, masked):
        zt = _dot(kb, qtm_ref[...])
        for h in range(B_HEADS):
            ls = slice(h * kb_rows, (h + 1) * kb_rows)
            z = zt[:, ls]
            e = jnp.exp(-jnp.abs(z))
            r = 1.0 / (1.0 + e)
            er = e * r
            pos = z >= 0
            beta = jnp.where(pos, r, er)
            omb = jnp.where(pos, er, r)
            if masked:
                beta = jnp.where(causal, beta, 0.0)
                omb = jnp.where(causal, omb, 1.0)
            run = jnp.ones((SUBLANES, kb_rows), F32)
            excl = [None] * run_len
            for jj in reversed(range(run_len)):
                excl[jj] = run
                run = run * omb[jj * SUBLANES:(jj + 1) * SUBLANES]
            inc = run
            for d in (1, 2, 4):
                inc = jnp.where(sub + d < SUBLANES, inc * pltpu.roll(inc, SUBLANES - d, 0), inc)
            carry = carry_ref[:, ls]
            off = jnp.where(sub < SUBLANES - 1, pltpu.roll(inc, SUBLANES - 1, 0), 1.0) * carry
            w = jnp.concatenate([beta[jj * SUBLANES:(jj + 1) * SUBLANES] * (excl[jj] * off)
                                 for jj in range(run_len)], axis=0).astype(BF16)
            carry_ref[:, ls] = carry * jnp.broadcast_to(inc[0:1, :], (SUBLANES, kb_rows))
            hs = slice(h * B_HEAD_DIM, (h + 1) * B_HEAD_DIM)
            acc_ref[hs, :] += _dot(vtb[hs, :], w)

    def alive():
        return (jnp.max(carry_ref[...]) > 0.0).astype(jnp.int32)

    def loop(first, fetch):
        def cond(state):
            kb, live = state
            return jnp.logical_and(kb >= 0, live > 0)

        def body(state):
            kb, _ = state
            kblk, vtblk = fetch(kb)
            process(kblk, vtblk, False)
            return kb - 1, alive()

        lax.while_loop(cond, body, (first, alive()))

    process(kc_ref[j], vtc_ref[j], True)
    loop(j - 1, lambda kb: (kc_ref[kb], vtc_ref[kb]))

    if n_past:
        def fetch_past(kb):
            r0 = pl.multiple_of(kb * kb_rows, kb_rows)
            for c in range(n_halves):
                cs = slice(c * LANES, (c + 1) * LANES)
                stk_ref[c] = pk_ref[0, pl.ds(r0, kb_rows), cs]
                stv_ref[c] = pv_ref[0, pl.ds(r0, kb_rows), cs]
            return permuted(stk_ref).astype(BF16), permuted(stv_ref).T.astype(BF16)

        loop(jnp.int32(n_past - 1), fetch_past)

    o = acc_ref[...].T[0:lq, :]
    o_ref[0] = (o * _silu(g_ref[0])).astype(BF16)


def _sb(q, k, v, g, pk, pv):
    b, l, w = q.shape
    lq = min(KEY_BLOCK, l)
    nblk = l // lq
    p = 0 if pk is None else pk.shape[1]
    blk = pl.BlockSpec((1, lq, w), lambda i, j: (i, j, 0))
    in_specs = [blk] * 4
    args = [q, k, v, g]
    if p:
        in_specs += [pl.BlockSpec((1, p, w), lambda i, j: (i, 0, 0))] * 2
        args += [pk, pv]
    return pl.pallas_call(
        functools.partial(_sb_kernel, lq=lq, n_past=p // KEY_BLOCK),
        grid=(b, nblk),
        in_specs=in_specs,
        out_specs=blk,
        out_shape=jax.ShapeDtypeStruct((b, l, w), BF16),
        scratch_shapes=[pltpu.VMEM((nblk, KEY_BLOCK, w), BF16),
                        pltpu.VMEM((nblk, w, KEY_BLOCK), BF16),
                        pltpu.VMEM((w // LANES, KEY_BLOCK, LANES), F32),
                        pltpu.VMEM((w // LANES, KEY_BLOCK, LANES), F32),
                        pltpu.VMEM((w, B_HEADS * KEY_BLOCK), BF16),
                        pltpu.VMEM((SUBLANES, B_HEADS * KEY_BLOCK), F32),
                        pltpu.VMEM((w, KEY_BLOCK), F32)],
        compiler_params=_params(2),
        name="sb_attn",
    )(*args)


def _post_kernel(x_ref, a_ref, b_ref, pc_ref, pprev_ref, mk_ref, mv_ref, wpool_ref, pscale_ref,
                 woa_ref, wob_ref, woc_ref, npost_ref, nxpre_ref, wxq_ref, wxo_ref, nxpost_ref,
                 xo_ref, pnew_ref, ext_ref, mkb_ref, mvb_ref, *, tl, past):
    j = pl.program_id(1)
    halo = POOL_STATE + 1

    @pl.when(j == 0)
    def _():
        ext_ref[0:halo, :] = pprev_ref[0]
        mkb_ref[...] = mk_ref[0].astype(BF16)
        mvb_ref[...] = mv_ref[0].astype(BF16)

    cu = pc_ref[0, :, 0:C_WIDTH]
    cg = pc_ref[0, :, C_WIDTH:2 * C_WIDTH]
    ext_ref[halo:halo + tl, :] = cu
    grp = lax.broadcasted_iota(jnp.int32, (tl, C_WIDTH), 1) // C_GROUP_DIM
    pos = past + j * tl + lax.broadcasted_iota(jnp.int32, (tl, C_WIDTH), 0)
    acc = cu
    win = jnp.zeros((tl, C_WIDTH), F32)
    wlen = jnp.zeros((tl, C_WIDTH), jnp.int32)
    for d in range(1, POOL_WINDOWS[-1]):
        acc = acc + ext_ref[pl.ds(halo - d, tl), :]
        if d + 1 in POOL_WINDOWS:
            gi = POOL_WINDOWS.index(d + 1)
            win = jnp.where(grp == gi, acc, win)
            wlen = jnp.where(grp == gi, d + 1, wlen)
    cnt = jnp.minimum(pos + 1, wlen).astype(F32)
    dd = win / cnt - cu
    y = _dot(dd.astype(BF16), wpool_ref[...]) * pscale_ref[...]
    c_out = (y * _silu(cg)).astype(BF16)
    pnew_ref[0] = ext_ref[tl + 1:tl + halo, :]
    ext_ref[0:halo, :] = ext_ref[tl:tl + halo, :]

    mix = _dot(a_ref[0], woa_ref[...]) + _dot(b_ref[0], wob_ref[...]) + _dot(c_out, woc_ref[...])
    x1 = x_ref[0] + _rms(mix, npost_ref[...])

    hx = _rms(x1, nxpre_ref[...]).astype(BF16)
    q = _dot(hx, wxq_ref[...])
    d_head = q.shape[1] // X_HEADS
    heads = []
    for h in range(X_HEADS):
        hs = slice(h * d_head, (h + 1) * d_head)
        s = lax.dot_general(q[:, hs].astype(BF16), mkb_ref[:, hs], NT_DIMS,
                            preferred_element_type=F32) * (d_head ** -0.5)
        ex = jnp.exp(s - jnp.max(s, axis=-1, keepdims=True))
        p = ex / jnp.sum(ex, axis=-1, keepdims=True)
        heads.append(_dot(p.astype(BF16), mvb_ref[:, hs]).astype(BF16))
    xo = _dot(jnp.concatenate(heads, axis=1), wxo_ref[...])
    xo_ref[0] = x1 + _rms(xo, nxpost_ref[...])


def _post(x, a_out, b_out, pc, pool_prev, mk, mv, wts, past):
    b, l, d = x.shape
    n_mem = mk.shape[1]
    tl = 256 if l % 256 == 0 else CHUNK
    halo = POOL_STATE + 1
    pprev = jnp.concatenate([jnp.zeros((b, 1, C_WIDTH), F32), pool_prev], axis=1)
    tile = lambda w: pl.BlockSpec((1, tl, w), lambda i, j: (i, j, 0))
    per_seq = lambda r, w: pl.BlockSpec((1, r, w), lambda i, j: (i, 0, 0))
    full = lambda a: pl.BlockSpec(a.shape, lambda i, j: (0,) * a.ndim)
    return pl.pallas_call(
        functools.partial(_post_kernel, tl=tl, past=past),
        grid=(b, l // tl),
        in_specs=[tile(d), tile(A_WIDTH), tile(B_WIDTH), tile(2 * C_WIDTH), per_seq(halo, C_WIDTH),
                  per_seq(n_mem, d), per_seq(n_mem, d)] + [full(a) for a in wts],
        out_specs=[tile(d), per_seq(POOL_STATE, C_WIDTH)],
        out_shape=[jax.ShapeDtypeStruct((b, l, d), F32),
                   jax.ShapeDtypeStruct((b, POOL_STATE, C_WIDTH), F32)],
        scratch_shapes=[pltpu.VMEM((halo + tl, C_WIDTH), F32),
                        pltpu.VMEM((n_mem, d), BF16),
                        pltpu.VMEM((n_mem, d), BF16)],
        compiler_params=_params(2),
        name="mix_out",
    )(x, a_out, b_out, pc, pprev, mk, mv, *wts)


def _layer(x, mk, mv, pk, pv, s0, pool_prev, lw):
    b, l, d = x.shape
    n = b * l
    pa, bq, bk, bv, bg, pc = _proj(x.reshape(n, d), lw["n_pre"], lw["w_in"])
    a_out, s_new = _hgrn(pa.reshape(b, l, -1), lw["lb"], lw["onorm_g"], s0)
    seq = lambda t: t.reshape(b, l, -1)
    b_out = _sb(seq(bq), seq(bk), seq(bv), seq(bg), pk, pv)
    past = 0 if pk is None else pk.shape[1]
    x_new, pool_new = _post(x, a_out, b_out, seq(pc), pool_prev, mk, mv, lw["post"], past)
    heads = lambda t: t.reshape(b, l, B_HEADS, B_HEAD_DIM)
    return x_new, heads(bk), heads(bv), s_new, pool_new


def kernel(x_prompt, x_sample, mem_prompt, cache_sb_k, cache_sb_v, state_hgrn, state_pool, cache_mem_k, cache_mem_v, norm_mix_pre, norm_mix_post, w_in, hgrn_lb_logits, hgrn_onorm_g, w_pool, pool_scale, w_out, norm_x_pre, norm_x_post, norm_mem, w_xq, w_xk, w_xv, w_xo):
    depth, d = norm_mix_pre.shape
    bp = x_prompt.shape[0]
    bs, past = cache_sb_k.shape[1], cache_sb_k.shape[2]
    n_mem = mem_prompt.shape[1]

    lbs = jax.nn.softmax(hgrn_lb_logits.astype(F32), axis=0)
    lower_bounds = jnp.maximum(jnp.cumsum(lbs, axis=0) - lbs[0], 0.0)

    mk_all, mv_all = _memkv(mem_prompt.reshape(bp * n_mem, d), norm_mem, w_xk.astype(BF16), w_xv.astype(BF16))
    mk_all = mk_all.reshape(depth, bp, n_mem, d)
    mv_all = mv_all.reshape(depth, bp, n_mem, d)

    row = lambda v: v.reshape(1, -1)
    s_zero = jnp.zeros((bp, A_HEADS, A_HEAD_DIM, A_HEAD_DIM), F32)
    pool_zero = jnp.zeros((bp, POOL_STATE, C_WIDTH), F32)

    xp, xs = x_prompt, x_sample
    outs = [[] for _ in range(10)]
    for l in range(depth):
        wp = jnp.zeros((C_WIDTH, C_WIDTH), F32)
        for gi in range(C_GROUPS):
            gs = slice(gi * C_GROUP_DIM, (gi + 1) * C_GROUP_DIM)
            wp = wp.at[gs, gs].set(w_pool[l, gi])
        wo = w_out[l].astype(BF16)
        lw = {
            "n_pre": norm_mix_pre[l], "w_in": w_in[l].astype(BF16), "lb": lower_bounds[l],
            "onorm_g": hgrn_onorm_g[l],
            "post": (wp.astype(BF16), row(pool_scale[l]), wo[0:A_WIDTH], wo[A_WIDTH:A_WIDTH + B_WIDTH],
                     wo[A_WIDTH + B_WIDTH:], row(norm_mix_post[l]), row(norm_x_pre[l]),
                     w_xq[l].astype(BF16), w_xo[l].astype(BF16), row(norm_x_post[l])),
        }
        xp, kb, vb, sn, pn = _layer(xp, mk_all[l], mv_all[l], None, None, s_zero, pool_zero, lw)
        for o, t in zip(outs[:4], (kb, vb, sn, pn)):
            o.append(t)
        flat = lambda c: c.reshape(bs, past, B_WIDTH)
        xs, kb, vb, sn, pn = _layer(xs, cache_mem_k[l].reshape(bs, n_mem, d), cache_mem_v[l].reshape(bs, n_mem, d),
                                    flat(cache_sb_k[l]), flat(cache_sb_v[l]), state_hgrn[l], state_pool[l], lw)
        for o, t in zip(outs[6:], (kb, vb, sn, pn)):
            o.append(t)

    heads = lambda t: t.reshape(depth, bp, n_mem, X_HEADS, d // X_HEADS)
    p_k, p_v, p_s, p_pool = (jnp.stack(o) for o in outs[:4])
    s_k, s_v, s_s, s_pool = (jnp.stack(o) for o in outs[6:])
    return (xp, xs, p_k, p_v, p_s, p_pool, heads(mk_all), heads(mv_all), s_k, s_v, s_s, s_pool)
```

```python
import functools

import jax
import jax.numpy as jnp
from jax import lax
from jax.experimental import pallas as pl
from jax.experimental.pallas import tpu as pltpu

F32 = jnp.float32
BF16 = jnp.bfloat16
EPS = 1e-6

A_HEADS, A_HEAD_DIM = 4, 128
A_WIDTH = A_HEADS * A_HEAD_DIM
B_HEADS, B_HEAD_DIM = 4, 64
B_WIDTH = B_HEADS * B_HEAD_DIM
C_GROUPS, C_GROUP_DIM = 4, 64
C_WIDTH = C_GROUPS * C_GROUP_DIM
POOL_WINDOWS = (2, 4, 8, 16)
POOL_STATE = 15
X_HEADS = 4
CHUNK = 64
SUB = 16
KEY_BLOCK = 128
SUBLANES = 8
LANES = 128
VMEM_LIMIT = 56 * 1024 * 1024

NT_DIMS = (((1,), (1,)), ((), ()))
TN_DIMS = (((0,), (0,)), ((), ()))


def _rms(x, g):
    ms = jnp.mean(x * x, axis=-1, keepdims=True)
    return x * lax.rsqrt(ms + EPS) * g


def _silu(x):
    return x * (1.0 / (1.0 + jnp.exp(-x)))


def _dot(a, b):
    return jnp.dot(a, b, preferred_element_type=F32)


def _params(n_grid):
    return pltpu.CompilerParams(dimension_semantics=("arbitrary",) * n_grid,
                                vmem_limit_bytes=VMEM_LIMIT)


def _memkv_kernel(mem_ref, g_ref, wk_ref, wv_ref, k_ref, v_ref):
    nb, n_mem, d = mem_ref.shape
    dh = d // X_HEADS
    for b in range(nb):
        m = _rms(mem_ref[b], g_ref[0]).astype(BF16)
        k = _dot(m, wk_ref[0])
        v = _dot(m, wv_ref[0])
        for h in range(X_HEADS):
            k_ref[0, b, :, h, :] = k[:, h * dh:(h + 1) * dh]
            v_ref[0, b, :, h, :] = v[:, h * dh:(h + 1) * dh]


def _memkv(mem, norm_mem, wk, wv):
    depth, d = norm_mem.shape
    nb, n_mem, _ = mem.shape
    dh = d // X_HEADS
    out = jax.ShapeDtypeStruct((depth, nb, n_mem, X_HEADS, dh), F32)
    return pl.pallas_call(
        _memkv_kernel,
        grid=(depth,),
        in_specs=[pl.BlockSpec((nb, n_mem, d), lambda l: (0, 0, 0)),
                  pl.BlockSpec((1, 1, d), lambda l: (l, 0, 0)),
                  pl.BlockSpec((1, d, d), lambda l: (l, 0, 0)),
                  pl.BlockSpec((1, d, d), lambda l: (l, 0, 0))],
        out_specs=[pl.BlockSpec((1, nb, n_mem, X_HEADS, dh), lambda l: (l, 0, 0, 0, 0))] * 2,
        out_shape=[out, out],
        compiler_params=_params(1),
        name="mem_kv",
    )(mem, norm_mem.reshape(depth, 1, d), wk, wv)


def _proj_kernel(*refs, transposed, aliased):
    x_ref, g_ref, w_ref = refs[:3]
    outs = refs[3 + 2 * aliased:]
    pa_ref, q_ref, k_ref, v_ref, bg_ref, pc_ref = outs[:6]
    h = _rms(x_ref[...], g_ref[...]).astype(BF16)
    a_end = 4 * A_WIDTH
    b_end = a_end + 4 * B_WIDTH
    pa_ref[...] = _dot(h, w_ref[:, 0:a_end])
    pb = _dot(h, w_ref[:, a_end:b_end])
    q_ref[...] = pb[:, 0:B_WIDTH]
    k_ref[...] = pb[:, B_WIDTH:2 * B_WIDTH]
    v_ref[...] = pb[:, 2 * B_WIDTH:3 * B_WIDTH]
    bg_ref[...] = pb[:, 3 * B_WIDTH:4 * B_WIDTH]
    pc_ref[...] = _dot(h, w_ref[:, b_end:])
    if transposed:
        kt_ref, vt_ref = outs[6:]
        kt_ref[0, 0] = pb[:, B_WIDTH:2 * B_WIDTH].T
        vt_ref[0, 0] = pb[:, 2 * B_WIDTH:3 * B_WIDTH].T


def _proj(x2d, g, w_in, seq_len, layer, depth, kv_t):
    n, d = x2d.shape
    d_in = w_in.shape[1]
    tm = 512 if n % 512 == 0 else CHUNK
    transposed = seq_len % tm == 0 and tm % LANES == 0
    aliased = transposed and kv_t is not None
    row = lambda width: pl.BlockSpec((tm, width), lambda i: (i, 0))
    shapes = [4 * A_WIDTH, B_WIDTH, B_WIDTH, B_WIDTH, B_WIDTH, 2 * C_WIDTH]
    in_specs = [row(d), pl.BlockSpec((1, d), lambda i: (0, 0)), pl.BlockSpec((d, d_in), lambda i: (0, 0))]
    args = [x2d, g.reshape(1, d), w_in]
    out_specs = [row(w) for w in shapes]
    out_shape = [jax.ShapeDtypeStruct((n, w), F32) for w in shapes]
    aliases = {}
    if transposed:
        tiles = seq_len // tm
        t_spec = pl.BlockSpec((1, 1, B_WIDTH, tm), lambda i: (layer, i // tiles, 0, i % tiles))
        out_specs += [t_spec, t_spec]
        out_shape += [jax.ShapeDtypeStruct((depth, n // seq_len, B_WIDTH, seq_len), F32)] * 2
        if aliased:
            in_specs += [pl.BlockSpec(memory_space=pl.ANY)] * 2
            args += list(kv_t)
            aliases = {3: 6, 4: 7}
    res = pl.pallas_call(
        functools.partial(_proj_kernel, transposed=transposed, aliased=aliased),
        grid=(n // tm,),
        in_specs=in_specs,
        out_specs=out_specs,
        out_shape=out_shape,
        input_output_aliases=aliases,
        compiler_params=_params(1),
        name="in_proj",
    )(*args)
    return res[:6], (tuple(res[6:]) if transposed else None)


def _hgrn_kernel(pa_ref, loglb_ref, log1mlb_ref, omlb_ref, og_ref, s0_ref, a_ref, sout_ref, st_ref, *, nch):
    j = pl.program_id(1)

    @pl.when(j == 0)
    def _():
        for h in range(A_HEADS):
            st_ref[h] = s0_ref[0, h].T

    rowc = lax.broadcasted_iota(jnp.int32, (CHUNK, CHUNK), 0)
    colc = lax.broadcasted_iota(jnp.int32, (CHUNK, CHUNK), 1)
    tri = (rowc >= colc).astype(F32)
    col_sub = colc // SUB
    row_k = lax.broadcasted_iota(jnp.int32, (CHUNK, A_HEAD_DIM), 0)
    lane_d = lax.broadcasted_iota(jnp.int32, (SUB, CHUNK), 1)
    n_sub = CHUNK // SUB

    def chunk(c, carry):
        r0 = pl.multiple_of(c * CHUNK, CHUNK)
        aq = pa_ref[0, pl.ds(r0, CHUNK), 0:A_WIDTH]
        af = pa_ref[0, pl.ds(r0, CHUNK), A_WIDTH:2 * A_WIDTH]
        ai = pa_ref[0, pl.ds(r0, CHUNK), 2 * A_WIDTH:3 * A_WIDTH]
        ag = pa_ref[0, pl.ds(r0, CHUNK), 3 * A_WIDTH:4 * A_WIDTH]

        e = jnp.exp(-jnp.abs(af))
        logsig = jnp.minimum(af, 0.0) - jnp.log1p(e)
        r = 1.0 / (1.0 + e)
        sig_neg = jnp.where(af >= 0, e * r, r)
        la = loglb_ref[...]
        lb_ = log1mlb_ref[...] + logsig
        log_f = jnp.maximum(la, lb_) + jnp.log1p(jnp.exp(-jnp.abs(la - lb_)))
        k_all = omlb_ref[...] * sig_neg
        q_all = _silu(aq) * (A_HEAD_DIM ** -0.5)
        g_all = jnp.dot(tri, log_f, precision=lax.Precision.HIGHEST, preferred_element_type=F32)

        for h in range(A_HEADS):
            sl = slice(h * A_HEAD_DIM, (h + 1) * A_HEAD_DIM)
            gh, qh, kh = g_all[:, sl], q_all[:, sl], k_all[:, sl]
            vh = ai[:, sl].astype(BF16)
            gl = gh[CHUNK - 1:CHUNK, :]
            st = st_ref[h]

            o = lax.dot_general((qh * jnp.exp(gh)).astype(BF16), st.astype(BF16), NT_DIMS,
                                preferred_element_type=F32)

            parts, refrows = [], []
            for jsub in range(n_sub):
                ref = gh[SUB * jsub + SUB - 1:SUB * jsub + SUB, :]
                refrows.append(jnp.broadcast_to(ref, (SUB, A_HEAD_DIM)))
                if jsub < n_sub - 1:
                    later = row_k >= SUB * (jsub + 1)
                    parts.append(jnp.where(later, qh * jnp.exp(jnp.minimum(gh - ref, 0.0)), 0.0))
            q_rel = jnp.concatenate(parts, axis=0).astype(BF16)
            k_rel = (kh * jnp.exp(jnp.minimum(jnp.concatenate(refrows, axis=0) - gh, 0.0))).astype(BF16)
            rr = lax.dot_general(q_rel, k_rel, NT_DIMS, preferred_element_type=F32)
            att = jnp.zeros((CHUNK, CHUNK), F32)
            for jsub in range(n_sub - 1):
                att = jnp.where(col_sub == jsub, rr[jsub * CHUNK:(jsub + 1) * CHUNK, :], att)

            dparts = []
            for csub in range(n_sub):
                rs = slice(SUB * csub, SUB * (csub + 1))
                gc, qc, kc = gh[rs], qh[rs], kh[rs]
                accd = jnp.zeros((SUB, CHUNK), F32)
                for s in range(SUB):
                    dec = jnp.exp(jnp.minimum(gc - gc[s:s + 1, :], 0.0))
                    col = jnp.sum(dec * qc * kc[s:s + 1, :], axis=-1, keepdims=True)
                    accd = jnp.where(lane_d == SUB * csub + s, col, accd)
                dparts.append(accd)
            attd = jnp.concatenate(dparts, axis=0)
            att = att + jnp.where(rowc >= colc, attd, 0.0)
            o = o + _dot(att.astype(BF16), vh)

            on = o * lax.rsqrt(jnp.mean(o * o, axis=-1, keepdims=True) + EPS) * og_ref[:, sl]
            a_ref[0, pl.ds(r0, CHUNK), sl] = (on * _silu(ag[:, sl])).astype(BF16)

            kd = (kh * jnp.exp(gl - gh)).astype(BF16)
            st_ref[h] = st * jnp.exp(gl) + lax.dot_general(vh, kd, TN_DIMS, preferred_element_type=F32)
        return carry

    lax.fori_loop(0, nch, chunk, 0)

    @pl.when(j == pl.num_programs(1) - 1)
    def _():
        for h in range(A_HEADS):
            sout_ref[0, h] = st_ref[h].T


def _hgrn(pa, lb, onorm_g, s0):
    b, l, wa = pa.shape
    tl = 512 if l % 512 == 0 else CHUNK
    vec = lambda: pl.BlockSpec((1, A_WIDTH), lambda i, j: (0, 0))
    st_spec = pl.BlockSpec((1, A_HEADS, A_HEAD_DIM, A_HEAD_DIM), lambda i, j: (i, 0, 0, 0))
    lb = lb.reshape(1, A_WIDTH)
    return pl.pallas_call(
        functools.partial(_hgrn_kernel, nch=tl // CHUNK),
        grid=(b, l // tl),
        in_specs=[pl.BlockSpec((1, tl, wa), lambda i, j: (i, j, 0)), vec(), vec(), vec(), vec(), st_spec],
        out_specs=[pl.BlockSpec((1, tl, A_WIDTH), lambda i, j: (i, j, 0)), st_spec],
        out_shape=[jax.ShapeDtypeStruct((b, l, A_WIDTH), BF16),
                   jax.ShapeDtypeStruct((b, A_HEADS, A_HEAD_DIM, A_HEAD_DIM), F32)],
        scratch_shapes=[pltpu.VMEM((A_HEADS, A_HEAD_DIM, A_HEAD_DIM), F32)],
        compiler_params=_params(2),
        name="hgrn2",
    )(pa, jnp.log(lb), jnp.log1p(-lb), 1.0 - lb, onorm_g.reshape(1, A_WIDTH), s0)


def _sb_kernel(*refs, lq, n_past):
    if n_past:
        q_ref, k_ref, v_ref, g_ref, pk_ref, pv_ref, o_ref = refs[:7]
        kc_ref, vtc_ref, stk_ref, stv_ref, qtm_ref, carry_ref, acc_ref = refs[7:]
    else:
        q_ref, k_ref, v_ref, g_ref, o_ref = refs[:5]
        pk_ref = pv_ref = None
        kc_ref, vtc_ref, stk_ref, stv_ref, qtm_ref, carry_ref, acc_ref = refs[5:]
    j = pl.program_id(1)
    kb_rows = KEY_BLOCK
    run_len = kb_rows // SUBLANES
    width = B_WIDTH

    n_halves = width // LANES

    def permuted(ref):
        return jnp.concatenate(
            [jnp.concatenate([ref[c, pl.ds(jj, SUBLANES, stride=run_len), :] for jj in range(run_len)], axis=0)
             for c in range(n_halves)], axis=1)

    def stage(dst_ref, rows):
        for c in range(n_halves):
            if lq < kb_rows:
                dst_ref[c, lq:kb_rows, :] = jnp.zeros((kb_rows - lq, LANES), F32)
            dst_ref[c, 0:lq, :] = rows[:, c * LANES:(c + 1) * LANES]

    def staged(ref):
        return jnp.concatenate([ref[c] for c in range(n_halves)], axis=1)

    stage(stk_ref, k_ref[0])
    stage(stv_ref, v_ref[0])
    kc_ref[j] = permuted(stk_ref).astype(BF16)
    vtc_ref[j] = permuted(stv_ref).T.astype(BF16)

    stage(stk_ref, q_ref[0] * (B_HEAD_DIM ** -0.5))
    qt = staged(stk_ref).T
    row_head = lax.broadcasted_iota(jnp.int32, (width, kb_rows), 0) // B_HEAD_DIM
    qtm_ref[...] = jnp.concatenate([jnp.where(row_head == h, qt, 0.0) for h in range(B_HEADS)],
                                   axis=1).astype(BF16)

    row = lax.broadcasted_iota(jnp.int32, (kb_rows, kb_rows), 0)
    lane = lax.broadcasted_iota(jnp.int32, (kb_rows, kb_rows), 1)
    causal = (row % SUBLANES) * run_len + row // SUBLANES < lane
    sub = lax.broadcasted_iota(jnp.int32, (SUBLANES, kb_rows), 0)

    carry_ref[...] = jnp.ones(carry_ref.shape, F32)
    acc_ref[...] = jnp.zeros(acc_ref.shape, F32)

    def process(kb, vtb, masked):
        zt = _dot(kb, qtm_ref[...])
        for h in range(B_HEADS):
            ls = slice(h * kb_rows, (h + 1) * kb_rows)
            z = zt[:, ls]
            e = jnp.exp(-jnp.abs(z))
            r = 1.0 / (1.0 + e)
            er = e * r
            pos = z >= 0
            beta = jnp.where(pos, r, er)
            omb = jnp.where(pos, er, r)
            if masked:
                beta = jnp.where(causal, beta, 0.0)
                omb = jnp.where(causal, omb, 1.0)
            run = jnp.ones((SUBLANES, kb_rows), F32)
            excl = [None] * run_len
            for jj in reversed(range(run_len)):
                excl[jj] = run
                run = run * omb[jj * SUBLANES:(jj + 1) * SUBLANES]
            inc = run
            for d in (1, 2, 4):
                inc = jnp.where(sub + d < SUBLANES, inc * pltpu.roll(inc, SUBLANES - d, 0), inc)
            carry = carry_ref[:, ls]
            off = jnp.where(sub < SUBLANES - 1, pltpu.roll(inc, SUBLANES - 1, 0), 1.0) * carry
            w = jnp.concatenate([beta[jj * SUBLANES:(jj + 1) * SUBLANES] * (excl[jj] * off)
                                 for jj in range(run_len)], axis=0).astype(BF16)
            carry_ref[:, ls] = carry * jnp.broadcast_to(inc[0:1, :], (SUBLANES, kb_rows))
            hs = slice(h * B_HEAD_DIM, (h + 1) * B_HEAD_DIM)
            acc_ref[hs, :] += _dot(vtb[hs, :], w)

    def alive():
        return (jnp.max(carry_ref[...]) > 0.0).astype(jnp.int32)

    def loop(first, fetch):
        def cond(state):
            kb, live = state
            return jnp.logical_and(kb >= 0, live > 0)

        def body(state):
            kb, _ = state
            kblk, vtblk = fetch(kb)
            process(kblk, vtblk, False)
            return kb - 1, alive()

        lax.while_loop(cond, body, (first, alive()))

    process(kc_ref[j], vtc_ref[j], True)
    loop(j - 1, lambda kb: (kc_ref[kb], vtc_ref[kb]))

    for kb in reversed(range(n_past)):
        @pl.when(alive() > 0)
        def _():
            cols = slice(kb * kb_rows, (kb + 1) * kb_rows)
            kn = pk_ref[0, 0, :, cols].T
            vn = pv_ref[0, 0, :, cols].T
            for c in range(n_halves):
                stk_ref[c] = kn[:, c * LANES:(c + 1) * LANES]
                stv_ref[c] = vn[:, c * LANES:(c + 1) * LANES]
            process(permuted(stk_ref).astype(BF16), permuted(stv_ref).T.astype(BF16), False)

    o = acc_ref[...].T[0:lq, :]
    o_ref[0] = (o * _silu(g_ref[0])).astype(BF16)


def _sb(q, k, v, g, past_t, layer):
    b, l, w = q.shape
    lq = min(KEY_BLOCK, l)
    nblk = l // lq
    p = 0 if past_t is None else past_t[0].shape[3]
    blk = pl.BlockSpec((1, lq, w), lambda i, j: (i, j, 0))
    in_specs = [blk] * 4
    args = [q, k, v, g]
    if p:
        in_specs += [pl.BlockSpec((1, 1, w, p), lambda i, j: (layer, i, 0, 0))] * 2
        args += list(past_t)
    return pl.pallas_call(
        functools.partial(_sb_kernel, lq=lq, n_past=p // KEY_BLOCK),
        grid=(b, nblk),
        in_specs=in_specs,
        out_specs=blk,
        out_shape=jax.ShapeDtypeStruct((b, l, w), BF16),
        scratch_shapes=[pltpu.VMEM((nblk, KEY_BLOCK, w), BF16),
                        pltpu.VMEM((nblk, w, KEY_BLOCK), BF16),
                        pltpu.VMEM((w // LANES, KEY_BLOCK, LANES), F32),
                        pltpu.VMEM((w // LANES, KEY_BLOCK, LANES), F32),
                        pltpu.VMEM((w, B_HEADS * KEY_BLOCK), BF16),
                        pltpu.VMEM((SUBLANES, B_HEADS * KEY_BLOCK), F32),
                        pltpu.VMEM((w, KEY_BLOCK), F32)],
        compiler_params=_params(2),
        name="sb_attn",
    )(*args)


def _post_kernel(x_ref, a_ref, b_ref, pc_ref, pprev_ref, mk_ref, mv_ref, wpool_ref, pscale_ref,
                 woa_ref, wob_ref, woc_ref, npost_ref, nxpre_ref, wxq_ref, wxo_ref, nxpost_ref,
                 xo_ref, pnew_ref, ext_ref, mkb_ref, mvb_ref, *, tl, past):
    j = pl.program_id(1)
    halo = POOL_STATE + 1

    @pl.when(j == 0)
    def _():
        ext_ref[0:halo, :] = pprev_ref[0]
        dh = mk_ref.shape[-1]
        for h in range(X_HEADS):
            mkb_ref[:, h * dh:(h + 1) * dh] = mk_ref[0, 0, :, h, :].astype(BF16)
            mvb_ref[:, h * dh:(h + 1) * dh] = mv_ref[0, 0, :, h, :].astype(BF16)

    cu = pc_ref[0, :, 0:C_WIDTH]
    cg = pc_ref[0, :, C_WIDTH:2 * C_WIDTH]
    ext_ref[halo:halo + tl, :] = cu
    grp = lax.broadcasted_iota(jnp.int32, (tl, C_WIDTH), 1) // C_GROUP_DIM
    pos = past + j * tl + lax.broadcasted_iota(jnp.int32, (tl, C_WIDTH), 0)
    acc = cu
    win = jnp.zeros((tl, C_WIDTH), F32)
    wlen = jnp.zeros((tl, C_WIDTH), jnp.int32)
    for d in range(1, POOL_WINDOWS[-1]):
        acc = acc + ext_ref[pl.ds(halo - d, tl), :]
        if d + 1 in POOL_WINDOWS:
            gi = POOL_WINDOWS.index(d + 1)
            win = jnp.where(grp == gi, acc, win)
            wlen = jnp.where(grp == gi, d + 1, wlen)
    cnt = jnp.minimum(pos + 1, wlen).astype(F32)
    dd = win / cnt - cu
    y = _dot(dd.astype(BF16), wpool_ref[...]) * pscale_ref[...]
    c_out = (y * _silu(cg)).astype(BF16)
    pnew_ref[0] = ext_ref[tl + 1:tl + halo, :]
    ext_ref[0:halo, :] = ext_ref[tl:tl + halo, :]

    mix = _dot(a_ref[0], woa_ref[...]) + _dot(b_ref[0], wob_ref[...]) + _dot(c_out, woc_ref[...])
    x1 = x_ref[0] + _rms(mix, npost_ref[...])

    hx = _rms(x1, nxpre_ref[...]).astype(BF16)
    q = _dot(hx, wxq_ref[...])
    d_head = q.shape[1] // X_HEADS
    heads = []
    for h in range(X_HEADS):
        hs = slice(h * d_head, (h + 1) * d_head)
        s = lax.dot_general(q[:, hs].astype(BF16), mkb_ref[:, hs], NT_DIMS,
                            preferred_element_type=F32) * (d_head ** -0.5)
        ex = jnp.exp(s - jnp.max(s, axis=-1, keepdims=True))
        p = ex / jnp.sum(ex, axis=-1, keepdims=True)
        heads.append(_dot(p.astype(BF16), mvb_ref[:, hs]).astype(BF16))
    xo = _dot(jnp.concatenate(heads, axis=1), wxo_ref[...])
    xo_ref[0] = x1 + _rms(xo, nxpost_ref[...])


def _post(x, a_out, b_out, pc, pool_prev, mk, mv, wts, past, layer):
    b, l, d = x.shape
    n_mem = mk.shape[2]
    tl = 256 if l % 256 == 0 else CHUNK
    halo = POOL_STATE + 1
    pprev = jnp.concatenate([jnp.zeros((b, 1, C_WIDTH), F32), pool_prev], axis=1)
    tile = lambda w: pl.BlockSpec((1, tl, w), lambda i, j: (i, j, 0))
    per_seq = lambda r, w: pl.BlockSpec((1, r, w), lambda i, j: (i, 0, 0))
    full = lambda a: pl.BlockSpec(a.shape, lambda i, j: (0,) * a.ndim)
    mem_spec = pl.BlockSpec((1, 1) + mk.shape[2:], lambda i, j: (layer, i, 0, 0, 0))
    return pl.pallas_call(
        functools.partial(_post_kernel, tl=tl, past=past),
        grid=(b, l // tl),
        in_specs=[tile(d), tile(A_WIDTH), tile(B_WIDTH), tile(2 * C_WIDTH), per_seq(halo, C_WIDTH),
                  mem_spec, mem_spec] + [full(a) for a in wts],
        out_specs=[tile(d), per_seq(POOL_STATE, C_WIDTH)],
        out_shape=[jax.ShapeDtypeStruct((b, l, d), F32),
                   jax.ShapeDtypeStruct((b, POOL_STATE, C_WIDTH), F32)],
        scratch_shapes=[pltpu.VMEM((halo + tl, C_WIDTH), F32),
                        pltpu.VMEM((n_mem, d), BF16),
                        pltpu.VMEM((n_mem, d), BF16)],
        compiler_params=_params(2),
        name="mix_out",
    )(x, a_out, b_out, pc, pprev, mk, mv, *wts)


def _layer(x, mk, mv, past_t, s0, pool_prev, lw, layer, depth, kv_t):
    b, l, d = x.shape
    n = b * l
    (pa, bq, bk, bv, bg, pc), kv_t = _proj(x.reshape(n, d), lw["n_pre"], lw["w_in"], l, layer, depth, kv_t)
    a_out, s_new = _hgrn(pa.reshape(b, l, -1), lw["lb"], lw["onorm_g"], s0)
    seq = lambda t: t.reshape(b, l, -1)
    b_out = _sb(seq(bq), seq(bk), seq(bv), seq(bg), past_t, layer)
    past = 0 if past_t is None else past_t[0].shape[3]
    x_new, pool_new = _post(x, a_out, b_out, seq(pc), pool_prev, mk, mv, lw["post"], past, layer)
    heads = lambda t: t.reshape(b, l, B_HEADS, B_HEAD_DIM)
    return x_new, heads(bk), heads(bv), kv_t, s_new, pool_new


def kernel(x_prompt, x_sample, mem_prompt, cache_sb_k, cache_sb_v, state_hgrn, state_pool, cache_mem_k, cache_mem_v, norm_mix_pre, norm_mix_post, w_in, hgrn_lb_logits, hgrn_onorm_g, w_pool, pool_scale, w_out, norm_x_pre, norm_x_post, norm_mem, w_xq, w_xk, w_xv, w_xo):
    depth, d = norm_mix_pre.shape
    bp = x_prompt.shape[0]
    bs, past = cache_sb_k.shape[1], cache_sb_k.shape[2]
    n_mem = mem_prompt.shape[1]

    lbs = jax.nn.softmax(hgrn_lb_logits.astype(F32), axis=0)
    lower_bounds = jnp.maximum(jnp.cumsum(lbs, axis=0) - lbs[0], 0.0)

    mk_all, mv_all = _memkv(mem_prompt, norm_mem, w_xk.astype(BF16), w_xv.astype(BF16))

    def seq_minor(c):
        return jnp.transpose(c, (0, 1, 3, 4, 2)).reshape(depth, bs, B_WIDTH, past)

    def seq_major(t):
        dp, b, _, l = t.shape
        return jnp.transpose(t.reshape(dp, b, B_HEADS, B_HEAD_DIM, l), (0, 1, 4, 2, 3))

    past_t = (seq_minor(cache_sb_k), seq_minor(cache_sb_v))

    row = lambda v: v.reshape(1, -1)
    s_zero = jnp.zeros((bp, A_HEADS, A_HEAD_DIM, A_HEAD_DIM), F32)
    pool_zero = jnp.zeros((bp, POOL_STATE, C_WIDTH), F32)

    xp, xs = x_prompt, x_sample
    outs = [[] for _ in range(10)]
    kvt_p = kvt_s = None
    for l in range(depth):
        wp = jnp.zeros((C_WIDTH, C_WIDTH), F32)
        for gi in range(C_GROUPS):
            gs = slice(gi * C_GROUP_DIM, (gi + 1) * C_GROUP_DIM)
            wp = wp.at[gs, gs].set(w_pool[l, gi])
        wo = w_out[l].astype(BF16)
        lw = {
            "n_pre": norm_mix_pre[l], "w_in": w_in[l].astype(BF16), "lb": lower_bounds[l],
            "onorm_g": hgrn_onorm_g[l],
            "post": (wp.astype(BF16), row(pool_scale[l]), wo[0:A_WIDTH], wo[A_WIDTH:A_WIDTH + B_WIDTH],
                     wo[A_WIDTH + B_WIDTH:], row(norm_mix_post[l]), row(norm_x_pre[l]),
                     w_xq[l].astype(BF16), w_xo[l].astype(BF16), row(norm_x_post[l])),
        }
        xp, kb, vb, kvt_p, sn, pn = _layer(xp, mk_all, mv_all, None, s_zero, pool_zero, lw, l, depth, kvt_p)
        for o, t in zip(outs[:4], (kb, vb, sn, pn)):
            o.append(t)
        xs, kb, vb, kvt_s, sn, pn = _layer(xs, cache_mem_k, cache_mem_v, past_t, state_hgrn[l], state_pool[l],
                                           lw, l, depth, kvt_s)
        for o, t in zip(outs[6:], (kb, vb, sn, pn)):
            o.append(t)

    def new_kv(kvt, nat_k, nat_v):
        if kvt is not None:
            return seq_major(kvt[0]), seq_major(kvt[1])
        return jnp.stack(nat_k), jnp.stack(nat_v)

    p_k, p_v = new_kv(kvt_p, outs[0], outs[1])
    s_k, s_v = new_kv(kvt_s, outs[6], outs[7])
    p_s, p_pool = jnp.stack(outs[2]), jnp.stack(outs[3])
    s_s, s_pool = jnp.stack(outs[8]), jnp.stack(outs[9])
    return (xp, xs, p_k, p_v, p_s, p_pool, mk_all, mv_all, s_k, s_v, s_s, s_pool)
```

```python
import functools

import jax
import jax.numpy as jnp
from jax import lax
from jax.experimental import pallas as pl
from jax.experimental.pallas import tpu as pltpu

F32 = jnp.float32
BF16 = jnp.bfloat16
EPS = 1e-6

A_HEADS, A_HEAD_DIM = 4, 128
A_WIDTH = A_HEADS * A_HEAD_DIM
B_HEADS, B_HEAD_DIM = 4, 64
B_WIDTH = B_HEADS * B_HEAD_DIM
C_GROUPS, C_GROUP_DIM = 4, 64
C_WIDTH = C_GROUPS * C_GROUP_DIM
POOL_WINDOWS = (2, 4, 8, 16)
POOL_STATE = 15
X_HEADS = 4
CHUNK = 64
SUB = 16
SAFE_SPAN = 40.0
KEY_BLOCK = 128
SUBLANES = 8
LANES = 128
VMEM_LIMIT = 56 * 1024 * 1024

NT_DIMS = (((1,), (1,)), ((), ()))
TN_DIMS = (((0,), (0,)), ((), ()))


def _rms(x, g):
    ms = jnp.mean(x * x, axis=-1, keepdims=True)
    return x * lax.rsqrt(ms + EPS) * g


def _silu(x):
    return x * (1.0 / (1.0 + jnp.exp(-x)))


def _dot(a, b):
    return jnp.dot(a, b, preferred_element_type=F32)


def _params(n_grid):
    return pltpu.CompilerParams(dimension_semantics=("arbitrary",) * n_grid,
                                vmem_limit_bytes=VMEM_LIMIT)


def _memkv_kernel(mem_ref, g_ref, wk_ref, wv_ref, k_ref, v_ref):
    nb, n_mem, d = mem_ref.shape
    dh = d // X_HEADS
    for b in range(nb):
        m = _rms(mem_ref[b], g_ref[0]).astype(BF16)
        k = _dot(m, wk_ref[0])
        v = _dot(m, wv_ref[0])
        for h in range(X_HEADS):
            k_ref[0, b, :, h, :] = k[:, h * dh:(h + 1) * dh]
            v_ref[0, b, :, h, :] = v[:, h * dh:(h + 1) * dh]


def _memkv(mem, norm_mem, wk, wv):
    depth, d = norm_mem.shape
    nb, n_mem, _ = mem.shape
    dh = d // X_HEADS
    out = jax.ShapeDtypeStruct((depth, nb, n_mem, X_HEADS, dh), F32)
    return pl.pallas_call(
        _memkv_kernel,
        grid=(depth,),
        in_specs=[pl.BlockSpec((nb, n_mem, d), lambda l: (0, 0, 0)),
                  pl.BlockSpec((1, 1, d), lambda l: (l, 0, 0)),
                  pl.BlockSpec((1, d, d), lambda l: (l, 0, 0)),
                  pl.BlockSpec((1, d, d), lambda l: (l, 0, 0))],
        out_specs=[pl.BlockSpec((1, nb, n_mem, X_HEADS, dh), lambda l: (l, 0, 0, 0, 0))] * 2,
        out_shape=[out, out],
        compiler_params=_params(1),
        name="mem_kv",
    )(mem, norm_mem.reshape(depth, 1, d), wk, wv)


def _proj_kernel(*refs, transposed, aliased):
    x_ref, g_ref, w_ref = refs[:3]
    outs = refs[3 + 2 * aliased:]
    pa_ref, q_ref, k_ref, v_ref, bg_ref, pc_ref = outs[:6]
    h = _rms(x_ref[...], g_ref[...]).astype(BF16)
    a_end = 4 * A_WIDTH
    b_end = a_end + 4 * B_WIDTH
    pa_ref[...] = _dot(h, w_ref[:, 0:a_end])
    pb = _dot(h, w_ref[:, a_end:b_end])
    q_ref[...] = pb[:, 0:B_WIDTH]
    k_ref[...] = pb[:, B_WIDTH:2 * B_WIDTH]
    v_ref[...] = pb[:, 2 * B_WIDTH:3 * B_WIDTH]
    bg_ref[...] = pb[:, 3 * B_WIDTH:4 * B_WIDTH]
    pc_ref[...] = _dot(h, w_ref[:, b_end:])
    if transposed:
        kt_ref, vt_ref = outs[6:]
        kt_ref[0, 0] = pb[:, B_WIDTH:2 * B_WIDTH].T
        vt_ref[0, 0] = pb[:, 2 * B_WIDTH:3 * B_WIDTH].T


def _proj(x2d, g, w_in, seq_len, layer, depth, kv_t):
    n, d = x2d.shape
    d_in = w_in.shape[1]
    tm = 512 if n % 512 == 0 else CHUNK
    transposed = seq_len % tm == 0 and tm % LANES == 0
    aliased = transposed and kv_t is not None
    row = lambda width: pl.BlockSpec((tm, width), lambda i: (i, 0))
    shapes = [4 * A_WIDTH, B_WIDTH, B_WIDTH, B_WIDTH, B_WIDTH, 2 * C_WIDTH]
    in_specs = [row(d), pl.BlockSpec((1, d), lambda i: (0, 0)), pl.BlockSpec((d, d_in), lambda i: (0, 0))]
    args = [x2d, g.reshape(1, d), w_in]
    out_specs = [row(w) for w in shapes]
    out_shape = [jax.ShapeDtypeStruct((n, w), F32) for w in shapes]
    aliases = {}
    if transposed:
        tiles = seq_len // tm
        t_spec = pl.BlockSpec((1, 1, B_WIDTH, tm), lambda i: (layer, i // tiles, 0, i % tiles))
        out_specs += [t_spec, t_spec]
        out_shape += [jax.ShapeDtypeStruct((depth, n // seq_len, B_WIDTH, seq_len), F32)] * 2
        if aliased:
            in_specs += [pl.BlockSpec(memory_space=pl.ANY)] * 2
            args += list(kv_t)
            aliases = {3: 6, 4: 7}
    res = pl.pallas_call(
        functools.partial(_proj_kernel, transposed=transposed, aliased=aliased),
        grid=(n // tm,),
        in_specs=in_specs,
        out_specs=out_specs,
        out_shape=out_shape,
        input_output_aliases=aliases,
        compiler_params=_params(1),
        name="in_proj",
    )(*args)
    return res[:6], (tuple(res[6:]) if transposed else None)


def _hgrn_kernel(pa_ref, loglb_ref, log1mlb_ref, omlb_ref, og_ref, s0_ref, a_ref, sout_ref, st_ref, *, nch):
    j = pl.program_id(1)

    @pl.when(j == 0)
    def _():
        for h in range(A_HEADS):
            st_ref[h] = s0_ref[0, h].T

    rowc = lax.broadcasted_iota(jnp.int32, (CHUNK, CHUNK), 0)
    colc = lax.broadcasted_iota(jnp.int32, (CHUNK, CHUNK), 1)
    tri = (rowc >= colc).astype(F32)
    col_sub = colc // SUB
    row_k = lax.broadcasted_iota(jnp.int32, (CHUNK, A_HEAD_DIM), 0)
    lane_d = lax.broadcasted_iota(jnp.int32, (SUB, CHUNK), 1)
    n_sub = CHUNK // SUB

    def chunk(c, carry):
        r0 = pl.multiple_of(c * CHUNK, CHUNK)
        aq = pa_ref[0, pl.ds(r0, CHUNK), 0:A_WIDTH]
        af = pa_ref[0, pl.ds(r0, CHUNK), A_WIDTH:2 * A_WIDTH]
        ai = pa_ref[0, pl.ds(r0, CHUNK), 2 * A_WIDTH:3 * A_WIDTH]
        ag = pa_ref[0, pl.ds(r0, CHUNK), 3 * A_WIDTH:4 * A_WIDTH]

        e = jnp.exp(-jnp.abs(af))
        ope = 1.0 + e
        logsig = jnp.minimum(af, 0.0) - jnp.log(ope)
        r = 1.0 / ope
        sig_neg = jnp.where(af >= 0, e * r, r)
        la = loglb_ref[...]
        lb_ = log1mlb_ref[...] + logsig
        log_f = jnp.maximum(la, lb_) + jnp.log(1.0 + jnp.exp(-jnp.abs(la - lb_)))
        k_all = omlb_ref[...] * sig_neg
        q_all = _silu(aq) * (A_HEAD_DIM ** -0.5)
        g_all = jnp.dot(tri, log_f, precision=lax.Precision.HIGHEST, preferred_element_type=F32)
        qg_all = q_all * jnp.exp(g_all)

        starts = [jnp.zeros((1, A_WIDTH), F32)] + [g_all[SUB * s - 1:SUB * s, :] for s in range(1, n_sub)]
        spans = [starts[s] - g_all[SUB * s + SUB - 1:SUB * s + SUB, :] for s in range(n_sub)]
        small_decay = jnp.max(jnp.concatenate(spans, axis=0)) < SAFE_SPAN

        def att_factored(h):
            sl = slice(h * A_HEAD_DIM, (h + 1) * A_HEAD_DIM)
            gh, qh, kh = g_all[:, sl], q_all[:, sl], k_all[:, sl]
            parts, krows = [qg_all[:, sl]], []
            for s in range(n_sub):
                rs = slice(SUB * s, SUB * (s + 1))
                st = starts[s][:, sl]
                krows.append(kh[rs] * jnp.exp(st - gh[rs]))
                if s:
                    parts.append(jnp.where(row_k >= SUB * s, qh * jnp.exp(jnp.minimum(gh - st, 0.0)), 0.0))
            rr = lax.dot_general(jnp.concatenate(parts, axis=0).astype(BF16),
                                 jnp.concatenate(krows, axis=0).astype(BF16), NT_DIMS,
                                 preferred_element_type=F32)
            att = rr[0:CHUNK, :]
            for s in range(1, n_sub):
                att = jnp.where(col_sub == s, rr[s * CHUNK:(s + 1) * CHUNK, :], att)
            return jnp.where(rowc >= colc, att, 0.0)

        def att_exact(h):
            sl = slice(h * A_HEAD_DIM, (h + 1) * A_HEAD_DIM)
            gh, qh, kh = g_all[:, sl], q_all[:, sl], k_all[:, sl]
            parts, refrows = [], []
            for jsub in range(n_sub):
                ref = gh[SUB * jsub + SUB - 1:SUB * jsub + SUB, :]
                refrows.append(jnp.broadcast_to(ref, (SUB, A_HEAD_DIM)))
                if jsub < n_sub - 1:
                    later = row_k >= SUB * (jsub + 1)
                    parts.append(jnp.where(later, qh * jnp.exp(jnp.minimum(gh - ref, 0.0)), 0.0))
            q_rel = jnp.concatenate(parts, axis=0).astype(BF16)
            k_rel = (kh * jnp.exp(jnp.minimum(jnp.concatenate(refrows, axis=0) - gh, 0.0))).astype(BF16)
            rr = lax.dot_general(q_rel, k_rel, NT_DIMS, preferred_element_type=F32)
            att = jnp.zeros((CHUNK, CHUNK), F32)
            for jsub in range(n_sub - 1):
                att = jnp.where(col_sub == jsub, rr[jsub * CHUNK:(jsub + 1) * CHUNK, :], att)

            dparts = []
            for csub in range(n_sub):
                rs = slice(SUB * csub, SUB * (csub + 1))
                gc, qc, kc = gh[rs], qh[rs], kh[rs]
                accd = jnp.zeros((SUB, CHUNK), F32)
                for s in range(SUB):
                    dec = jnp.exp(jnp.minimum(gc - gc[s:s + 1, :], 0.0))
                    col = jnp.sum(dec * qc * kc[s:s + 1, :], axis=-1, keepdims=True)
                    accd = jnp.where(lane_d == SUB * csub + s, col, accd)
                dparts.append(accd)
            return att + jnp.where(rowc >= colc, jnp.concatenate(dparts, axis=0), 0.0)

        atts = lax.cond(small_decay,
                        lambda: tuple(att_factored(h) for h in range(A_HEADS)),
                        lambda: tuple(att_exact(h) for h in range(A_HEADS)))

        for h in range(A_HEADS):
            sl = slice(h * A_HEAD_DIM, (h + 1) * A_HEAD_DIM)
            gh, kh = g_all[:, sl], k_all[:, sl]
            vh = ai[:, sl].astype(BF16)
            gl = gh[CHUNK - 1:CHUNK, :]
            st = st_ref[h]

            o = lax.dot_general(qg_all[:, sl].astype(BF16), st.astype(BF16), NT_DIMS,
                                preferred_element_type=F32)
            o = o + _dot(atts[h].astype(BF16), vh)

            on = o * lax.rsqrt(jnp.mean(o * o, axis=-1, keepdims=True) + EPS) * og_ref[:, sl]
            a_ref[0, pl.ds(r0, CHUNK), sl] = (on * _silu(ag[:, sl])).astype(BF16)

            kd = (kh * jnp.exp(gl - gh)).astype(BF16)
            st_ref[h] = st * jnp.exp(gl) + lax.dot_general(vh, kd, TN_DIMS, preferred_element_type=F32)
        return carry

    lax.fori_loop(0, nch, chunk, 0)

    @pl.when(j == pl.num_programs(1) - 1)
    def _():
        for h in range(A_HEADS):
            sout_ref[0, h] = st_ref[h].T


def _hgrn(pa, lb, onorm_g, s0):
    b, l, wa = pa.shape
    tl = 512 if l % 512 == 0 else CHUNK
    vec = lambda: pl.BlockSpec((1, A_WIDTH), lambda i, j: (0, 0))
    st_spec = pl.BlockSpec((1, A_HEADS, A_HEAD_DIM, A_HEAD_DIM), lambda i, j: (i, 0, 0, 0))
    lb = lb.reshape(1, A_WIDTH)
    return pl.pallas_call(
        functools.partial(_hgrn_kernel, nch=tl // CHUNK),
        grid=(b, l // tl),
        in_specs=[pl.BlockSpec((1, tl, wa), lambda i, j: (i, j, 0)), vec(), vec(), vec(), vec(), st_spec],
        out_specs=[pl.BlockSpec((1, tl, A_WIDTH), lambda i, j: (i, j, 0)), st_spec],
        out_shape=[jax.ShapeDtypeStruct((b, l, A_WIDTH), BF16),
                   jax.ShapeDtypeStruct((b, A_HEADS, A_HEAD_DIM, A_HEAD_DIM), F32)],
        scratch_shapes=[pltpu.VMEM((A_HEADS, A_HEAD_DIM, A_HEAD_DIM), F32)],
        compiler_params=_params(2),
        name="hgrn2",
    )(pa, jnp.log(lb), jnp.log1p(-lb), 1.0 - lb, onorm_g.reshape(1, A_WIDTH), s0)


def _sb_kernel(*refs, lq, n_past):
    if n_past:
        q_ref, k_ref, v_ref, g_ref, pk_ref, pv_ref, o_ref = refs[:7]
        kc_ref, vtc_ref, stk_ref, stv_ref, qtm_ref, carry_ref, acc_ref = refs[7:]
    else:
        q_ref, k_ref, v_ref, g_ref, o_ref = refs[:5]
        pk_ref = pv_ref = None
        kc_ref, vtc_ref, stk_ref, stv_ref, qtm_ref, carry_ref, acc_ref = refs[5:]
    j = pl.program_id(1)
    kb_rows = KEY_BLOCK
    run_len = kb_rows // SUBLANES
    width = B_WIDTH

    n_halves = width // LANES

    def permuted(ref):
        return jnp.concatenate(
            [jnp.concatenate([ref[c, pl.ds(jj, SUBLANES, stride=run_len), :] for jj in range(run_len)], axis=0)
             for c in range(n_halves)], axis=1)

    def stage(dst_ref, rows):
        for c in range(n_halves):
            if lq < kb_rows:
                dst_ref[c, lq:kb_rows, :] = jnp.zeros((kb_rows - lq, LANES), F32)
            dst_ref[c, 0:lq, :] = rows[:, c * LANES:(c + 1) * LANES]

    def staged(ref):
        return jnp.concatenate([ref[c] for c in range(n_halves)], axis=1)

    stage(stk_ref, k_ref[0])
    stage(stv_ref, v_ref[0])
    kc_ref[j] = permuted(stk_ref).astype(BF16)
    vtc_ref[j] = permuted(stv_ref).T.astype(BF16)

    stage(stk_ref, q_ref[0] * (B_HEAD_DIM ** -0.5))
    qt = staged(stk_ref).T
    row_head = lax.broadcasted_iota(jnp.int32, (width, kb_rows), 0) // B_HEAD_DIM
    qtm_ref[...] = jnp.concatenate([jnp.where(row_head == h, qt, 0.0) for h in range(B_HEADS)],
                                   axis=1).astype(BF16)

    row = lax.broadcasted_iota(jnp.int32, (kb_rows, kb_rows), 0)
    lane = lax.broadcasted_iota(jnp.int32, (kb_rows, kb_rows), 1)
    causal = (row % SUBLANES) * run_len + row // SUBLANES < lane
    sub = lax.broadcasted_iota(jnp.int32, (SUBLANES, kb_rows), 0)

    carry_ref[...] = jnp.ones(carry_ref.shape, F32)
    acc_ref[...] = jnp.zeros(acc_ref.shape, F32)

    def process(kb, vtb, masked):
        zt = _dot(kb, qtm_ref[...])
        for h in range(B_HEADS):
            ls = slice(h * kb_rows, (h + 1) * kb_rows)
            z = zt[:, ls]
            e = jnp.exp(-jnp.abs(z))
            r = 1.0 / (1.0 + e)
            er = e * r
            pos = z >= 0
            beta = jnp.where(pos, r, er)
            omb = jnp.where(pos, er, r)
            if masked:
                beta = jnp.where(causal, beta, 0.0)
                omb = jnp.where(causal, omb, 1.0)
            run = jnp.ones((SUBLANES, kb_rows), F32)
            excl = [None] * run_len
            for jj in reversed(range(run_len)):
                excl[jj] = run
                run = run * omb[jj * SUBLANES:(jj + 1) * SUBLANES]
            inc = run
            for d in (1, 2, 4):
                inc = jnp.where(sub + d < SUBLANES, inc * pltpu.roll(inc, SUBLANES - d, 0), inc)
            carry = carry_ref[:, ls]
            off = jnp.where(sub < SUBLANES - 1, pltpu.roll(inc, SUBLANES - 1, 0), 1.0) * carry
            w = jnp.concatenate([beta[jj * SUBLANES:(jj + 1) * SUBLANES] * (excl[jj] * off)
                                 for jj in range(run_len)], axis=0).astype(BF16)
            carry_ref[:, ls] = carry * jnp.broadcast_to(inc[0:1, :], (SUBLANES, kb_rows))
            hs = slice(h * B_HEAD_DIM, (h + 1) * B_HEAD_DIM)
            acc_ref[hs, :] += _dot(vtb[hs, :], w)

    def alive():
        return (jnp.max(carry_ref[...]) > 0.0).astype(jnp.int32)

    def loop(first, fetch):
        def cond(state):
            kb, live = state
            return jnp.logical_and(kb >= 0, live > 0)

        def body(state):
            kb, _ = state
            kblk, vtblk = fetch(kb)
            process(kblk, vtblk, False)
            return kb - 1, alive()

        lax.while_loop(cond, body, (first, alive()))

    process(kc_ref[j], vtc_ref[j], True)
    loop(j - 1, lambda kb: (kc_ref[kb], vtc_ref[kb]))

    for kb in reversed(range(n_past)):
        @pl.when(alive() > 0)
        def _():
            cols = slice(kb * kb_rows, (kb + 1) * kb_rows)
            kn = pk_ref[0, 0, :, cols].T
            vn = pv_ref[0, 0, :, cols].T
            for c in range(n_halves):
                stk_ref[c] = kn[:, c * LANES:(c + 1) * LANES]
                stv_ref[c] = vn[:, c * LANES:(c + 1) * LANES]
            process(permuted(stk_ref).astype(BF16), permuted(stv_ref).T.astype(BF16), False)

    o = acc_ref[...].T[0:lq, :]
    o_ref[0] = (o * _silu(g_ref[0])).astype(BF16)


def _sb(q, k, v, g, past_t, layer):
    b, l, w = q.shape
    lq = min(KEY_BLOCK, l)
    nblk = l // lq
    p = 0 if past_t is None else past_t[0].shape[3]
    blk = pl.BlockSpec((1, lq, w), lambda i, j: (i, j, 0))
    in_specs = [blk] * 4
    args = [q, k, v, g]
    if p:
        in_specs += [pl.BlockSpec((1, 1, w, p), lambda i, j: (layer, i, 0, 0))] * 2
        args += list(past_t)
    return pl.pallas_call(
        functools.partial(_sb_kernel, lq=lq, n_past=p // KEY_BLOCK),
        grid=(b, nblk),
        in_specs=in_specs,
        out_specs=blk,
        out_shape=jax.ShapeDtypeStruct((b, l, w), BF16),
        scratch_shapes=[pltpu.VMEM((nblk, KEY_BLOCK, w), BF16),
                        pltpu.VMEM((nblk, w, KEY_BLOCK), BF16),
                        pltpu.VMEM((w // LANES, KEY_BLOCK, LANES), F32),
                        pltpu.VMEM((w // LANES, KEY_BLOCK, LANES), F32),
                        pltpu.VMEM((w, B_HEADS * KEY_BLOCK), BF16),
                        pltpu.VMEM((SUBLANES, B_HEADS * KEY_BLOCK), F32),
                        pltpu.VMEM((w, KEY_BLOCK), F32)],
        compiler_params=_params(2),
        name="sb_attn",
    )(*args)


def _post_kernel(x_ref, a_ref, b_ref, pc_ref, pprev_ref, mk_hbm, mv_hbm, wpool_ref, pscale_ref,
                 woa_ref, wob_ref, woc_ref, npost_ref, nxpre_ref, wxq_ref, wxo_ref, nxpost_ref,
                 xo_ref, pnew_ref, ext_ref, kv_ref, mkb_ref, mvb_ref, sem_ref, *, nb, tl, past, layer):
    i = pl.program_id(0)
    j = pl.program_id(1)
    n_groups = pl.num_programs(0)
    halo = POOL_STATE + 1
    dh = kv_ref.shape[-1]

    def kv_copies(group, slot):
        return [pltpu.make_async_copy(src.at[layer, group * nb + n, :, h, :], kv_ref.at[slot, t, n, h],
                                      sem_ref.at[slot, t])
                for t, src in enumerate((mk_hbm, mv_hbm)) for n in range(nb) for h in range(X_HEADS)]

    @pl.when(j == 0)
    def _():
        slot = i % 2

        @pl.when(i == 0)
        def _():
            for cp in kv_copies(0, 0):
                cp.start()

        @pl.when(i + 1 < n_groups)
        def _():
            for cp in kv_copies(i + 1, 1 - slot):
                cp.start()

        for cp in kv_copies(i, slot):
            cp.wait()
        for n in range(nb):
            ext_ref[n, 0:halo, :] = pprev_ref[n]
            for h in range(X_HEADS):
                mkb_ref[n, :, h * dh:(h + 1) * dh] = kv_ref[slot, 0, n, h].astype(BF16)
                mvb_ref[n, :, h * dh:(h + 1) * dh] = kv_ref[slot, 1, n, h].astype(BF16)

    grp = lax.broadcasted_iota(jnp.int32, (tl, C_WIDTH), 1) // C_GROUP_DIM
    pos = past + j * tl + lax.broadcasted_iota(jnp.int32, (tl, C_WIDTH), 0)
    dds = []
    for n in range(nb):
        cu = pc_ref[n, :, 0:C_WIDTH]
        ext_ref[n, halo:halo + tl, :] = cu
        acc = cu
        win = jnp.zeros((tl, C_WIDTH), F32)
        wlen = jnp.zeros((tl, C_WIDTH), jnp.int32)
        for d in range(1, POOL_WINDOWS[-1]):
            acc = acc + ext_ref[n, pl.ds(halo - d, tl), :]
            if d + 1 in POOL_WINDOWS:
                gi = POOL_WINDOWS.index(d + 1)
                win = jnp.where(grp == gi, acc, win)
                wlen = jnp.where(grp == gi, d + 1, wlen)
        cnt = jnp.minimum(pos + 1, wlen).astype(F32)
        dds.append((win / cnt - cu).astype(BF16))
        pnew_ref[n] = ext_ref[n, tl + 1:tl + halo, :]
        ext_ref[n, 0:halo, :] = ext_ref[n, tl:tl + halo, :]

    rows = nb * tl
    flat = lambda ref: ref[...].reshape(rows, ref.shape[-1])
    pc = flat(pc_ref)
    y = _dot(jnp.concatenate(dds, axis=0), wpool_ref[...]) * pscale_ref[...]
    c_out = (y * _silu(pc[:, C_WIDTH:2 * C_WIDTH])).astype(BF16)
    mix = _dot(flat(a_ref), woa_ref[...]) + _dot(flat(b_ref), wob_ref[...]) + _dot(c_out, woc_ref[...])
    x1 = flat(x_ref) + _rms(mix, npost_ref[...])

    hx = _rms(x1, nxpre_ref[...]).astype(BF16)
    q = _dot(hx, wxq_ref[...]).astype(BF16)
    seqs = []
    for n in range(nb):
        heads = []
        for h in range(X_HEADS):
            hs = slice(h * dh, (h + 1) * dh)
            s = lax.dot_general(q[n * tl:(n + 1) * tl, hs], mkb_ref[n, :, hs], NT_DIMS,
                                preferred_element_type=F32) * (dh ** -0.5)
            ex = jnp.exp(s - jnp.max(s, axis=-1, keepdims=True))
            p = ex / jnp.sum(ex, axis=-1, keepdims=True)
            heads.append(_dot(p.astype(BF16), mvb_ref[n, :, hs]).astype(BF16))
        seqs.append(jnp.concatenate(heads, axis=1))
    xo = _dot(jnp.concatenate(seqs, axis=0), wxo_ref[...])
    xo_ref[...] = (x1 + _rms(xo, nxpost_ref[...])).reshape(nb, tl, x1.shape[-1])


def _post(x, a_out, b_out, pc, pool_prev, mk, mv, wts, past, layer):
    b, l, d = x.shape
    n_mem, dh = mk.shape[2], mk.shape[4]
    tl = 256 if l % 256 == 0 else CHUNK
    nb = max(1, min(b, 256 // tl))
    while b % nb:
        nb -= 1
    halo = POOL_STATE + 1
    pprev = jnp.concatenate([jnp.zeros((b, 1, C_WIDTH), F32), pool_prev], axis=1)
    tile = lambda w: pl.BlockSpec((nb, tl, w), lambda i, j: (i, j, 0))
    per_seq = lambda r, w: pl.BlockSpec((nb, r, w), lambda i, j: (i, 0, 0))
    full = lambda a: pl.BlockSpec(a.shape, lambda i, j: (0,) * a.ndim)
    hbm = pl.BlockSpec(memory_space=pl.ANY)
    return pl.pallas_call(
        functools.partial(_post_kernel, nb=nb, tl=tl, past=past, layer=layer),
        grid=(b // nb, l // tl),
        in_specs=[tile(d), tile(A_WIDTH), tile(B_WIDTH), tile(2 * C_WIDTH), per_seq(halo, C_WIDTH),
                  hbm, hbm] + [full(a) for a in wts],
        out_specs=[tile(d), per_seq(POOL_STATE, C_WIDTH)],
        out_shape=[jax.ShapeDtypeStruct((b, l, d), F32),
                   jax.ShapeDtypeStruct((b, POOL_STATE, C_WIDTH), F32)],
        scratch_shapes=[pltpu.VMEM((nb, halo + tl, C_WIDTH), F32),
                        pltpu.VMEM((2, 2, nb, X_HEADS, n_mem, dh), F32),
                        pltpu.VMEM((nb, n_mem, d), BF16),
                        pltpu.VMEM((nb, n_mem, d), BF16),
                        pltpu.SemaphoreType.DMA((2, 2))],
        compiler_params=_params(2),
        name="mix_out",
    )(x, a_out, b_out, pc, pprev, mk, mv, *wts)


def _layer(x, mk, mv, past_t, s0, pool_prev, lw, layer, depth, kv_t):
    b, l, d = x.shape
    n = b * l
    (pa, bq, bk, bv, bg, pc), kv_t = _proj(x.reshape(n, d), lw["n_pre"], lw["w_in"], l, layer, depth, kv_t)
    a_out, s_new = _hgrn(pa.reshape(b, l, -1), lw["lb"], lw["onorm_g"], s0)
    seq = lambda t: t.reshape(b, l, -1)
    b_out = _sb(seq(bq), seq(bk), seq(bv), seq(bg), past_t, layer)
    past = 0 if past_t is None else past_t[0].shape[3]
    x_new, pool_new = _post(x, a_out, b_out, seq(pc), pool_prev, mk, mv, lw["post"], past, layer)
    heads = lambda t: t.reshape(b, l, B_HEADS, B_HEAD_DIM)
    return x_new, heads(bk), heads(bv), kv_t, s_new, pool_new


def kernel(x_prompt, x_sample, mem_prompt, cache_sb_k, cache_sb_v, state_hgrn, state_pool, cache_mem_k, cache_mem_v, norm_mix_pre, norm_mix_post, w_in, hgrn_lb_logits, hgrn_onorm_g, w_pool, pool_scale, w_out, norm_x_pre, norm_x_post, norm_mem, w_xq, w_xk, w_xv, w_xo):
    depth, d = norm_mix_pre.shape
    bp = x_prompt.shape[0]
    bs, past = cache_sb_k.shape[1], cache_sb_k.shape[2]
    n_mem = mem_prompt.shape[1]

    lbs = jax.nn.softmax(hgrn_lb_logits.astype(F32), axis=0)
    lower_bounds = jnp.maximum(jnp.cumsum(lbs, axis=0) - lbs[0], 0.0)

    mk_all, mv_all = _memkv(mem_prompt, norm_mem, w_xk.astype(BF16), w_xv.astype(BF16))

    def seq_minor(c):
        return jnp.transpose(c, (0, 1, 3, 4, 2)).reshape(depth, bs, B_WIDTH, past)

    def seq_major(t):
        dp, b, _, l = t.shape
        return jnp.transpose(t.reshape(dp, b, B_HEADS, B_HEAD_DIM, l), (0, 1, 4, 2, 3))

    past_t = (seq_minor(cache_sb_k), seq_minor(cache_sb_v))

    row = lambda v: v.reshape(1, -1)
    s_zero = jnp.zeros((bp, A_HEADS, A_HEAD_DIM, A_HEAD_DIM), F32)
    pool_zero = jnp.zeros((bp, POOL_STATE, C_WIDTH), F32)

    xp, xs = x_prompt, x_sample
    outs = [[] for _ in range(10)]
    kvt_p = kvt_s = None
    for l in range(depth):
        wp = jnp.zeros((C_WIDTH, C_WIDTH), F32)
        for gi in range(C_GROUPS):
            gs = slice(gi * C_GROUP_DIM, (gi + 1) * C_GROUP_DIM)
            wp = wp.at[gs, gs].set(w_pool[l, gi])
        wo = w_out[l].astype(BF16)
        lw = {
            "n_pre": norm_mix_pre[l], "w_in": w_in[l].astype(BF16), "lb": lower_bounds[l],
            "onorm_g": hgrn_onorm_g[l],
            "post": (wp.astype(BF16), row(pool_scale[l]), wo[0:A_WIDTH], wo[A_WIDTH:A_WIDTH + B_WIDTH],
                     wo[A_WIDTH + B_WIDTH:], row(norm_mix_post[l]), row(norm_x_pre[l]),
                     w_xq[l].astype(BF16), w_xo[l].astype(BF16), row(norm_x_post[l])),
        }
        xp, kb, vb, kvt_p, sn, pn = _layer(xp, mk_all, mv_all, None, s_zero, pool_zero, lw, l, depth, kvt_p)
        for o, t in zip(outs[:4], (kb, vb, sn, pn)):
            o.append(t)
        xs, kb, vb, kvt_s, sn, pn = _layer(xs, cache_mem_k, cache_mem_v, past_t, state_hgrn[l], state_pool[l],
                                           lw, l, depth, kvt_s)
        for o, t in zip(outs[6:], (kb, vb, sn, pn)):
            o.append(t)

    def new_kv(kvt, nat_k, nat_v):
        if kvt is not None:
            return seq_major(kvt[0]), seq_major(kvt[1])
        return jnp.stack(nat_k), jnp.stack(nat_v)

    p_k, p_v = new_kv(kvt_p, outs[0], outs[1])
    s_k, s_v = new_kv(kvt_s, outs[6], outs[7])
    p_s, p_pool = jnp.stack(outs[2]), jnp.stack(outs[3])
    s_s, s_pool = jnp.stack(outs[8]), jnp.stack(outs[9])
    return (xp, xs, p_k, p_v, p_s, p_pool, mk_all, mv_all, s_k, s_v, s_s, s_pool)
```

```python
import functools

import jax
import jax.numpy as jnp
from jax import lax
from jax.experimental import pallas as pl
from jax.experimental.pallas import tpu as pltpu

F32 = jnp.float32
BF16 = jnp.bfloat16
EPS = 1e-6

A_HEADS, A_HEAD_DIM = 4, 128
A_WIDTH = A_HEADS * A_HEAD_DIM
B_HEADS, B_HEAD_DIM = 4, 64
B_WIDTH = B_HEADS * B_HEAD_DIM
C_GROUPS, C_GROUP_DIM = 4, 64
C_WIDTH = C_GROUPS * C_GROUP_DIM
POOL_WINDOWS = (2, 4, 8, 16)
POOL_STATE = 15
X_HEADS = 4
CHUNK = 64
SUB = 16
SAFE_SPAN = 40.0
KEY_BLOCK = 128
HGRN_ROWS = 512
POST_ROWS = 512
POST_SEQS = 4
SUBLANES = 8
LANES = 128
VMEM_LIMIT = 56 * 1024 * 1024

NT_DIMS = (((1,), (1,)), ((), ()))
TN_DIMS = (((0,), (0,)), ((), ()))


def _rms(x, g):
    ms = jnp.mean(x * x, axis=-1, keepdims=True)
    return x * lax.rsqrt(ms + EPS) * g


def _silu(x):
    return x * (1.0 / (1.0 + jnp.exp(-x)))


def _dot(a, b):
    return jnp.dot(a, b, preferred_element_type=F32)


def _params(n_grid, flags=None):
    return pltpu.CompilerParams(dimension_semantics=("arbitrary",) * n_grid,
                                vmem_limit_bytes=VMEM_LIMIT, flags=flags)


def _memkv_kernel(mem_ref, g_ref, wk_ref, wv_ref, k_ref, v_ref):
    nb, n_mem, d = mem_ref.shape
    dh = d // X_HEADS
    for b in range(nb):
        m = _rms(mem_ref[b], g_ref[0]).astype(BF16)
        k = _dot(m, wk_ref[0])
        v = _dot(m, wv_ref[0])
        for h in range(X_HEADS):
            k_ref[0, b, :, h, :] = k[:, h * dh:(h + 1) * dh]
            v_ref[0, b, :, h, :] = v[:, h * dh:(h + 1) * dh]


def _memkv(mem, norm_mem, wk, wv):
    depth, d = norm_mem.shape
    nb, n_mem, _ = mem.shape
    dh = d // X_HEADS
    out = jax.ShapeDtypeStruct((depth, nb, n_mem, X_HEADS, dh), F32)
    return pl.pallas_call(
        _memkv_kernel,
        grid=(depth,),
        in_specs=[pl.BlockSpec((nb, n_mem, d), lambda l: (0, 0, 0)),
                  pl.BlockSpec((1, 1, d), lambda l: (l, 0, 0)),
                  pl.BlockSpec((1, d, d), lambda l: (l, 0, 0)),
                  pl.BlockSpec((1, d, d), lambda l: (l, 0, 0))],
        out_specs=[pl.BlockSpec((1, nb, n_mem, X_HEADS, dh), lambda l: (l, 0, 0, 0, 0))] * 2,
        out_shape=[out, out],
        compiler_params=_params(1),
        name="mem_kv",
    )(mem, norm_mem.reshape(depth, 1, d), wk, wv)


def _proj_kernel(*refs, transposed, aliased):
    x_ref, g_ref, w_ref = refs[:3]
    outs = refs[3 + 2 * aliased:]
    pa_ref, q_ref, k_ref, v_ref, bg_ref, pc_ref = outs[:6]
    h = _rms(x_ref[...], g_ref[...]).astype(BF16)
    a_end = 4 * A_WIDTH
    b_end = a_end + 4 * B_WIDTH
    pa_ref[...] = _dot(h, w_ref[:, 0:a_end])
    pb = _dot(h, w_ref[:, a_end:b_end])
    q_ref[...] = pb[:, 0:B_WIDTH]
    k_ref[...] = pb[:, B_WIDTH:2 * B_WIDTH]
    v_ref[...] = pb[:, 2 * B_WIDTH:3 * B_WIDTH]
    bg_ref[...] = pb[:, 3 * B_WIDTH:4 * B_WIDTH]
    pc_ref[...] = _dot(h, w_ref[:, b_end:])
    if transposed:
        kt_ref, vt_ref = outs[6:]
        kt_ref[0, 0] = pb[:, B_WIDTH:2 * B_WIDTH].T
        vt_ref[0, 0] = pb[:, 2 * B_WIDTH:3 * B_WIDTH].T


def _proj(x2d, g, w_in, seq_len, layer, depth, kv_t):
    n, d = x2d.shape
    d_in = w_in.shape[1]
    tm = 512 if n % 512 == 0 else CHUNK
    transposed = seq_len % tm == 0 and tm % LANES == 0
    aliased = transposed and kv_t is not None
    row = lambda width: pl.BlockSpec((tm, width), lambda i: (i, 0))
    shapes = [4 * A_WIDTH, B_WIDTH, B_WIDTH, B_WIDTH, B_WIDTH, 2 * C_WIDTH]
    in_specs = [row(d), pl.BlockSpec((1, d), lambda i: (0, 0)), pl.BlockSpec((d, d_in), lambda i: (0, 0))]
    args = [x2d, g.reshape(1, d), w_in]
    out_specs = [row(w) for w in shapes]
    out_shape = [jax.ShapeDtypeStruct((n, w), F32) for w in shapes]
    aliases = {}
    if transposed:
        tiles = seq_len // tm
        t_spec = pl.BlockSpec((1, 1, B_WIDTH, tm), lambda i: (layer, i // tiles, 0, i % tiles))
        out_specs += [t_spec, t_spec]
        out_shape += [jax.ShapeDtypeStruct((depth, n // seq_len, B_WIDTH, seq_len), F32)] * 2
        if aliased:
            in_specs += [pl.BlockSpec(memory_space=pl.ANY)] * 2
            args += list(kv_t)
            aliases = {3: 6, 4: 7}
    res = pl.pallas_call(
        functools.partial(_proj_kernel, transposed=transposed, aliased=aliased),
        grid=(n // tm,),
        in_specs=in_specs,
        out_specs=out_specs,
        out_shape=out_shape,
        input_output_aliases=aliases,
        compiler_params=_params(1),
        name="in_proj",
    )(*args)
    return res[:6], (tuple(res[6:]) if transposed else None)


def _hgrn_kernel(pa_ref, loglb_ref, log1mlb_ref, omlb_ref, og_ref, s0_ref, a_ref, sout_ref,
                 st_ref, g_ref, q_ref, k_ref, *, nb, nch):
    j = pl.program_id(1)

    @pl.when(j == 0)
    def _():
        for n in range(nb):
            for h in range(A_HEADS):
                st_ref[n, h] = s0_ref[n, h].T

    rowc = lax.broadcasted_iota(jnp.int32, (CHUNK, CHUNK), 0)
    colc = lax.broadcasted_iota(jnp.int32, (CHUNK, CHUNK), 1)
    tri = (rowc >= colc).astype(F32)
    col_sub = colc // SUB
    row_k = lax.broadcasted_iota(jnp.int32, (CHUNK, A_HEAD_DIM), 0)
    lane_d = lax.broadcasted_iota(jnp.int32, (SUB, CHUNK), 1)
    n_sub = CHUNK // SUB
    head = [slice(h * A_HEAD_DIM, (h + 1) * A_HEAD_DIM) for h in range(A_HEADS)]

    def sub_starts(g_all):
        return [jnp.zeros((1, A_WIDTH), F32)] + [g_all[SUB * s - 1:SUB * s, :] for s in range(1, n_sub)]

    span = jnp.zeros((1, A_WIDTH), F32)
    for n in range(nb):
        for c in range(nch):
            rs = slice(c * CHUNK, (c + 1) * CHUNK)
            aq = pa_ref[n, rs, 0:A_WIDTH]
            af = pa_ref[n, rs, A_WIDTH:2 * A_WIDTH]
            e = jnp.exp(-jnp.abs(af))
            ope = 1.0 + e
            logsig = jnp.minimum(af, 0.0) - jnp.log(ope)
            r = 1.0 / ope
            sig_neg = jnp.where(af >= 0, e * r, r)
            la = loglb_ref[...]
            lb_ = log1mlb_ref[...] + logsig
            log_f = jnp.maximum(la, lb_) + jnp.log(1.0 + jnp.exp(-jnp.abs(la - lb_)))
            g_all = jnp.dot(tri, log_f, precision=lax.Precision.HIGHEST, preferred_element_type=F32)
            g_ref[n, rs, :] = g_all
            k_ref[n, rs, :] = omlb_ref[...] * sig_neg
            q_ref[n, rs, :] = _silu(aq) * (A_HEAD_DIM ** -0.5)
            starts = sub_starts(g_all)
            for s in range(n_sub):
                span = jnp.maximum(span, starts[s] - g_all[SUB * s + SUB - 1:SUB * s + SUB, :])
    small_decay = jnp.max(span) < SAFE_SPAN

    def att_factored(g_all, q_all, k_all, qg_all):
        starts = sub_starts(g_all)
        operands = []
        for sl in head:
            gh, qh, kh = g_all[:, sl], q_all[:, sl], k_all[:, sl]
            parts, krows = [qg_all[:, sl]], []
            for s in range(n_sub):
                rs = slice(SUB * s, SUB * (s + 1))
                st = starts[s][:, sl]
                krows.append(kh[rs] * jnp.exp(st - gh[rs]))
                if s:
                    parts.append(jnp.where(row_k >= SUB * s, qh * jnp.exp(jnp.minimum(gh - st, 0.0)), 0.0))
            operands.append((jnp.concatenate(parts, axis=0).astype(BF16),
                             jnp.concatenate(krows, axis=0).astype(BF16)))
        rrs = [lax.dot_general(qr, kr, NT_DIMS, preferred_element_type=F32) for qr, kr in operands]
        atts = []
        for rr in rrs:
            att = rr[0:CHUNK, :]
            for s in range(1, n_sub):
                att = jnp.where(col_sub == s, rr[s * CHUNK:(s + 1) * CHUNK, :], att)
            atts.append(jnp.where(rowc >= colc, att, 0.0))
        return atts

    def att_exact(g_all, q_all, k_all, qg_all):
        def one_head(sl):
            gh, qh, kh = g_all[:, sl], q_all[:, sl], k_all[:, sl]
            parts, refrows = [], []
            for jsub in range(n_sub):
                ref = gh[SUB * jsub + SUB - 1:SUB * jsub + SUB, :]
                refrows.append(jnp.broadcast_to(ref, (SUB, A_HEAD_DIM)))
                if jsub < n_sub - 1:
                    later = row_k >= SUB * (jsub + 1)
                    parts.append(jnp.where(later, qh * jnp.exp(jnp.minimum(gh - ref, 0.0)), 0.0))
            q_rel = jnp.concatenate(parts, axis=0).astype(BF16)
            k_rel = (kh * jnp.exp(jnp.minimum(jnp.concatenate(refrows, axis=0) - gh, 0.0))).astype(BF16)
            rr = lax.dot_general(q_rel, k_rel, NT_DIMS, preferred_element_type=F32)
            att = jnp.zeros((CHUNK, CHUNK), F32)
            for jsub in range(n_sub - 1):
                att = jnp.where(col_sub == jsub, rr[jsub * CHUNK:(jsub + 1) * CHUNK, :], att)

            dparts = []
            for csub in range(n_sub):
                rs = slice(SUB * csub, SUB * (csub + 1))
                gc, qc, kc = gh[rs], qh[rs], kh[rs]
                accd = jnp.zeros((SUB, CHUNK), F32)
                for s in range(SUB):
                    dec = jnp.exp(jnp.minimum(gc - gc[s:s + 1, :], 0.0))
                    col = jnp.sum(dec * qc * kc[s:s + 1, :], axis=-1, keepdims=True)
                    accd = jnp.where(lane_d == SUB * csub + s, col, accd)
                dparts.append(accd)
            return att + jnp.where(rowc >= colc, jnp.concatenate(dparts, axis=0), 0.0)

        return [one_head(sl) for sl in head]

    def chunk(n, r0, sts, att_fn):
        g_all = g_ref[n, pl.ds(r0, CHUNK), :]
        q_all = q_ref[n, pl.ds(r0, CHUNK), :]
        k_all = k_ref[n, pl.ds(r0, CHUNK), :]
        v_all = pa_ref[n, pl.ds(r0, CHUNK), 2 * A_WIDTH:3 * A_WIDTH].astype(BF16)
        ag = pa_ref[n, pl.ds(r0, CHUNK), 3 * A_WIDTH:4 * A_WIDTH]
        qg_all = q_all * jnp.exp(g_all)
        gl_all = g_all[CHUNK - 1:CHUNK, :]
        kd_all = (k_all * jnp.exp(gl_all - g_all)).astype(BF16)
        qg_bf = qg_all.astype(BF16)
        atts = att_fn(g_all, q_all, k_all, qg_all)

        o_state = [lax.dot_general(qg_bf[:, sl], st.astype(BF16), NT_DIMS, preferred_element_type=F32)
                   for sl, st in zip(head, sts)]
        o_pairs = [_dot(att.astype(BF16), v_all[:, sl]) for sl, att in zip(head, atts)]
        updates = [lax.dot_general(v_all[:, sl], kd_all[:, sl], TN_DIMS, preferred_element_type=F32)
                   for sl in head]
        new_sts = [st * jnp.exp(gl_all[:, sl]) + up for sl, st, up in zip(head, sts, updates)]
        o = jnp.concatenate([a + b for a, b in zip(o_state, o_pairs)], axis=1)
        sq = o * o
        inv = jnp.concatenate(
            [jnp.broadcast_to(lax.rsqrt(jnp.mean(sq[:, sl], axis=-1, keepdims=True) + EPS), (CHUNK, A_HEAD_DIM))
             for sl in head], axis=1)
        a_ref[n, pl.ds(r0, CHUNK), :] = (o * inv * og_ref[...] * _silu(ag)).astype(BF16)
        return new_sts

    @pl.when(small_decay)
    def _():
        for n in range(nb):
            sts = [st_ref[n, h] for h in range(A_HEADS)]
            for c in range(nch):
                sts = chunk(n, c * CHUNK, sts, att_factored)
            for h in range(A_HEADS):
                st_ref[n, h] = sts[h]

    @pl.when(jnp.logical_not(small_decay))
    def _():
        for n in range(nb):
            def body(c, carry):
                sts = chunk(n, pl.multiple_of(c * CHUNK, CHUNK), [st_ref[n, h] for h in range(A_HEADS)], att_exact)
                for h in range(A_HEADS):
                    st_ref[n, h] = sts[h]
                return carry
            lax.fori_loop(0, nch, body, 0)

    @pl.when(j == pl.num_programs(1) - 1)
    def _():
        for n in range(nb):
            for h in range(A_HEADS):
                sout_ref[n, h] = st_ref[n, h].T


def _hgrn(pa, lb, onorm_g, s0):
    b, l, wa = pa.shape
    tl = HGRN_ROWS if l % HGRN_ROWS == 0 else CHUNK
    nb = max(1, min(b, HGRN_ROWS // tl))
    while b % nb:
        nb -= 1
    vec = lambda: pl.BlockSpec((1, A_WIDTH), lambda i, j: (0, 0))
    st_spec = pl.BlockSpec((nb, A_HEADS, A_HEAD_DIM, A_HEAD_DIM), lambda i, j: (i, 0, 0, 0))
    lb = lb.reshape(1, A_WIDTH)
    return pl.pallas_call(
        functools.partial(_hgrn_kernel, nb=nb, nch=tl // CHUNK),
        grid=(b // nb, l // tl),
        in_specs=[pl.BlockSpec((nb, tl, wa), lambda i, j: (i, j, 0)), vec(), vec(), vec(), vec(), st_spec],
        out_specs=[pl.BlockSpec((nb, tl, A_WIDTH), lambda i, j: (i, j, 0)), st_spec],
        out_shape=[jax.ShapeDtypeStruct((b, l, A_WIDTH), BF16),
                   jax.ShapeDtypeStruct((b, A_HEADS, A_HEAD_DIM, A_HEAD_DIM), F32)],
        scratch_shapes=[pltpu.VMEM((nb, A_HEADS, A_HEAD_DIM, A_HEAD_DIM), F32)]
                       + [pltpu.VMEM((nb, tl, A_WIDTH), F32)] * 3,
        compiler_params=_params(2),
        name="hgrn2",
    )(pa, jnp.log(lb), jnp.log1p(-lb), 1.0 - lb, onorm_g.reshape(1, A_WIDTH), s0)


def _sb_kernel(*refs, lq, n_past):
    if n_past:
        q_ref, k_ref, v_ref, g_ref, pk_ref, pv_ref, o_ref = refs[:7]
        kc_ref, vtc_ref, stk_ref, stv_ref, qtm_ref, carry_ref, acc_ref = refs[7:]
    else:
        q_ref, k_ref, v_ref, g_ref, o_ref = refs[:5]
        pk_ref = pv_ref = None
        kc_ref, vtc_ref, stk_ref, stv_ref, qtm_ref, carry_ref, acc_ref = refs[5:]
    j = pl.program_id(1)
    kb_rows = KEY_BLOCK
    run_len = kb_rows // SUBLANES
    width = B_WIDTH

    n_halves = width // LANES

    def permuted(ref):
        return jnp.concatenate(
            [jnp.concatenate([ref[c, pl.ds(jj, SUBLANES, stride=run_len), :] for jj in range(run_len)], axis=0)
             for c in range(n_halves)], axis=1)

    def stage(dst_ref, rows):
        for c in range(n_halves):
            if lq < kb_rows:
                dst_ref[c, lq:kb_rows, :] = jnp.zeros((kb_rows - lq, LANES), F32)
            dst_ref[c, 0:lq, :] = rows[:, c * LANES:(c + 1) * LANES]

    def staged(ref):
        return jnp.concatenate([ref[c] for c in range(n_halves)], axis=1)

    stage(stk_ref, k_ref[0])
    stage(stv_ref, v_ref[0])
    kc_ref[j] = permuted(stk_ref).astype(BF16)
    vtc_ref[j] = permuted(stv_ref).T.astype(BF16)

    stage(stk_ref, q_ref[0] * (B_HEAD_DIM ** -0.5))
    qt = staged(stk_ref).T
    row_head = lax.broadcasted_iota(jnp.int32, (width, kb_rows), 0) // B_HEAD_DIM
    qtm_ref[...] = jnp.concatenate([jnp.where(row_head == h, qt, 0.0) for h in range(B_HEADS)],
                                   axis=1).astype(BF16)

    row = lax.broadcasted_iota(jnp.int32, (kb_rows, kb_rows), 0)
    lane = lax.broadcasted_iota(jnp.int32, (kb_rows, kb_rows), 1)
    causal = (row % SUBLANES) * run_len + row // SUBLANES < lane
    sub = lax.broadcasted_iota(jnp.int32, (SUBLANES, kb_rows), 0)

    carry_ref[...] = jnp.ones(carry_ref.shape, F32)
    acc_ref[...] = jnp.zeros(acc_ref.shape, F32)

    def process(kb, vtb, masked):
        zt = _dot(kb, qtm_ref[...])
        for h in range(B_HEADS):
            ls = slice(h * kb_rows, (h + 1) * kb_rows)
            z = zt[:, ls]
            e = jnp.exp(-jnp.abs(z))
            r = 1.0 / (1.0 + e)
            er = e * r
            pos = z >= 0
            beta = jnp.where(pos, r, er)
            omb = jnp.where(pos, er, r)
            if masked:
                beta = jnp.where(causal, beta, 0.0)
                omb = jnp.where(causal, omb, 1.0)
            run = jnp.ones((SUBLANES, kb_rows), F32)
            excl = [None] * run_len
            for jj in reversed(range(run_len)):
                excl[jj] = run
                run = run * omb[jj * SUBLANES:(jj + 1) * SUBLANES]
            inc = run
            for d in (1, 2, 4):
                inc = jnp.where(sub + d < SUBLANES, inc * pltpu.roll(inc, SUBLANES - d, 0), inc)
            carry = carry_ref[:, ls]
            off = jnp.where(sub < SUBLANES - 1, pltpu.roll(inc, SUBLANES - 1, 0), 1.0) * carry
            w = jnp.concatenate([beta[jj * SUBLANES:(jj + 1) * SUBLANES] * (excl[jj] * off)
                                 for jj in range(run_len)], axis=0).astype(BF16)
            carry_ref[:, ls] = carry * jnp.broadcast_to(inc[0:1, :], (SUBLANES, kb_rows))
            hs = slice(h * B_HEAD_DIM, (h + 1) * B_HEAD_DIM)
            acc_ref[hs, :] += _dot(vtb[hs, :], w)

    def alive():
        return (jnp.max(carry_ref[...]) > 0.0).astype(jnp.int32)

    def loop(first, fetch):
        def cond(state):
            kb, live = state
            return jnp.logical_and(kb >= 0, live > 0)

        def body(state):
            kb, _ = state
            kblk, vtblk = fetch(kb)
            process(kblk, vtblk, False)
            return kb - 1, alive()

        lax.while_loop(cond, body, (first, alive()))

    process(kc_ref[j], vtc_ref[j], True)
    loop(j - 1, lambda kb: (kc_ref[kb], vtc_ref[kb]))

    for kb in reversed(range(n_past)):
        @pl.when(alive() > 0)
        def _():
            cols = slice(kb * kb_rows, (kb + 1) * kb_rows)
            kn = pk_ref[0, 0, :, cols].T
            vn = pv_ref[0, 0, :, cols].T
            for c in range(n_halves):
                stk_ref[c] = kn[:, c * LANES:(c + 1) * LANES]
                stv_ref[c] = vn[:, c * LANES:(c + 1) * LANES]
            process(permuted(stk_ref).astype(BF16), permuted(stv_ref).T.astype(BF16), False)

    o = acc_ref[...].T[0:lq, :]
    o_ref[0] = (o * _silu(g_ref[0])).astype(BF16)


def _sb(q, k, v, g, past_t, layer):
    b, l, w = q.shape
    lq = min(KEY_BLOCK, l)
    nblk = l // lq
    p = 0 if past_t is None else past_t[0].shape[3]
    blk = pl.BlockSpec((1, lq, w), lambda i, j: (i, j, 0))
    in_specs = [blk] * 4
    args = [q, k, v, g]
    if p:
        in_specs += [pl.BlockSpec((1, 1, w, p), lambda i, j: (layer, i, 0, 0))] * 2
        args += list(past_t)
    return pl.pallas_call(
        functools.partial(_sb_kernel, lq=lq, n_past=p // KEY_BLOCK),
        grid=(b, nblk),
        in_specs=in_specs,
        out_specs=blk,
        out_shape=jax.ShapeDtypeStruct((b, l, w), BF16),
        scratch_shapes=[pltpu.VMEM((nblk, KEY_BLOCK, w), BF16),
                        pltpu.VMEM((nblk, w, KEY_BLOCK), BF16),
                        pltpu.VMEM((w // LANES, KEY_BLOCK, LANES), F32),
                        pltpu.VMEM((w // LANES, KEY_BLOCK, LANES), F32),
                        pltpu.VMEM((w, B_HEADS * KEY_BLOCK), BF16),
                        pltpu.VMEM((SUBLANES, B_HEADS * KEY_BLOCK), F32),
                        pltpu.VMEM((w, KEY_BLOCK), F32)],
        compiler_params=_params(2),
        name="sb_attn",
    )(*args)


def _post_kernel(x_ref, a_ref, b_ref, pc_ref, pprev_ref, mk_hbm, mv_hbm, wpool_ref, pscale_ref,
                 woa_ref, wob_ref, woc_ref, npost_ref, nxpre_ref, wxq_ref, wxo_ref, nxpost_ref,
                 xo_ref, pnew_ref, ext_ref, kv_ref, mkb_ref, mvb_ref, sem_ref, *, nb, tl, n_split, past, layer):
    i = pl.program_id(0)
    j = pl.program_id(1)
    n_groups = pl.num_programs(0)
    halo = POOL_STATE + 1
    dh = kv_ref.shape[-1]

    def kv_copies(group, slot):
        return [pltpu.make_async_copy(src.at[layer, group * nb + n, :, h, :], kv_ref.at[slot, t, n, h],
                                      sem_ref.at[slot, t])
                for t, src in enumerate((mk_hbm, mv_hbm)) for n in range(nb) for h in range(X_HEADS)]

    @pl.when(j == 0)
    def _():
        slot = i % 2

        @pl.when(i == 0)
        def _():
            for cp in kv_copies(0, 0):
                cp.start()

        @pl.when(i + 1 < n_groups)
        def _():
            for cp in kv_copies(i + 1, 1 - slot):
                cp.start()

        for cp in kv_copies(i, slot):
            cp.wait()
        for n in range(nb):
            ext_ref[n, 0:halo, :] = pprev_ref[n]
            for h in range(X_HEADS):
                mkb_ref[n, :, h * dh:(h + 1) * dh] = kv_ref[slot, 0, n, h].astype(BF16)
                mvb_ref[n, :, h * dh:(h + 1) * dh] = kv_ref[slot, 1, n, h].astype(BF16)

    grp = lax.broadcasted_iota(jnp.int32, (tl, C_WIDTH), 1) // C_GROUP_DIM
    pos = past + j * tl + lax.broadcasted_iota(jnp.int32, (tl, C_WIDTH), 0)
    dds = []
    for n in range(nb):
        cu = pc_ref[n, :, 0:C_WIDTH]
        ext_ref[n, halo:halo + tl, :] = cu
        acc = cu
        win = jnp.zeros((tl, C_WIDTH), F32)
        wlen = jnp.zeros((tl, C_WIDTH), jnp.int32)
        for d in range(1, POOL_WINDOWS[-1]):
            acc = acc + ext_ref[n, pl.ds(halo - d, tl), :]
            if d + 1 in POOL_WINDOWS:
                gi = POOL_WINDOWS.index(d + 1)
                win = jnp.where(grp == gi, acc, win)
                wlen = jnp.where(grp == gi, d + 1, wlen)
        cnt = jnp.minimum(pos + 1, wlen).astype(F32)
        dds.append((win / cnt - cu).astype(BF16))
        pnew_ref[n] = ext_ref[n, tl + 1:tl + halo, :]
        ext_ref[n, 0:halo, :] = ext_ref[n, tl:tl + halo, :]

    seg = tl // n_split if nb == 1 else tl
    groups = [[(0, g * seg)] for g in range(n_split)] if nb == 1 else \
             [[(n, 0) for n in range(g * nb // n_split, (g + 1) * nb // n_split)] for g in range(n_split)]

    def rows(members, ref, w0=0, w1=None):
        return jnp.concatenate([ref[n, r0:r0 + seg, w0:w1] for n, r0 in members], axis=0)

    mixes = []
    for members in groups:
        dd = jnp.concatenate([dds[n][r0:r0 + seg] for n, r0 in members], axis=0)
        y = _dot(dd, wpool_ref[...]) * pscale_ref[...]
        c_out = (y * _silu(rows(members, pc_ref, C_WIDTH, 2 * C_WIDTH))).astype(BF16)
        mixes.append(_dot(rows(members, a_ref), woa_ref[...]) + _dot(rows(members, b_ref), wob_ref[...])
                     + _dot(c_out, woc_ref[...]))
    x1s = [rows(members, x_ref) + _rms(mix, npost_ref[...]) for members, mix in zip(groups, mixes)]

    qs = [_dot(_rms(x1, nxpre_ref[...]).astype(BF16), wxq_ref[...]).astype(BF16) for x1 in x1s]
    pairs = [(g, m, n, h) for g, members in enumerate(groups) for m, (n, _) in enumerate(members)
             for h in range(X_HEADS)]
    scores = [lax.dot_general(qs[g][m * seg:(m + 1) * seg, h * dh:(h + 1) * dh],
                              mkb_ref[n, :, h * dh:(h + 1) * dh], NT_DIMS,
                              preferred_element_type=F32) * (dh ** -0.5) for g, m, n, h in pairs]
    probs = []
    for s in scores:
        ex = jnp.exp(s - jnp.max(s, axis=-1, keepdims=True))
        probs.append((ex / jnp.sum(ex, axis=-1, keepdims=True)).astype(BF16))
    ctx = [_dot(p, mvb_ref[n, :, h * dh:(h + 1) * dh]).astype(BF16) for p, (g, m, n, h) in zip(probs, pairs)]
    xos = []
    for g, members in enumerate(groups):
        seqs = [jnp.concatenate([c for c, (g2, m2, _, _) in zip(ctx, pairs) if g2 == g and m2 == m], axis=1)
                for m in range(len(members))]
        xos.append(_dot(jnp.concatenate(seqs, axis=0), wxo_ref[...]))
    for members, x1, xo in zip(groups, x1s, xos):
        res = x1 + _rms(xo, nxpost_ref[...])
        for m, (n, r0) in enumerate(members):
            xo_ref[n, r0:r0 + seg, :] = res[m * seg:(m + 1) * seg]


def _post(x, a_out, b_out, pc, pool_prev, mk, mv, wts, past, layer):
    b, l, d = x.shape
    n_mem, dh = mk.shape[2], mk.shape[4]
    tl = POST_ROWS if l % POST_ROWS == 0 else CHUNK
    nb = max(1, min(b, POST_ROWS // tl, POST_SEQS))
    while b % nb:
        nb -= 1
    n_split = 2 if (nb * tl) % (2 * LANES) == 0 and (nb == 1 or nb % 2 == 0) else 1
    halo = POOL_STATE + 1
    pprev = jnp.concatenate([jnp.zeros((b, 1, C_WIDTH), F32), pool_prev], axis=1)
    tile = lambda w: pl.BlockSpec((nb, tl, w), lambda i, j: (i, j, 0))
    per_seq = lambda r, w: pl.BlockSpec((nb, r, w), lambda i, j: (i, 0, 0))
    full = lambda a: pl.BlockSpec(a.shape, lambda i, j: (0,) * a.ndim)
    hbm = pl.BlockSpec(memory_space=pl.ANY)
    return pl.pallas_call(
        functools.partial(_post_kernel, nb=nb, tl=tl, n_split=n_split, past=past, layer=layer),
        grid=(b // nb, l // tl),
        in_specs=[tile(d), tile(A_WIDTH), tile(B_WIDTH), tile(2 * C_WIDTH), per_seq(halo, C_WIDTH),
                  hbm, hbm] + [full(a) for a in wts],
        out_specs=[tile(d), per_seq(POOL_STATE, C_WIDTH)],
        out_shape=[jax.ShapeDtypeStruct((b, l, d), F32),
                   jax.ShapeDtypeStruct((b, POOL_STATE, C_WIDTH), F32)],
        scratch_shapes=[pltpu.VMEM((nb, halo + tl, C_WIDTH), F32),
                        pltpu.VMEM((2, 2, nb, X_HEADS, n_mem, dh), F32),
                        pltpu.VMEM((nb, n_mem, d), BF16),
                        pltpu.VMEM((nb, n_mem, d), BF16),
                        pltpu.SemaphoreType.DMA((2, 2))],
        compiler_params=_params(2),
        name="mix_out",
    )(x, a_out, b_out, pc, pprev, mk, mv, *wts)


def _layer(x, mk, mv, past_t, s0, pool_prev, lw, layer, depth, kv_t):
    b, l, d = x.shape
    n = b * l
    (pa, bq, bk, bv, bg, pc), kv_t = _proj(x.reshape(n, d), lw["n_pre"], lw["w_in"], l, layer, depth, kv_t)
    a_out, s_new = _hgrn(pa.reshape(b, l, -1), lw["lb"], lw["onorm_g"], s0)
    seq = lambda t: t.reshape(b, l, -1)
    b_out = _sb(seq(bq), seq(bk), seq(bv), seq(bg), past_t, layer)
    past = 0 if past_t is None else past_t[0].shape[3]
    x_new, pool_new = _post(x, a_out, b_out, seq(pc), pool_prev, mk, mv, lw["post"], past, layer)
    heads = lambda t: t.reshape(b, l, B_HEADS, B_HEAD_DIM)
    return x_new, heads(bk), heads(bv), kv_t, s_new, pool_new


def kernel(x_prompt, x_sample, mem_prompt, cache_sb_k, cache_sb_v, state_hgrn, state_pool, cache_mem_k, cache_mem_v, norm_mix_pre, norm_mix_post, w_in, hgrn_lb_logits, hgrn_onorm_g, w_pool, pool_scale, w_out, norm_x_pre, norm_x_post, norm_mem, w_xq, w_xk, w_xv, w_xo):
    depth, d = norm_mix_pre.shape
    bp = x_prompt.shape[0]
    bs, past = cache_sb_k.shape[1], cache_sb_k.shape[2]
    n_mem = mem_prompt.shape[1]

    lbs = jax.nn.softmax(hgrn_lb_logits.astype(F32), axis=0)
    lower_bounds = jnp.maximum(jnp.cumsum(lbs, axis=0) - lbs[0], 0.0)

    mk_all, mv_all = _memkv(mem_prompt, norm_mem, w_xk.astype(BF16), w_xv.astype(BF16))

    def seq_minor(c):
        return jnp.transpose(c, (0, 1, 3, 4, 2)).reshape(depth, bs, B_WIDTH, past)

    def seq_major(t):
        dp, b, _, l = t.shape
        return jnp.transpose(t.reshape(dp, b, B_HEADS, B_HEAD_DIM, l), (0, 1, 4, 2, 3))

    past_t = (seq_minor(cache_sb_k), seq_minor(cache_sb_v))

    row = lambda v: v.reshape(1, -1)
    s_zero = jnp.zeros((bp, A_HEADS, A_HEAD_DIM, A_HEAD_DIM), F32)
    pool_zero = jnp.zeros((bp, POOL_STATE, C_WIDTH), F32)

    xp, xs = x_prompt, x_sample
    outs = [[] for _ in range(10)]
    kvt_p = kvt_s = None
    for l in range(depth):
        wp = jnp.zeros((C_WIDTH, C_WIDTH), F32)
        for gi in range(C_GROUPS):
            gs = slice(gi * C_GROUP_DIM, (gi + 1) * C_GROUP_DIM)
            wp = wp.at[gs, gs].set(w_pool[l, gi])
        wo = w_out[l].astype(BF16)
        lw = {
            "n_pre": norm_mix_pre[l], "w_in": w_in[l].astype(BF16), "lb": lower_bounds[l],
            "onorm_g": hgrn_onorm_g[l],
            "post": (wp.astype(BF16), row(pool_scale[l]), wo[0:A_WIDTH], wo[A_WIDTH:A_WIDTH + B_WIDTH],
                     wo[A_WIDTH + B_WIDTH:], row(norm_mix_post[l]), row(norm_x_pre[l]),
                     w_xq[l].astype(BF16), w_xo[l].astype(BF16), row(norm_x_post[l])),
        }
        xp, kb, vb, kvt_p, sn, pn = _layer(xp, mk_all, mv_all, None, s_zero, pool_zero, lw, l, depth, kvt_p)
        for o, t in zip(outs[:4], (kb, vb, sn, pn)):
            o.append(t)
        xs, kb, vb, kvt_s, sn, pn = _layer(xs, cache_mem_k, cache_mem_v, past_t, state_hgrn[l], state_pool[l],
                                           lw, l, depth, kvt_s)
        for o, t in zip(outs[6:], (kb, vb, sn, pn)):
            o.append(t)

    def new_kv(kvt, nat_k, nat_v):
        if kvt is not None:
            return seq_major(kvt[0]), seq_major(kvt[1])
        return jnp.stack(nat_k), jnp.stack(nat_v)

    p_k, p_v = new_kv(kvt_p, outs[0], outs[1])
    s_k, s_v = new_kv(kvt_s, outs[6], outs[7])
    p_s, p_pool = jnp.stack(outs[2]), jnp.stack(outs[3])
    s_s, s_pool = jnp.stack(outs[8]), jnp.stack(outs[9])
    return (xp, xs, p_k, p_v, p_s, p_pool, mk_all, mv_all, s_k, s_v, s_s, s_pool)
```

```python
import functools

import jax
import jax.numpy as jnp
from jax import lax
from jax.experimental import pallas as pl
from jax.experimental.pallas import tpu as pltpu

F32 = jnp.float32
BF16 = jnp.bfloat16
EPS = 1e-6

A_HEADS, A_HEAD_DIM = 4, 128
A_WIDTH = A_HEADS * A_HEAD_DIM
B_HEADS, B_HEAD_DIM = 4, 64
B_WIDTH = B_HEADS * B_HEAD_DIM
C_GROUPS, C_GROUP_DIM = 4, 64
C_WIDTH = C_GROUPS * C_GROUP_DIM
POOL_WINDOWS = (2, 4, 8, 16)
POOL_STATE = 15
X_HEADS = 4
CHUNK = 64
SUB = 16
SAFE_SPAN = 40.0
KEY_BLOCK = 128
SB_SEQS = 4
HGRN_ROWS = 512
POST_ROWS = 512
POST_SEQS = 4
SUBLANES = 8
LANES = 128
VMEM_LIMIT = 56 * 1024 * 1024

NT_DIMS = (((1,), (1,)), ((), ()))
TN_DIMS = (((0,), (0,)), ((), ()))


def _rms(x, g):
    ms = jnp.mean(x * x, axis=-1, keepdims=True)
    return x * lax.rsqrt(ms + EPS) * g


def _silu(x):
    return x * (1.0 / (1.0 + jnp.exp(-x)))


def _dot(a, b):
    return jnp.dot(a, b, preferred_element_type=F32)


def _params(n_grid, flags=None):
    return pltpu.CompilerParams(dimension_semantics=("arbitrary",) * n_grid,
                                vmem_limit_bytes=VMEM_LIMIT, flags=flags)


def _memkv_kernel(mem_ref, g_ref, wk_ref, wv_ref, k_ref, v_ref):
    nb, n_mem, d = mem_ref.shape
    dh = d // X_HEADS
    for b in range(nb):
        m = _rms(mem_ref[b], g_ref[0]).astype(BF16)
        k = _dot(m, wk_ref[0])
        v = _dot(m, wv_ref[0])
        for h in range(X_HEADS):
            k_ref[0, b, :, h, :] = k[:, h * dh:(h + 1) * dh]
            v_ref[0, b, :, h, :] = v[:, h * dh:(h + 1) * dh]


def _memkv(mem, norm_mem, wk, wv):
    depth, d = norm_mem.shape
    nb, n_mem, _ = mem.shape
    dh = d // X_HEADS
    out = jax.ShapeDtypeStruct((depth, nb, n_mem, X_HEADS, dh), F32)
    return pl.pallas_call(
        _memkv_kernel,
        grid=(depth,),
        in_specs=[pl.BlockSpec((nb, n_mem, d), lambda l: (0, 0, 0)),
                  pl.BlockSpec((1, 1, d), lambda l: (l, 0, 0)),
                  pl.BlockSpec((1, d, d), lambda l: (l, 0, 0)),
                  pl.BlockSpec((1, d, d), lambda l: (l, 0, 0))],
        out_specs=[pl.BlockSpec((1, nb, n_mem, X_HEADS, dh), lambda l: (l, 0, 0, 0, 0))] * 2,
        out_shape=[out, out],
        compiler_params=_params(1),
        name="mem_kv",
    )(mem, norm_mem.reshape(depth, 1, d), wk, wv)


def _proj_kernel(*refs, transposed, aliased):
    x_ref, g_ref, w_ref = refs[:3]
    outs = refs[3 + 2 * aliased:]
    pa_ref, q_ref, k_ref, v_ref, bg_ref, pc_ref = outs[:6]
    h = _rms(x_ref[...], g_ref[...]).astype(BF16)
    a_end = 4 * A_WIDTH
    b_end = a_end + 4 * B_WIDTH
    pa_ref[...] = _dot(h, w_ref[:, 0:a_end])
    pb = _dot(h, w_ref[:, a_end:b_end])
    q_ref[...] = pb[:, 0:B_WIDTH]
    k_ref[...] = pb[:, B_WIDTH:2 * B_WIDTH]
    v_ref[...] = pb[:, 2 * B_WIDTH:3 * B_WIDTH]
    bg_ref[...] = pb[:, 3 * B_WIDTH:4 * B_WIDTH]
    pc_ref[...] = _dot(h, w_ref[:, b_end:])
    if transposed:
        kt_ref, vt_ref = outs[6:]
        kt_ref[0, 0] = pb[:, B_WIDTH:2 * B_WIDTH].T
        vt_ref[0, 0] = pb[:, 2 * B_WIDTH:3 * B_WIDTH].T


def _proj(x2d, g, w_in, seq_len, layer, depth, kv_t):
    n, d = x2d.shape
    d_in = w_in.shape[1]
    tm = 512 if n % 512 == 0 else CHUNK
    transposed = seq_len % tm == 0 and tm % LANES == 0
    aliased = transposed and kv_t is not None
    row = lambda width: pl.BlockSpec((tm, width), lambda i: (i, 0))
    shapes = [4 * A_WIDTH, B_WIDTH, B_WIDTH, B_WIDTH, B_WIDTH, 2 * C_WIDTH]
    in_specs = [row(d), pl.BlockSpec((1, d), lambda i: (0, 0)), pl.BlockSpec((d, d_in), lambda i: (0, 0))]
    args = [x2d, g.reshape(1, d), w_in]
    out_specs = [row(w) for w in shapes]
    out_shape = [jax.ShapeDtypeStruct((n, w), F32) for w in shapes]
    aliases = {}
    if transposed:
        tiles = seq_len // tm
        t_spec = pl.BlockSpec((1, 1, B_WIDTH, tm), lambda i: (layer, i // tiles, 0, i % tiles))
        out_specs += [t_spec, t_spec]
        out_shape += [jax.ShapeDtypeStruct((depth, n // seq_len, B_WIDTH, seq_len), F32)] * 2
        if aliased:
            in_specs += [pl.BlockSpec(memory_space=pl.ANY)] * 2
            args += list(kv_t)
            aliases = {3: 6, 4: 7}
    res = pl.pallas_call(
        functools.partial(_proj_kernel, transposed=transposed, aliased=aliased),
        grid=(n // tm,),
        in_specs=in_specs,
        out_specs=out_specs,
        out_shape=out_shape,
        input_output_aliases=aliases,
        compiler_params=_params(1),
        name="in_proj",
    )(*args)
    return res[:6], (tuple(res[6:]) if transposed else None)


def _hgrn_kernel(pa_ref, loglb_ref, log1mlb_ref, omlb_ref, og_ref, s0_ref, a_ref, sout_ref,
                 st_ref, g_ref, q_ref, k_ref, *, nb, nch):
    j = pl.program_id(1)

    @pl.when(j == 0)
    def _():
        for n in range(nb):
            for h in range(A_HEADS):
                st_ref[n, h] = s0_ref[n, h].T

    rowc = lax.broadcasted_iota(jnp.int32, (CHUNK, CHUNK), 0)
    colc = lax.broadcasted_iota(jnp.int32, (CHUNK, CHUNK), 1)
    tri = (rowc >= colc).astype(F32)
    col_sub = colc // SUB
    row_k = lax.broadcasted_iota(jnp.int32, (CHUNK, A_HEAD_DIM), 0)
    lane_d = lax.broadcasted_iota(jnp.int32, (SUB, CHUNK), 1)
    n_sub = CHUNK // SUB
    head = [slice(h * A_HEAD_DIM, (h + 1) * A_HEAD_DIM) for h in range(A_HEADS)]

    def sub_starts(g_all):
        return [jnp.zeros((1, A_WIDTH), F32)] + [g_all[SUB * s - 1:SUB * s, :] for s in range(1, n_sub)]

    span = jnp.zeros((1, A_WIDTH), F32)
    for n in range(nb):
        for c in range(nch):
            rs = slice(c * CHUNK, (c + 1) * CHUNK)
            aq = pa_ref[n, rs, 0:A_WIDTH]
            af = pa_ref[n, rs, A_WIDTH:2 * A_WIDTH]
            e = jnp.exp(-jnp.abs(af))
            ope = 1.0 + e
            logsig = jnp.minimum(af, 0.0) - jnp.log(ope)
            r = 1.0 / ope
            sig_neg = jnp.where(af >= 0, e * r, r)
            la = loglb_ref[...]
            lb_ = log1mlb_ref[...] + logsig
            log_f = jnp.maximum(la, lb_) + jnp.log(1.0 + jnp.exp(-jnp.abs(la - lb_)))
            g_all = jnp.dot(tri, log_f, precision=lax.Precision.HIGHEST, preferred_element_type=F32)
            g_ref[n, rs, :] = g_all
            k_ref[n, rs, :] = omlb_ref[...] * sig_neg
            q_ref[n, rs, :] = _silu(aq) * (A_HEAD_DIM ** -0.5)
            starts = sub_starts(g_all)
            for s in range(n_sub):
                span = jnp.maximum(span, starts[s] - g_all[SUB * s + SUB - 1:SUB * s + SUB, :])
    small_decay = jnp.max(span) < SAFE_SPAN

    def att_factored(g_all, q_all, k_all, qg_all):
        starts = sub_starts(g_all)
        operands = []
        for sl in head:
            gh, qh, kh = g_all[:, sl], q_all[:, sl], k_all[:, sl]
            parts, krows = [qg_all[:, sl]], []
            for s in range(n_sub):
                rs = slice(SUB * s, SUB * (s + 1))
                st = starts[s][:, sl]
                krows.append(kh[rs] * jnp.exp(st - gh[rs]))
                if s:
                    parts.append(jnp.where(row_k >= SUB * s, qh * jnp.exp(jnp.minimum(gh - st, 0.0)), 0.0))
            operands.append((jnp.concatenate(parts, axis=0).astype(BF16),
                             jnp.concatenate(krows, axis=0).astype(BF16)))
        rrs = [lax.dot_general(qr, kr, NT_DIMS, preferred_element_type=F32) for qr, kr in operands]
        atts = []
        for rr in rrs:
            att = rr[0:CHUNK, :]
            for s in range(1, n_sub):
                att = jnp.where(col_sub == s, rr[s * CHUNK:(s + 1) * CHUNK, :], att)
            atts.append(jnp.where(rowc >= colc, att, 0.0))
        return atts

    def att_exact(g_all, q_all, k_all, qg_all):
        def one_head(sl):
            gh, qh, kh = g_all[:, sl], q_all[:, sl], k_all[:, sl]
            parts, refrows = [], []
            for jsub in range(n_sub):
                ref = gh[SUB * jsub + SUB - 1:SUB * jsub + SUB, :]
                refrows.append(jnp.broadcast_to(ref, (SUB, A_HEAD_DIM)))
                if jsub < n_sub - 1:
                    later = row_k >= SUB * (jsub + 1)
                    parts.append(jnp.where(later, qh * jnp.exp(jnp.minimum(gh - ref, 0.0)), 0.0))
            q_rel = jnp.concatenate(parts, axis=0).astype(BF16)
            k_rel = (kh * jnp.exp(jnp.minimum(jnp.concatenate(refrows, axis=0) - gh, 0.0))).astype(BF16)
            rr = lax.dot_general(q_rel, k_rel, NT_DIMS, preferred_element_type=F32)
            att = jnp.zeros((CHUNK, CHUNK), F32)
            for jsub in range(n_sub - 1):
                att = jnp.where(col_sub == jsub, rr[jsub * CHUNK:(jsub + 1) * CHUNK, :], att)

            dparts = []
            for csub in range(n_sub):
                rs = slice(SUB * csub, SUB * (csub + 1))
                gc, qc, kc = gh[rs], qh[rs], kh[rs]
                accd = jnp.zeros((SUB, CHUNK), F32)
                for s in range(SUB):
                    dec = jnp.exp(jnp.minimum(gc - gc[s:s + 1, :], 0.0))
                    col = jnp.sum(dec * qc * kc[s:s + 1, :], axis=-1, keepdims=True)
                    accd = jnp.where(lane_d == SUB * csub + s, col, accd)
                dparts.append(accd)
            return att + jnp.where(rowc >= colc, jnp.concatenate(dparts, axis=0), 0.0)

        return [one_head(sl) for sl in head]

    def chunk(n, r0, sts, att_fn):
        g_all = g_ref[n, pl.ds(r0, CHUNK), :]
        q_all = q_ref[n, pl.ds(r0, CHUNK), :]
        k_all = k_ref[n, pl.ds(r0, CHUNK), :]
        v_all = pa_ref[n, pl.ds(r0, CHUNK), 2 * A_WIDTH:3 * A_WIDTH].astype(BF16)
        ag = pa_ref[n, pl.ds(r0, CHUNK), 3 * A_WIDTH:4 * A_WIDTH]
        qg_all = q_all * jnp.exp(g_all)
        gl_all = g_all[CHUNK - 1:CHUNK, :]
        kd_all = (k_all * jnp.exp(gl_all - g_all)).astype(BF16)
        qg_bf = qg_all.astype(BF16)
        atts = att_fn(g_all, q_all, k_all, qg_all)

        o_state = [lax.dot_general(qg_bf[:, sl], st.astype(BF16), NT_DIMS, preferred_element_type=F32)
                   for sl, st in zip(head, sts)]
        o_pairs = [_dot(att.astype(BF16), v_all[:, sl]) for sl, att in zip(head, atts)]
        updates = [lax.dot_general(v_all[:, sl], kd_all[:, sl], TN_DIMS, preferred_element_type=F32)
                   for sl in head]
        new_sts = [st * jnp.exp(gl_all[:, sl]) + up for sl, st, up in zip(head, sts, updates)]
        o = jnp.concatenate([a + b for a, b in zip(o_state, o_pairs)], axis=1)
        sq = o * o
        inv = jnp.concatenate(
            [jnp.broadcast_to(lax.rsqrt(jnp.mean(sq[:, sl], axis=-1, keepdims=True) + EPS), (CHUNK, A_HEAD_DIM))
             for sl in head], axis=1)
        a_ref[n, pl.ds(r0, CHUNK), :] = (o * inv * og_ref[...] * _silu(ag)).astype(BF16)
        return new_sts

    @pl.when(small_decay)
    def _():
        for n in range(nb):
            sts = [st_ref[n, h] for h in range(A_HEADS)]
            for c in range(nch):
                sts = chunk(n, c * CHUNK, sts, att_factored)
            for h in range(A_HEADS):
                st_ref[n, h] = sts[h]

    @pl.when(jnp.logical_not(small_decay))
    def _():
        for n in range(nb):
            def body(c, carry):
                sts = chunk(n, pl.multiple_of(c * CHUNK, CHUNK), [st_ref[n, h] for h in range(A_HEADS)], att_exact)
                for h in range(A_HEADS):
                    st_ref[n, h] = sts[h]
                return carry
            lax.fori_loop(0, nch, body, 0)

    @pl.when(j == pl.num_programs(1) - 1)
    def _():
        for n in range(nb):
            for h in range(A_HEADS):
                sout_ref[n, h] = st_ref[n, h].T


def _hgrn(pa, lb, onorm_g, s0):
    b, l, wa = pa.shape
    tl = HGRN_ROWS if l % HGRN_ROWS == 0 else CHUNK
    nb = max(1, min(b, HGRN_ROWS // tl))
    while b % nb:
        nb -= 1
    vec = lambda: pl.BlockSpec((1, A_WIDTH), lambda i, j: (0, 0))
    st_spec = pl.BlockSpec((nb, A_HEADS, A_HEAD_DIM, A_HEAD_DIM), lambda i, j: (i, 0, 0, 0))
    lb = lb.reshape(1, A_WIDTH)
    return pl.pallas_call(
        functools.partial(_hgrn_kernel, nb=nb, nch=tl // CHUNK),
        grid=(b // nb, l // tl),
        in_specs=[pl.BlockSpec((nb, tl, wa), lambda i, j: (i, j, 0)), vec(), vec(), vec(), vec(), st_spec],
        out_specs=[pl.BlockSpec((nb, tl, A_WIDTH), lambda i, j: (i, j, 0)), st_spec],
        out_shape=[jax.ShapeDtypeStruct((b, l, A_WIDTH), BF16),
                   jax.ShapeDtypeStruct((b, A_HEADS, A_HEAD_DIM, A_HEAD_DIM), F32)],
        scratch_shapes=[pltpu.VMEM((nb, A_HEADS, A_HEAD_DIM, A_HEAD_DIM), F32)]
                       + [pltpu.VMEM((nb, tl, A_WIDTH), F32)] * 3,
        compiler_params=_params(2),
        name="hgrn2",
    )(pa, jnp.log(lb), jnp.log1p(-lb), 1.0 - lb, onorm_g.reshape(1, A_WIDTH), s0)


def _sb_kernel(*refs, ns, lq, n_past):
    if n_past:
        q_ref, k_ref, v_ref, g_ref, pk_ref, pv_ref, o_ref = refs[:7]
        scratch = refs[7:]
    else:
        q_ref, k_ref, v_ref, g_ref, o_ref = refs[:5]
        pk_ref = pv_ref = None
        scratch = refs[5:]
    kc_ref, vtc_ref, stk_ref, stv_ref, stq_ref, qtm_ref, carry_ref, acc_ref = scratch
    j = pl.program_id(1)
    kb_rows = KEY_BLOCK
    run_len = kb_rows // SUBLANES
    width = B_WIDTH
    n_halves = width // LANES
    streams = range(ns)

    def permuted(ref, s):
        return jnp.concatenate(
            [jnp.concatenate([ref[s, c, pl.ds(jj, SUBLANES, stride=run_len), :] for jj in range(run_len)], axis=0)
             for c in range(n_halves)], axis=1)

    def stage(dst_ref, s, rows):
        for c in range(n_halves):
            if rows.shape[0] < kb_rows:
                dst_ref[s, c, rows.shape[0]:kb_rows, :] = jnp.zeros((kb_rows - rows.shape[0], LANES), F32)
            dst_ref[s, c, 0:rows.shape[0], :] = rows[:, c * LANES:(c + 1) * LANES]

    row_head = lax.broadcasted_iota(jnp.int32, (width, kb_rows), 0) // B_HEAD_DIM
    for s in streams:
        stage(stk_ref, s, k_ref[s])
        stage(stv_ref, s, v_ref[s])
        stage(stq_ref, s, q_ref[s] * (B_HEAD_DIM ** -0.5))
    for s in streams:
        kc_ref[s, j] = permuted(stk_ref, s).astype(BF16)
        vtc_ref[s, j] = permuted(stv_ref, s).T.astype(BF16)
        qt = jnp.concatenate([stq_ref[s, c] for c in range(n_halves)], axis=1).T
        qtm_ref[s] = jnp.concatenate([jnp.where(row_head == h, qt, 0.0) for h in range(B_HEADS)],
                                     axis=1).astype(BF16)

    row = lax.broadcasted_iota(jnp.int32, (kb_rows, kb_rows), 0)
    lane = lax.broadcasted_iota(jnp.int32, (kb_rows, kb_rows), 1)
    causal = (row % SUBLANES) * run_len + row // SUBLANES < lane
    sub = lax.broadcasted_iota(jnp.int32, (SUBLANES, kb_rows), 0)

    carry_ref[...] = jnp.ones(carry_ref.shape, F32)
    acc_ref[...] = jnp.zeros(acc_ref.shape, F32)

    def process(blocks, masked):
        zts = [_dot(kb, qtm_ref[s]) for s, (kb, _) in enumerate(blocks)]
        ws = []
        for s, zt in enumerate(zts):
            for h in range(B_HEADS):
                ls = slice(h * kb_rows, (h + 1) * kb_rows)
                z = zt[:, ls]
                e = jnp.exp(-jnp.abs(z))
                r = 1.0 / (1.0 + e)
                er = e * r
                pos = z >= 0
                beta = jnp.where(pos, r, er)
                omb = jnp.where(pos, er, r)
                if masked:
                    beta = jnp.where(causal, beta, 0.0)
                    omb = jnp.where(causal, omb, 1.0)
                run = jnp.ones((SUBLANES, kb_rows), F32)
                excl = [None] * run_len
                for jj in reversed(range(run_len)):
                    excl[jj] = run
                    run = run * omb[jj * SUBLANES:(jj + 1) * SUBLANES]
                inc = run
                for d in (1, 2, 4):
                    inc = jnp.where(sub + d < SUBLANES, inc * pltpu.roll(inc, SUBLANES - d, 0), inc)
                carry = carry_ref[s, :, ls]
                off = jnp.where(sub < SUBLANES - 1, pltpu.roll(inc, SUBLANES - 1, 0), 1.0) * carry
                ws.append(jnp.concatenate([beta[jj * SUBLANES:(jj + 1) * SUBLANES] * (excl[jj] * off)
                                           for jj in range(run_len)], axis=0).astype(BF16))
                carry_ref[s, :, ls] = carry * jnp.broadcast_to(inc[0:1, :], (SUBLANES, kb_rows))
        for s, (_, vtb) in enumerate(blocks):
            for h in range(B_HEADS):
                hs = slice(h * B_HEAD_DIM, (h + 1) * B_HEAD_DIM)
                acc_ref[s, hs, :] += _dot(vtb[hs, :], ws[s * B_HEADS + h])

    def alive():
        return (jnp.max(carry_ref[...]) > 0.0).astype(jnp.int32)

    process([(kc_ref[s, j], vtc_ref[s, j]) for s in streams], True)

    def cond(state):
        kb, live = state
        return jnp.logical_and(kb >= 0, live > 0)

    def body(state):
        kb, _ = state
        process([(kc_ref[s, kb], vtc_ref[s, kb]) for s in streams], False)
        return kb - 1, alive()

    lax.while_loop(cond, body, (j - 1, alive()))

    for kb in reversed(range(n_past)):
        @pl.when(alive() > 0)
        def _():
            cols = slice(kb * kb_rows, (kb + 1) * kb_rows)
            for s in streams:
                kn = pk_ref[0, s, :, cols].T
                vn = pv_ref[0, s, :, cols].T
                for c in range(n_halves):
                    stk_ref[s, c] = kn[:, c * LANES:(c + 1) * LANES]
                    stv_ref[s, c] = vn[:, c * LANES:(c + 1) * LANES]
            process([(permuted(stk_ref, s).astype(BF16), permuted(stv_ref, s).T.astype(BF16)) for s in streams],
                    False)

    for s in streams:
        o = acc_ref[s].T[0:lq, :]
        o_ref[s] = (o * _silu(g_ref[s])).astype(BF16)


def _sb(q, k, v, g, past_t, layer):
    b, l, w = q.shape
    lq = min(KEY_BLOCK, l)
    nblk = l // lq
    ns = min(b, SB_SEQS)
    while b % ns:
        ns -= 1
    p = 0 if past_t is None else past_t[0].shape[3]
    blk = pl.BlockSpec((ns, lq, w), lambda i, j: (i, j, 0))
    in_specs = [blk] * 4
    args = [q, k, v, g]
    if p:
        in_specs += [pl.BlockSpec((1, ns, w, p), lambda i, j: (layer, i, 0, 0))] * 2
        args += list(past_t)
    staging = pltpu.VMEM((ns, w // LANES, KEY_BLOCK, LANES), F32)
    return pl.pallas_call(
        functools.partial(_sb_kernel, ns=ns, lq=lq, n_past=p // KEY_BLOCK),
        grid=(b // ns, nblk),
        in_specs=in_specs,
        out_specs=blk,
        out_shape=jax.ShapeDtypeStruct((b, l, w), BF16),
        scratch_shapes=[pltpu.VMEM((ns, nblk, KEY_BLOCK, w), BF16),
                        pltpu.VMEM((ns, nblk, w, KEY_BLOCK), BF16),
                        staging, staging, staging,
                        pltpu.VMEM((ns, w, B_HEADS * KEY_BLOCK), BF16),
                        pltpu.VMEM((ns, SUBLANES, B_HEADS * KEY_BLOCK), F32),
                        pltpu.VMEM((ns, w, KEY_BLOCK), F32)],
        compiler_params=_params(2),
        name="sb_attn",
    )(*args)


def _post_kernel(x_ref, a_ref, b_ref, pc_ref, pprev_ref, mk_hbm, mv_hbm, wpool_ref, pscale_ref,
                 woa_ref, wob_ref, woc_ref, npost_ref, nxpre_ref, wxq_ref, wxo_ref, nxpost_ref,
                 xo_ref, pnew_ref, ext_ref, kv_ref, mkb_ref, mvb_ref, sem_ref, *, nb, tl, n_split, past, layer):
    i = pl.program_id(0)
    j = pl.program_id(1)
    n_groups = pl.num_programs(0)
    halo = POOL_STATE + 1
    dh = kv_ref.shape[-1]

    def kv_copies(group, slot):
        return [pltpu.make_async_copy(src.at[layer, group * nb + n, :, h, :], kv_ref.at[slot, t, n, h],
                                      sem_ref.at[slot, t])
                for t, src in enumerate((mk_hbm, mv_hbm)) for n in range(nb) for h in range(X_HEADS)]

    @pl.when(j == 0)
    def _():
        slot = i % 2

        @pl.when(i == 0)
        def _():
            for cp in kv_copies(0, 0):
                cp.start()

        @pl.when(i + 1 < n_groups)
        def _():
            for cp in kv_copies(i + 1, 1 - slot):
                cp.start()

        for cp in kv_copies(i, slot):
            cp.wait()
        for n in range(nb):
            ext_ref[n, 0:halo, :] = pprev_ref[n]
            for h in range(X_HEADS):
                mkb_ref[n, :, h * dh:(h + 1) * dh] = kv_ref[slot, 0, n, h].astype(BF16)
                mvb_ref[n, :, h * dh:(h + 1) * dh] = kv_ref[slot, 1, n, h].astype(BF16)

    grp = lax.broadcasted_iota(jnp.int32, (tl, C_WIDTH), 1) // C_GROUP_DIM
    pos = past + j * tl + lax.broadcasted_iota(jnp.int32, (tl, C_WIDTH), 0)
    dds = []
    for n in range(nb):
        cu = pc_ref[n, :, 0:C_WIDTH]
        ext_ref[n, halo:halo + tl, :] = cu
        acc = cu
        win = jnp.zeros((tl, C_WIDTH), F32)
        wlen = jnp.zeros((tl, C_WIDTH), jnp.int32)
        for d in range(1, POOL_WINDOWS[-1]):
            acc = acc + ext_ref[n, pl.ds(halo - d, tl), :]
            if d + 1 in POOL_WINDOWS:
                gi = POOL_WINDOWS.index(d + 1)
                win = jnp.where(grp == gi, acc, win)
                wlen = jnp.where(grp == gi, d + 1, wlen)
        cnt = jnp.minimum(pos + 1, wlen).astype(F32)
        dds.append((win / cnt - cu).astype(BF16))
        pnew_ref[n] = ext_ref[n, tl + 1:tl + halo, :]
        ext_ref[n, 0:halo, :] = ext_ref[n, tl:tl + halo, :]

    seg = tl // n_split if nb == 1 else tl
    groups = [[(0, g * seg)] for g in range(n_split)] if nb == 1 else \
             [[(n, 0) for n in range(g * nb // n_split, (g + 1) * nb // n_split)] for g in range(n_split)]

    def rows(members, ref, w0=0, w1=None):
        return jnp.concatenate([ref[n, r0:r0 + seg, w0:w1] for n, r0 in members], axis=0)

    mixes = []
    for members in groups:
        dd = jnp.concatenate([dds[n][r0:r0 + seg] for n, r0 in members], axis=0)
        y = _dot(dd, wpool_ref[...]) * pscale_ref[...]
        c_out = (y * _silu(rows(members, pc_ref, C_WIDTH, 2 * C_WIDTH))).astype(BF16)
        mixes.append(_dot(rows(members, a_ref), woa_ref[...]) + _dot(rows(members, b_ref), wob_ref[...])
                     + _dot(c_out, woc_ref[...]))
    x1s = [rows(members, x_ref) + _rms(mix, npost_ref[...]) for members, mix in zip(groups, mixes)]

    qs = [_dot(_rms(x1, nxpre_ref[...]).astype(BF16), wxq_ref[...]).astype(BF16) for x1 in x1s]
    pairs = [(g, m, n, h) for g, members in enumerate(groups) for m, (n, _) in enumerate(members)
             for h in range(X_HEADS)]
    scores = [lax.dot_general(qs[g][m * seg:(m + 1) * seg, h * dh:(h + 1) * dh],
                              mkb_ref[n, :, h * dh:(h + 1) * dh], NT_DIMS,
                              preferred_element_type=F32) * (dh ** -0.5) for g, m, n, h in pairs]
    probs = []
    for s in scores:
        ex = jnp.exp(s - jnp.max(s, axis=-1, keepdims=True))
        probs.append((ex / jnp.sum(ex, axis=-1, keepdims=True)).astype(BF16))
    ctx = [_dot(p, mvb_ref[n, :, h * dh:(h + 1) * dh]).astype(BF16) for p, (g, m, n, h) in zip(probs, pairs)]
    xos = []
    for g, members in enumerate(groups):
        seqs = [jnp.concatenate([c for c, (g2, m2, _, _) in zip(ctx, pairs) if g2 == g and m2 == m], axis=1)
                for m in range(len(members))]
        xos.append(_dot(jnp.concatenate(seqs, axis=0), wxo_ref[...]))
    for members, x1, xo in zip(groups, x1s, xos):
        res = x1 + _rms(xo, nxpost_ref[...])
        for m, (n, r0) in enumerate(members):
            xo_ref[n, r0:r0 + seg, :] = res[m * seg:(m + 1) * seg]


def _post(x, a_out, b_out, pc, pool_prev, mk, mv, wts, past, layer):
    b, l, d = x.shape
    n_mem, dh = mk.shape[2], mk.shape[4]
    tl = POST_ROWS if l % POST_ROWS == 0 else CHUNK
    nb = max(1, min(b, POST_ROWS // tl, POST_SEQS))
    while b % nb:
        nb -= 1
    n_split = 2 if (nb * tl) % (2 * LANES) == 0 and (nb == 1 or nb % 2 == 0) else 1
    halo = POOL_STATE + 1
    pprev = jnp.concatenate([jnp.zeros((b, 1, C_WIDTH), F32), pool_prev], axis=1)
    tile = lambda w: pl.BlockSpec((nb, tl, w), lambda i, j: (i, j, 0))
    per_seq = lambda r, w: pl.BlockSpec((nb, r, w), lambda i, j: (i, 0, 0))
    full = lambda a: pl.BlockSpec(a.shape, lambda i, j: (0,) * a.ndim)
    hbm = pl.BlockSpec(memory_space=pl.ANY)
    return pl.pallas_call(
        functools.partial(_post_kernel, nb=nb, tl=tl, n_split=n_split, past=past, layer=layer),
        grid=(b // nb, l // tl),
        in_specs=[tile(d), tile(A_WIDTH), tile(B_WIDTH), tile(2 * C_WIDTH), per_seq(halo, C_WIDTH),
                  hbm, hbm] + [full(a) for a in wts],
        out_specs=[tile(d), per_seq(POOL_STATE, C_WIDTH)],
        out_shape=[jax.ShapeDtypeStruct((b, l, d), F32),
                   jax.ShapeDtypeStruct((b, POOL_STATE, C_WIDTH), F32)],
        scratch_shapes=[pltpu.VMEM((nb, halo + tl, C_WIDTH), F32),
                        pltpu.VMEM((2, 2, nb, X_HEADS, n_mem, dh), F32),
                        pltpu.VMEM((nb, n_mem, d), BF16),
                        pltpu.VMEM((nb, n_mem, d), BF16),
                        pltpu.SemaphoreType.DMA((2, 2))],
        compiler_params=_params(2),
        name="mix_out",
    )(x, a_out, b_out, pc, pprev, mk, mv, *wts)


def _layer(x, mk, mv, past_t, s0, pool_prev, lw, layer, depth, kv_t):
    b, l, d = x.shape
    n = b * l
    (pa, bq, bk, bv, bg, pc), kv_t = _proj(x.reshape(n, d), lw["n_pre"], lw["w_in"], l, layer, depth, kv_t)
    a_out, s_new = _hgrn(pa.reshape(b, l, -1), lw["lb"], lw["onorm_g"], s0)
    seq = lambda t: t.reshape(b, l, -1)
    b_out = _sb(seq(bq), seq(bk), seq(bv), seq(bg), past_t, layer)
    past = 0 if past_t is None else past_t[0].shape[3]
    x_new, pool_new = _post(x, a_out, b_out, seq(pc), pool_prev, mk, mv, lw["post"], past, layer)
    heads = lambda t: t.reshape(b, l, B_HEADS, B_HEAD_DIM)
    return x_new, heads(bk), heads(bv), kv_t, s_new, pool_new


def kernel(x_prompt, x_sample, mem_prompt, cache_sb_k, cache_sb_v, state_hgrn, state_pool, cache_mem_k, cache_mem_v, norm_mix_pre, norm_mix_post, w_in, hgrn_lb_logits, hgrn_onorm_g, w_pool, pool_scale, w_out, norm_x_pre, norm_x_post, norm_mem, w_xq, w_xk, w_xv, w_xo):
    depth, d = norm_mix_pre.shape
    bp = x_prompt.shape[0]
    bs, past = cache_sb_k.shape[1], cache_sb_k.shape[2]
    n_mem = mem_prompt.shape[1]

    lbs = jax.nn.softmax(hgrn_lb_logits.astype(F32), axis=0)
    lower_bounds = jnp.maximum(jnp.cumsum(lbs, axis=0) - lbs[0], 0.0)

    mk_all, mv_all = _memkv(mem_prompt, norm_mem, w_xk.astype(BF16), w_xv.astype(BF16))

    def seq_minor(c):
        return jnp.transpose(c, (0, 1, 3, 4, 2)).reshape(depth, bs, B_WIDTH, past)

    def seq_major(t):
        dp, b, _, l = t.shape
        return jnp.transpose(t.reshape(dp, b, B_HEADS, B_HEAD_DIM, l), (0, 1, 4, 2, 3))

    past_t = (seq_minor(cache_sb_k), seq_minor(cache_sb_v))

    row = lambda v: v.reshape(1, -1)
    s_zero = jnp.zeros((bp, A_HEADS, A_HEAD_DIM, A_HEAD_DIM), F32)
    pool_zero = jnp.zeros((bp, POOL_STATE, C_WIDTH), F32)

    xp, xs = x_prompt, x_sample
    outs = [[] for _ in range(10)]
    kvt_p = kvt_s = None
    for l in range(depth):
        wp = jnp.zeros((C_WIDTH, C_WIDTH), F32)
        for gi in range(C_GROUPS):
            gs = slice(gi * C_GROUP_DIM, (gi + 1) * C_GROUP_DIM)
            wp = wp.at[gs, gs].set(w_pool[l, gi])
        wo = w_out[l].astype(BF16)
        lw = {
            "n_pre": norm_mix_pre[l], "w_in": w_in[l].astype(BF16), "lb": lower_bounds[l],
            "onorm_g": hgrn_onorm_g[l],
            "post": (wp.astype(BF16), row(pool_scale[l]), wo[0:A_WIDTH], wo[A_WIDTH:A_WIDTH + B_WIDTH],
                     wo[A_WIDTH + B_WIDTH:], row(norm_mix_post[l]), row(norm_x_pre[l]),
                     w_xq[l].astype(BF16), w_xo[l].astype(BF16), row(norm_x_post[l])),
        }
        xp, kb, vb, kvt_p, sn, pn = _layer(xp, mk_all, mv_all, None, s_zero, pool_zero, lw, l, depth, kvt_p)
        for o, t in zip(outs[:4], (kb, vb, sn, pn)):
            o.append(t)
        xs, kb, vb, kvt_s, sn, pn = _layer(xs, cache_mem_k, cache_mem_v, past_t, state_hgrn[l], state_pool[l],
                                           lw, l, depth, kvt_s)
        for o, t in zip(outs[6:], (kb, vb, sn, pn)):
            o.append(t)

    def new_kv(kvt, nat_k, nat_v):
        if kvt is not None:
            return seq_major(kvt[0]), seq_major(kvt[1])
        return jnp.stack(nat_k), jnp.stack(nat_v)

    p_k, p_v = new_kv(kvt_p, outs[0], outs[1])
    s_k, s_v = new_kv(kvt_s, outs[6], outs[7])
    p_s, p_pool = jnp.stack(outs[2]), jnp.stack(outs[3])
    s_s, s_pool = jnp.stack(outs[8]), jnp.stack(outs[9])
    return (xp, xs, p_k, p_v, p_s, p_pool, mk_all, mv_all, s_k, s_v, s_s, s_pool)
```

```python
import functools

import jax
import jax.numpy as jnp
from jax import lax
from jax.experimental import pallas as pl
from jax.experimental.pallas import tpu as pltpu

F32 = jnp.float32
BF16 = jnp.bfloat16
EPS = 1e-6

A_HEADS, A_HEAD_DIM = 4, 128
A_WIDTH = A_HEADS * A_HEAD_DIM
B_HEADS, B_HEAD_DIM = 4, 64
B_WIDTH = B_HEADS * B_HEAD_DIM
C_GROUPS, C_GROUP_DIM = 4, 64
C_WIDTH = C_GROUPS * C_GROUP_DIM
POOL_WINDOWS = (2, 4, 8, 16)
POOL_STATE = 15
X_HEADS = 4
CHUNK = 64
SUB = 16
SAFE_SPAN = 40.0
KEY_BLOCK = 128
SB_SEQS = 4
HGRN_ROWS = 512
POST_ROWS = 512
POST_SEQS = 4
SUBLANES = 8
LANES = 128
VMEM_LIMIT = 56 * 1024 * 1024

NT_DIMS = (((1,), (1,)), ((), ()))
TN_DIMS = (((0,), (0,)), ((), ()))


def _rms(x, g):
    ms = jnp.mean(x * x, axis=-1, keepdims=True)
    return x * lax.rsqrt(ms + EPS) * g


def _silu(x):
    return x * (1.0 / (1.0 + jnp.exp(-x)))


def _dot(a, b):
    return jnp.dot(a, b, preferred_element_type=F32)


def _params(n_grid, flags=None):
    return pltpu.CompilerParams(dimension_semantics=("arbitrary",) * n_grid,
                                vmem_limit_bytes=VMEM_LIMIT, flags=flags)


def _memkv_kernel(mem_ref, g_ref, wk_ref, wv_ref, k_ref, v_ref):
    nb, n_mem, d = mem_ref.shape
    dh = d // X_HEADS
    for b in range(nb):
        m = _rms(mem_ref[b], g_ref[0]).astype(BF16)
        k = _dot(m, wk_ref[0])
        v = _dot(m, wv_ref[0])
        for h in range(X_HEADS):
            k_ref[0, b, :, h, :] = k[:, h * dh:(h + 1) * dh]
            v_ref[0, b, :, h, :] = v[:, h * dh:(h + 1) * dh]


def _memkv(mem, norm_mem, wk, wv):
    depth, d = norm_mem.shape
    nb, n_mem, _ = mem.shape
    dh = d // X_HEADS
    out = jax.ShapeDtypeStruct((depth, nb, n_mem, X_HEADS, dh), F32)
    return pl.pallas_call(
        _memkv_kernel,
        grid=(depth,),
        in_specs=[pl.BlockSpec((nb, n_mem, d), lambda l: (0, 0, 0)),
                  pl.BlockSpec((1, 1, d), lambda l: (l, 0, 0)),
                  pl.BlockSpec((1, d, d), lambda l: (l, 0, 0)),
                  pl.BlockSpec((1, d, d), lambda l: (l, 0, 0))],
        out_specs=[pl.BlockSpec((1, nb, n_mem, X_HEADS, dh), lambda l: (l, 0, 0, 0, 0))] * 2,
        out_shape=[out, out],
        compiler_params=_params(1),
        name="mem_kv",
    )(mem, norm_mem.reshape(depth, 1, d), wk, wv)


def _proj_kernel(*refs, transposed, aliased):
    x_ref, g_ref, w_ref = refs[:3]
    outs = refs[3 + 2 * aliased:]
    pa_ref, q_ref, k_ref, v_ref, bg_ref, pc_ref = outs[:6]
    a_end = 4 * A_WIDTH
    b_end = a_end + 4 * B_WIDTH
    tm = x_ref.shape[0]
    n_groups = 2 if tm % (2 * LANES) == 0 else 1
    groups = [slice(g * tm // n_groups, (g + 1) * tm // n_groups) for g in range(n_groups)]
    hs = [_rms(x_ref[rows, :], g_ref[...]).astype(BF16) for rows in groups]
    for rows, h in zip(groups, hs):
        pa_ref[rows, :] = _dot(h, w_ref[:, 0:a_end])
        pb = _dot(h, w_ref[:, a_end:b_end])
        q_ref[rows, :] = pb[:, 0:B_WIDTH]
        k_ref[rows, :] = pb[:, B_WIDTH:2 * B_WIDTH]
        v_ref[rows, :] = pb[:, 2 * B_WIDTH:3 * B_WIDTH]
        bg_ref[rows, :] = pb[:, 3 * B_WIDTH:4 * B_WIDTH]
        pc_ref[rows, :] = _dot(h, w_ref[:, b_end:])
        if transposed:
            kt_ref, vt_ref = outs[6:]
            kt_ref[0, 0, :, rows] = pb[:, B_WIDTH:2 * B_WIDTH].T
            vt_ref[0, 0, :, rows] = pb[:, 2 * B_WIDTH:3 * B_WIDTH].T


def _proj(x2d, g, w_in, seq_len, layer, depth, kv_t):
    n, d = x2d.shape
    d_in = w_in.shape[1]
    tm = 512 if n % 512 == 0 else CHUNK
    transposed = seq_len % tm == 0 and tm % LANES == 0
    aliased = transposed and kv_t is not None
    row = lambda width: pl.BlockSpec((tm, width), lambda i: (i, 0))
    shapes = [4 * A_WIDTH, B_WIDTH, B_WIDTH, B_WIDTH, B_WIDTH, 2 * C_WIDTH]
    in_specs = [row(d), pl.BlockSpec((1, d), lambda i: (0, 0)), pl.BlockSpec((d, d_in), lambda i: (0, 0))]
    args = [x2d, g.reshape(1, d), w_in]
    out_specs = [row(w) for w in shapes]
    out_shape = [jax.ShapeDtypeStruct((n, w), F32) for w in shapes]
    aliases = {}
    if transposed:
        tiles = seq_len // tm
        t_spec = pl.BlockSpec((1, 1, B_WIDTH, tm), lambda i: (layer, i // tiles, 0, i % tiles))
        out_specs += [t_spec, t_spec]
        out_shape += [jax.ShapeDtypeStruct((depth, n // seq_len, B_WIDTH, seq_len), F32)] * 2
        if aliased:
            in_specs += [pl.BlockSpec(memory_space=pl.ANY)] * 2
            args += list(kv_t)
            aliases = {3: 6, 4: 7}
    res = pl.pallas_call(
        functools.partial(_proj_kernel, transposed=transposed, aliased=aliased),
        grid=(n // tm,),
        in_specs=in_specs,
        out_specs=out_specs,
        out_shape=out_shape,
        input_output_aliases=aliases,
        compiler_params=_params(1),
        name="in_proj",
    )(*args)
    return res[:6], (tuple(res[6:]) if transposed else None)


def _hgrn_kernel(*refs, nb, nch, aliased):
    pa_ref, loglb_ref, log1mlb_ref, omlb_ref, og_ref, s0_ref = refs[:6]
    a_ref, sout_ref, st_ref, g_ref, q_ref, k_ref = refs[6 + aliased:]
    _hgrn_body(pa_ref, loglb_ref, log1mlb_ref, omlb_ref, og_ref, s0_ref, a_ref, sout_ref,
               st_ref, g_ref, q_ref, k_ref, nb=nb, nch=nch)


def _hgrn_body(pa_ref, loglb_ref, log1mlb_ref, omlb_ref, og_ref, s0_ref, a_ref, sout_ref,
               st_ref, g_ref, q_ref, k_ref, *, nb, nch):
    j = pl.program_id(1)

    @pl.when(j == 0)
    def _():
        for n in range(nb):
            for h in range(A_HEADS):
                st_ref[n, h] = s0_ref[0, n, h].T

    rowc = lax.broadcasted_iota(jnp.int32, (CHUNK, CHUNK), 0)
    colc = lax.broadcasted_iota(jnp.int32, (CHUNK, CHUNK), 1)
    tri = (rowc >= colc).astype(F32)
    col_sub = colc // SUB
    row_k = lax.broadcasted_iota(jnp.int32, (CHUNK, A_HEAD_DIM), 0)
    lane_d = lax.broadcasted_iota(jnp.int32, (SUB, CHUNK), 1)
    n_sub = CHUNK // SUB
    head = [slice(h * A_HEAD_DIM, (h + 1) * A_HEAD_DIM) for h in range(A_HEADS)]

    def sub_starts(g_all):
        return [jnp.zeros((1, A_WIDTH), F32)] + [g_all[SUB * s - 1:SUB * s, :] for s in range(1, n_sub)]

    span = jnp.zeros((1, A_WIDTH), F32)
    for n in range(nb):
        for c in range(nch):
            rs = slice(c * CHUNK, (c + 1) * CHUNK)
            aq = pa_ref[n, rs, 0:A_WIDTH]
            af = pa_ref[n, rs, A_WIDTH:2 * A_WIDTH]
            e = jnp.exp(-jnp.abs(af))
            ope = 1.0 + e
            logsig = jnp.minimum(af, 0.0) - jnp.log(ope)
            r = 1.0 / ope
            sig_neg = jnp.where(af >= 0, e * r, r)
            la = loglb_ref[...]
            lb_ = log1mlb_ref[...] + logsig
            log_f = jnp.maximum(la, lb_) + jnp.log(1.0 + jnp.exp(-jnp.abs(la - lb_)))
            g_all = jnp.dot(tri, log_f, precision=lax.Precision.HIGHEST, preferred_element_type=F32)
            g_ref[n, rs, :] = g_all
            k_ref[n, rs, :] = omlb_ref[...] * sig_neg
            q_ref[n, rs, :] = _silu(aq) * (A_HEAD_DIM ** -0.5)
            starts = sub_starts(g_all)
            for s in range(n_sub):
                span = jnp.maximum(span, starts[s] - g_all[SUB * s + SUB - 1:SUB * s + SUB, :])
    small_decay = jnp.max(span) < SAFE_SPAN

    def att_factored(g_all, q_all, k_all, qg_all):
        starts = sub_starts(g_all)
        operands = []
        for sl in head:
            gh, qh, kh = g_all[:, sl], q_all[:, sl], k_all[:, sl]
            parts, krows = [qg_all[:, sl]], []
            for s in range(n_sub):
                rs = slice(SUB * s, SUB * (s + 1))
                st = starts[s][:, sl]
                krows.append(kh[rs] * jnp.exp(st - gh[rs]))
                if s:
                    parts.append(jnp.where(row_k >= SUB * s, qh * jnp.exp(jnp.minimum(gh - st, 0.0)), 0.0))
            operands.append((jnp.concatenate(parts, axis=0).astype(BF16),
                             jnp.concatenate(krows, axis=0).astype(BF16)))
        rrs = [lax.dot_general(qr, kr, NT_DIMS, preferred_element_type=F32) for qr, kr in operands]
        atts = []
        for rr in rrs:
            att = rr[0:CHUNK, :]
            for s in range(1, n_sub):
                att = jnp.where(col_sub == s, rr[s * CHUNK:(s + 1) * CHUNK, :], att)
            atts.append(jnp.where(rowc >= colc, att, 0.0))
        return atts

    def att_exact(g_all, q_all, k_all, qg_all):
        def one_head(sl):
            gh, qh, kh = g_all[:, sl], q_all[:, sl], k_all[:, sl]
            parts, refrows = [], []
            for jsub in range(n_sub):
                ref = gh[SUB * jsub + SUB - 1:SUB * jsub + SUB, :]
                refrows.append(jnp.broadcast_to(ref, (SUB, A_HEAD_DIM)))
                if jsub < n_sub - 1:
                    later = row_k >= SUB * (jsub + 1)
                    parts.append(jnp.where(later, qh * jnp.exp(jnp.minimum(gh - ref, 0.0)), 0.0))
            q_rel = jnp.concatenate(parts, axis=0).astype(BF16)
            k_rel = (kh * jnp.exp(jnp.minimum(jnp.concatenate(refrows, axis=0) - gh, 0.0))).astype(BF16)
            rr = lax.dot_general(q_rel, k_rel, NT_DIMS, preferred_element_type=F32)
            att = jnp.zeros((CHUNK, CHUNK), F32)
            for jsub in range(n_sub - 1):
                att = jnp.where(col_sub == jsub, rr[jsub * CHUNK:(jsub + 1) * CHUNK, :], att)

            dparts = []
            for csub in range(n_sub):
                rs = slice(SUB * csub, SUB * (csub + 1))
                gc, qc, kc = gh[rs], qh[rs], kh[rs]
                accd = jnp.zeros((SUB, CHUNK), F32)
                for s in range(SUB):
                    dec = jnp.exp(jnp.minimum(gc - gc[s:s + 1, :], 0.0))
                    col = jnp.sum(dec * qc * kc[s:s + 1, :], axis=-1, keepdims=True)
                    accd = jnp.where(lane_d == SUB * csub + s, col, accd)
                dparts.append(accd)
            return att + jnp.where(rowc >= colc, jnp.concatenate(dparts, axis=0), 0.0)

        return [one_head(sl) for sl in head]

    def chunk(n, r0, sts, att_fn):
        g_all = g_ref[n, pl.ds(r0, CHUNK), :]
        q_all = q_ref[n, pl.ds(r0, CHUNK), :]
        k_all = k_ref[n, pl.ds(r0, CHUNK), :]
        v_all = pa_ref[n, pl.ds(r0, CHUNK), 2 * A_WIDTH:3 * A_WIDTH].astype(BF16)
        ag = pa_ref[n, pl.ds(r0, CHUNK), 3 * A_WIDTH:4 * A_WIDTH]
        qg_all = q_all * jnp.exp(g_all)
        gl_all = g_all[CHUNK - 1:CHUNK, :]
        kd_all = (k_all * jnp.exp(gl_all - g_all)).astype(BF16)
        qg_bf = qg_all.astype(BF16)
        atts = att_fn(g_all, q_all, k_all, qg_all)

        o_state = [lax.dot_general(qg_bf[:, sl], st.astype(BF16), NT_DIMS, preferred_element_type=F32)
                   for sl, st in zip(head, sts)]
        o_pairs = [_dot(att.astype(BF16), v_all[:, sl]) for sl, att in zip(head, atts)]
        updates = [lax.dot_general(v_all[:, sl], kd_all[:, sl], TN_DIMS, preferred_element_type=F32)
                   for sl in head]
        new_sts = [st * jnp.exp(gl_all[:, sl]) + up for sl, st, up in zip(head, sts, updates)]
        o = jnp.concatenate([a + b for a, b in zip(o_state, o_pairs)], axis=1)
        sq = o * o
        inv = jnp.concatenate(
            [jnp.broadcast_to(lax.rsqrt(jnp.mean(sq[:, sl], axis=-1, keepdims=True) + EPS), (CHUNK, A_HEAD_DIM))
             for sl in head], axis=1)
        a_ref[n, pl.ds(r0, CHUNK), :] = (o * inv * og_ref[...] * _silu(ag)).astype(BF16)
        return new_sts

    @pl.when(small_decay)
    def _():
        for n in range(nb):
            sts = [st_ref[n, h] for h in range(A_HEADS)]
            for c in range(nch):
                sts = chunk(n, c * CHUNK, sts, att_factored)
            for h in range(A_HEADS):
                st_ref[n, h] = sts[h]

    @pl.when(jnp.logical_not(small_decay))
    def _():
        for n in range(nb):
            def body(c, carry):
                sts = chunk(n, pl.multiple_of(c * CHUNK, CHUNK), [st_ref[n, h] for h in range(A_HEADS)], att_exact)
                for h in range(A_HEADS):
                    st_ref[n, h] = sts[h]
                return carry
            lax.fori_loop(0, nch, body, 0)

    @pl.when(j == pl.num_programs(1) - 1)
    def _():
        for n in range(nb):
            for h in range(A_HEADS):
                sout_ref[0, n, h] = st_ref[n, h].T


def _hgrn(pa, lb, onorm_g, s0, s0_layer, layer, depth, s_all):
    b, l, wa = pa.shape
    tl = HGRN_ROWS if l % HGRN_ROWS == 0 else CHUNK
    nb = max(1, min(b, HGRN_ROWS // tl))
    while b % nb:
        nb -= 1
    vec = lambda: pl.BlockSpec((1, A_WIDTH), lambda i, j: (0, 0))
    st_block = (1, nb, A_HEADS, A_HEAD_DIM, A_HEAD_DIM)
    lb = lb.reshape(1, A_WIDTH)
    aliased = s_all is not None
    in_specs = [pl.BlockSpec((nb, tl, wa), lambda i, j: (i, j, 0)), vec(), vec(), vec(), vec(),
                pl.BlockSpec(st_block, lambda i, j: (s0_layer, i, 0, 0, 0))]
    args = [pa, jnp.log(lb), jnp.log1p(-lb), 1.0 - lb, onorm_g.reshape(1, A_WIDTH), s0]
    if aliased:
        in_specs.append(pl.BlockSpec(memory_space=pl.ANY))
        args.append(s_all)
    return pl.pallas_call(
        functools.partial(_hgrn_kernel, nb=nb, nch=tl // CHUNK, aliased=aliased),
        grid=(b // nb, l // tl),
        in_specs=in_specs,
        out_specs=[pl.BlockSpec((nb, tl, A_WIDTH), lambda i, j: (i, j, 0)),
                   pl.BlockSpec(st_block, lambda i, j: (layer, i, 0, 0, 0))],
        out_shape=[jax.ShapeDtypeStruct((b, l, A_WIDTH), BF16),
                   jax.ShapeDtypeStruct((depth, b, A_HEADS, A_HEAD_DIM, A_HEAD_DIM), F32)],
        scratch_shapes=[pltpu.VMEM((nb, A_HEADS, A_HEAD_DIM, A_HEAD_DIM), F32)]
                       + [pltpu.VMEM((nb, tl, A_WIDTH), F32)] * 3,
        input_output_aliases={6: 1} if aliased else {},
        compiler_params=_params(2),
        name="hgrn2",
    )(*args)


def _sb_kernel(*refs, ns, lq, n_past):
    if n_past:
        q_ref, k_ref, v_ref, g_ref, pk_ref, pv_ref, o_ref = refs[:7]
        scratch = refs[7:]
    else:
        q_ref, k_ref, v_ref, g_ref, o_ref = refs[:5]
        pk_ref = pv_ref = None
        scratch = refs[5:]
    kc_ref, vtc_ref, stk_ref, stv_ref, stq_ref, qtm_ref, carry_ref, acc_ref = scratch
    j = pl.program_id(1)
    kb_rows = KEY_BLOCK
    run_len = kb_rows // SUBLANES
    width = B_WIDTH
    n_halves = width // LANES
    streams = range(ns)

    def permuted(ref, s):
        return jnp.concatenate(
            [jnp.concatenate([ref[s, c, pl.ds(jj, SUBLANES, stride=run_len), :] for jj in range(run_len)], axis=0)
             for c in range(n_halves)], axis=1)

    def stage(dst_ref, s, rows):
        for c in range(n_halves):
            if rows.shape[0] < kb_rows:
                dst_ref[s, c, rows.shape[0]:kb_rows, :] = jnp.zeros((kb_rows - rows.shape[0], LANES), F32)
            dst_ref[s, c, 0:rows.shape[0], :] = rows[:, c * LANES:(c + 1) * LANES]

    row_head = lax.broadcasted_iota(jnp.int32, (width, kb_rows), 0) // B_HEAD_DIM
    for s in streams:
        stage(stk_ref, s, k_ref[s])
        stage(stv_ref, s, v_ref[s])
        stage(stq_ref, s, q_ref[s] * (0.5 * B_HEAD_DIM ** -0.5))
    for s in streams:
        kc_ref[s, j] = permuted(stk_ref, s).astype(BF16)
        vtc_ref[s, j] = permuted(stv_ref, s).T.astype(BF16)
        qt = jnp.concatenate([stq_ref[s, c] for c in range(n_halves)], axis=1).T
        qtm_ref[s] = jnp.concatenate([jnp.where(row_head == h, qt, 0.0) for h in range(B_HEADS)],
                                     axis=1).astype(BF16)

    row = lax.broadcasted_iota(jnp.int32, (kb_rows, kb_rows), 0)
    lane = lax.broadcasted_iota(jnp.int32, (kb_rows, kb_rows), 1)
    causal = (row % SUBLANES) * run_len + row // SUBLANES < lane
    sub = lax.broadcasted_iota(jnp.int32, (SUBLANES, kb_rows), 0)

    carry_ref[...] = jnp.ones(carry_ref.shape, F32)
    acc_ref[...] = jnp.zeros(acc_ref.shape, F32)

    def process(blocks, masked):
        zts = [_dot(kb, qtm_ref[s]) for s, (kb, _) in enumerate(blocks)]
        ws = []
        for s, zt in enumerate(zts):
            for h in range(B_HEADS):
                ls = slice(h * kb_rows, (h + 1) * kb_rows)
                half_t = 0.5 * jnp.tanh(zt[:, ls])
                beta = 0.5 + half_t
                omb = 0.5 - half_t
                if masked:
                    beta = jnp.where(causal, beta, 0.0)
                    omb = jnp.where(causal, omb, 1.0)
                run = jnp.ones((SUBLANES, kb_rows), F32)
                excl = [None] * run_len
                for jj in reversed(range(run_len)):
                    excl[jj] = run
                    run = run * omb[jj * SUBLANES:(jj + 1) * SUBLANES]
                inc = run
                for d in (1, 2, 4):
                    inc = jnp.where(sub + d < SUBLANES, inc * pltpu.roll(inc, SUBLANES - d, 0), inc)
                carry = carry_ref[s, :, ls]
                off = jnp.where(sub < SUBLANES - 1, pltpu.roll(inc, SUBLANES - 1, 0), 1.0) * carry
                ws.append(jnp.concatenate([beta[jj * SUBLANES:(jj + 1) * SUBLANES] * (excl[jj] * off)
                                           for jj in range(run_len)], axis=0).astype(BF16))
                carry_ref[s, :, ls] = carry * jnp.broadcast_to(inc[0:1, :], (SUBLANES, kb_rows))
        for s, (_, vtb) in enumerate(blocks):
            for h in range(B_HEADS):
                hs = slice(h * B_HEAD_DIM, (h + 1) * B_HEAD_DIM)
                acc_ref[s, hs, :] += _dot(vtb[hs, :], ws[s * B_HEADS + h])

    def alive():
        return (jnp.max(carry_ref[...]) > 0.0).astype(jnp.int32)

    process([(kc_ref[s, j], vtc_ref[s, j]) for s in streams], True)

    def cond(state):
        kb, live = state
        return jnp.logical_and(kb >= 0, live > 0)

    def body(state):
        kb, _ = state
        process([(kc_ref[s, kb], vtc_ref[s, kb]) for s in streams], False)
        return kb - 1, alive()

    lax.while_loop(cond, body, (j - 1, alive()))

    for kb in reversed(range(n_past)):
        @pl.when(alive() > 0)
        def _():
            cols = slice(kb * kb_rows, (kb + 1) * kb_rows)
            for s in streams:
                kn = pk_ref[0, s, :, cols].T
                vn = pv_ref[0, s, :, cols].T
                for c in range(n_halves):
                    stk_ref[s, c] = kn[:, c * LANES:(c + 1) * LANES]
                    stv_ref[s, c] = vn[:, c * LANES:(c + 1) * LANES]
            process([(permuted(stk_ref, s).astype(BF16), permuted(stv_ref, s).T.astype(BF16)) for s in streams],
                    False)

    for s in streams:
        o = acc_ref[s].T[0:lq, :]
        o_ref[s] = (o * _silu(g_ref[s])).astype(BF16)


def _sb(q, k, v, g, past_t, layer):
    b, l, w = q.shape
    lq = min(KEY_BLOCK, l)
    nblk = l // lq
    ns = min(b, SB_SEQS)
    while b % ns:
        ns -= 1
    p = 0 if past_t is None else past_t[0].shape[3]
    blk = pl.BlockSpec((ns, lq, w), lambda i, j: (i, j, 0))
    in_specs = [blk] * 4
    args = [q, k, v, g]
    if p:
        in_specs += [pl.BlockSpec((1, ns, w, p), lambda i, j: (layer, i, 0, 0))] * 2
        args += list(past_t)
    staging = pltpu.VMEM((ns, w // LANES, KEY_BLOCK, LANES), F32)
    return pl.pallas_call(
        functools.partial(_sb_kernel, ns=ns, lq=lq, n_past=p // KEY_BLOCK),
        grid=(b // ns, nblk),
        in_specs=in_specs,
        out_specs=blk,
        out_shape=jax.ShapeDtypeStruct((b, l, w), BF16),
        scratch_shapes=[pltpu.VMEM((ns, nblk, KEY_BLOCK, w), BF16),
                        pltpu.VMEM((ns, nblk, w, KEY_BLOCK), BF16),
                        staging, staging, staging,
                        pltpu.VMEM((ns, w, B_HEADS * KEY_BLOCK), BF16),
                        pltpu.VMEM((ns, SUBLANES, B_HEADS * KEY_BLOCK), F32),
                        pltpu.VMEM((ns, w, KEY_BLOCK), F32)],
        compiler_params=_params(2),
        name="sb_attn",
    )(*args)


def _post_kernel(x_ref, a_ref, b_ref, pc_ref, pprev_ref, mk_hbm, mv_hbm, wpool_ref, pscale_ref,
                 woa_ref, wob_ref, woc_ref, npost_ref, nxpre_ref, wxq_ref, wxo_ref, nxpost_ref,
                 xo_ref, pnew_ref, ext_ref, kv_ref, mkb_ref, mvb_ref, sem_ref, *, nb, tl, n_split, past, layer):
    i = pl.program_id(0)
    j = pl.program_id(1)
    n_groups = pl.num_programs(0)
    halo = POOL_STATE + 1
    dh = kv_ref.shape[-1]

    def kv_copies(group, slot):
        return [pltpu.make_async_copy(src.at[layer, group * nb + n, :, h, :], kv_ref.at[slot, t, n, h],
                                      sem_ref.at[slot, t])
                for t, src in enumerate((mk_hbm, mv_hbm)) for n in range(nb) for h in range(X_HEADS)]

    @pl.when(j == 0)
    def _():
        slot = i % 2

        @pl.when(i == 0)
        def _():
            for cp in kv_copies(0, 0):
                cp.start()

        @pl.when(i + 1 < n_groups)
        def _():
            for cp in kv_copies(i + 1, 1 - slot):
                cp.start()

        for cp in kv_copies(i, slot):
            cp.wait()
        for n in range(nb):
            ext_ref[n, 0:halo, :] = pprev_ref[n]
            for h in range(X_HEADS):
                mkb_ref[n, :, h * dh:(h + 1) * dh] = kv_ref[slot, 0, n, h].astype(BF16)
                mvb_ref[n, :, h * dh:(h + 1) * dh] = kv_ref[slot, 1, n, h].astype(BF16)

    grp = lax.broadcasted_iota(jnp.int32, (tl, C_WIDTH), 1) // C_GROUP_DIM
    pos = past + j * tl + lax.broadcasted_iota(jnp.int32, (tl, C_WIDTH), 0)
    dds = []
    for n in range(nb):
        cu = pc_ref[n, :, 0:C_WIDTH]
        ext_ref[n, halo:halo + tl, :] = cu
        acc = cu
        win = jnp.zeros((tl, C_WIDTH), F32)
        wlen = jnp.zeros((tl, C_WIDTH), jnp.int32)
        for d in range(1, POOL_WINDOWS[-1]):
            acc = acc + ext_ref[n, pl.ds(halo - d, tl), :]
            if d + 1 in POOL_WINDOWS:
                gi = POOL_WINDOWS.index(d + 1)
                win = jnp.where(grp == gi, acc, win)
                wlen = jnp.where(grp == gi, d + 1, wlen)
        cnt = jnp.minimum(pos + 1, wlen).astype(F32)
        dds.append((win / cnt - cu).astype(BF16))
        pnew_ref[n] = ext_ref[n, tl + 1:tl + halo, :]
        ext_ref[n, 0:halo, :] = ext_ref[n, tl:tl + halo, :]

    seg = tl // n_split if nb == 1 else tl
    groups = [[(0, g * seg)] for g in range(n_split)] if nb == 1 else \
             [[(n, 0) for n in range(g * nb // n_split, (g + 1) * nb // n_split)] for g in range(n_split)]

    def rows(members, ref, w0=0, w1=None):
        return jnp.concatenate([ref[n, r0:r0 + seg, w0:w1] for n, r0 in members], axis=0)

    mixes = []
    for members in groups:
        dd = jnp.concatenate([dds[n][r0:r0 + seg] for n, r0 in members], axis=0)
        y = _dot(dd, wpool_ref[...]) * pscale_ref[...]
        c_out = (y * _silu(rows(members, pc_ref, C_WIDTH, 2 * C_WIDTH))).astype(BF16)
        mixes.append(_dot(rows(members, a_ref), woa_ref[...]) + _dot(rows(members, b_ref), wob_ref[...])
                     + _dot(c_out, woc_ref[...]))
    x1s = [rows(members, x_ref) + _rms(mix, npost_ref[...]) for members, mix in zip(groups, mixes)]

    qs = [_dot(_rms(x1, nxpre_ref[...]).astype(BF16), wxq_ref[...]).astype(BF16) for x1 in x1s]
    pairs = [(g, m, n, h) for g, members in enumerate(groups) for m, (n, _) in enumerate(members)
             for h in range(X_HEADS)]
    scores = [lax.dot_general(qs[g][m * seg:(m + 1) * seg, h * dh:(h + 1) * dh],
                              mkb_ref[n, :, h * dh:(h + 1) * dh], NT_DIMS,
                              preferred_element_type=F32) * (dh ** -0.5) for g, m, n, h in pairs]
    probs = []
    for s in scores:
        ex = jnp.exp(s - jnp.max(s, axis=-1, keepdims=True))
        probs.append((ex / jnp.sum(ex, axis=-1, keepdims=True)).astype(BF16))
    ctx = [_dot(p, mvb_ref[n, :, h * dh:(h + 1) * dh]).astype(BF16) for p, (g, m, n, h) in zip(probs, pairs)]
    xos = []
    for g, members in enumerate(groups):
        seqs = [jnp.concatenate([c for c, (g2, m2, _, _) in zip(ctx, pairs) if g2 == g and m2 == m], axis=1)
                for m in range(len(members))]
        xos.append(_dot(jnp.concatenate(seqs, axis=0), wxo_ref[...]))
    for members, x1, xo in zip(groups, x1s, xos):
        res = x1 + _rms(xo, nxpost_ref[...])
        for m, (n, r0) in enumerate(members):
            xo_ref[n, r0:r0 + seg, :] = res[m * seg:(m + 1) * seg]


def _post(x, a_out, b_out, pc, pool_prev, mk, mv, wts, past, layer):
    b, l, d = x.shape
    n_mem, dh = mk.shape[2], mk.shape[4]
    tl = POST_ROWS if l % POST_ROWS == 0 else CHUNK
    nb = max(1, min(b, POST_ROWS // tl, POST_SEQS))
    while b % nb:
        nb -= 1
    n_split = 2 if (nb * tl) % (2 * LANES) == 0 and (nb == 1 or nb % 2 == 0) else 1
    halo = POOL_STATE + 1
    pprev = jnp.concatenate([jnp.zeros((b, 1, C_WIDTH), F32), pool_prev], axis=1)
    tile = lambda w: pl.BlockSpec((nb, tl, w), lambda i, j: (i, j, 0))
    per_seq = lambda r, w: pl.BlockSpec((nb, r, w), lambda i, j: (i, 0, 0))
    full = lambda a: pl.BlockSpec(a.shape, lambda i, j: (0,) * a.ndim)
    hbm = pl.BlockSpec(memory_space=pl.ANY)
    return pl.pallas_call(
        functools.partial(_post_kernel, nb=nb, tl=tl, n_split=n_split, past=past, layer=layer),
        grid=(b // nb, l // tl),
        in_specs=[tile(d), tile(A_WIDTH), tile(B_WIDTH), tile(2 * C_WIDTH), per_seq(halo, C_WIDTH),
                  hbm, hbm] + [full(a) for a in wts],
        out_specs=[tile(d), per_seq(POOL_STATE, C_WIDTH)],
        out_shape=[jax.ShapeDtypeStruct((b, l, d), F32),
                   jax.ShapeDtypeStruct((b, POOL_STATE, C_WIDTH), F32)],
        scratch_shapes=[pltpu.VMEM((nb, halo + tl, C_WIDTH), F32),
                        pltpu.VMEM((2, 2, nb, X_HEADS, n_mem, dh), F32),
                        pltpu.VMEM((nb, n_mem, d), BF16),
                        pltpu.VMEM((nb, n_mem, d), BF16),
                        pltpu.SemaphoreType.DMA((2, 2))],
        compiler_params=_params(2),
        name="mix_out",
    )(x, a_out, b_out, pc, pprev, mk, mv, *wts)


def _layer(x, mk, mv, past_t, s0, s0_layer, pool_prev, lw, layer, depth, chain):
    kv_t, s_all = chain
    b, l, d = x.shape
    n = b * l
    (pa, bq, bk, bv, bg, pc), kv_t = _proj(x.reshape(n, d), lw["n_pre"], lw["w_in"], l, layer, depth, kv_t)
    a_out, s_all = _hgrn(pa.reshape(b, l, -1), lw["lb"], lw["onorm_g"], s0, s0_layer, layer, depth, s_all)
    seq = lambda t: t.reshape(b, l, -1)
    b_out = _sb(seq(bq), seq(bk), seq(bv), seq(bg), past_t, layer)
    past = 0 if past_t is None else past_t[0].shape[3]
    x_new, pool_new = _post(x, a_out, b_out, seq(pc), pool_prev, mk, mv, lw["post"], past, layer)
    heads = lambda t: t.reshape(b, l, B_HEADS, B_HEAD_DIM)
    return x_new, heads(bk), heads(bv), (kv_t, s_all), pool_new


def kernel(x_prompt, x_sample, mem_prompt, cache_sb_k, cache_sb_v, state_hgrn, state_pool, cache_mem_k, cache_mem_v, norm_mix_pre, norm_mix_post, w_in, hgrn_lb_logits, hgrn_onorm_g, w_pool, pool_scale, w_out, norm_x_pre, norm_x_post, norm_mem, w_xq, w_xk, w_xv, w_xo):
    depth, d = norm_mix_pre.shape
    bp = x_prompt.shape[0]
    bs, past = cache_sb_k.shape[1], cache_sb_k.shape[2]
    n_mem = mem_prompt.shape[1]

    lbs = jax.nn.softmax(hgrn_lb_logits.astype(F32), axis=0)
    lower_bounds = jnp.maximum(jnp.cumsum(lbs, axis=0) - lbs[0], 0.0)

    mk_all, mv_all = _memkv(mem_prompt, norm_mem, w_xk.astype(BF16), w_xv.astype(BF16))

    def seq_minor(c):
        return jnp.transpose(c, (0, 1, 3, 4, 2)).reshape(depth, bs, B_WIDTH, past)

    def seq_major(t):
        dp, b, _, l = t.shape
        return jnp.transpose(t.reshape(dp, b, B_HEADS, B_HEAD_DIM, l), (0, 1, 4, 2, 3))

    past_t = (seq_minor(cache_sb_k), seq_minor(cache_sb_v))

    row = lambda v: v.reshape(1, -1)
    s_zero = jnp.zeros((1, bp, A_HEADS, A_HEAD_DIM, A_HEAD_DIM), F32)
    pool_zero = jnp.zeros((bp, POOL_STATE, C_WIDTH), F32)

    xp, xs = x_prompt, x_sample
    p_kn, p_vn, p_pools, s_kn, s_vn, s_pools = [], [], [], [], [], []
    chain_p = chain_s = (None, None)
    for l in range(depth):
        wp = jnp.zeros((C_WIDTH, C_WIDTH), F32)
        for gi in range(C_GROUPS):
            gs = slice(gi * C_GROUP_DIM, (gi + 1) * C_GROUP_DIM)
            wp = wp.at[gs, gs].set(w_pool[l, gi])
        wo = w_out[l].astype(BF16)
        lw = {
            "n_pre": norm_mix_pre[l], "w_in": w_in[l].astype(BF16), "lb": lower_bounds[l],
            "onorm_g": hgrn_onorm_g[l],
            "post": (wp.astype(BF16), row(pool_scale[l]), wo[0:A_WIDTH], wo[A_WIDTH:A_WIDTH + B_WIDTH],
                     wo[A_WIDTH + B_WIDTH:], row(norm_mix_post[l]), row(norm_x_pre[l]),
                     w_xq[l].astype(BF16), w_xo[l].astype(BF16), row(norm_x_post[l])),
        }
        xp, kb, vb, chain_p, pn = _layer(xp, mk_all, mv_all, None, s_zero, 0, pool_zero, lw, l, depth, chain_p)
        p_kn.append(kb); p_vn.append(vb); p_pools.append(pn)
        xs, kb, vb, chain_s, pn = _layer(xs, cache_mem_k, cache_mem_v, past_t, state_hgrn, l, state_pool[l],
                                         lw, l, depth, chain_s)
        s_kn.append(kb); s_vn.append(vb); s_pools.append(pn)

    def new_kv(kvt, nat_k, nat_v):
        if kvt is not None:
            return seq_major(kvt[0]), seq_major(kvt[1])
        return jnp.stack(nat_k), jnp.stack(nat_v)

    p_k, p_v = new_kv(chain_p[0], p_kn, p_vn)
    s_k, s_v = new_kv(chain_s[0], s_kn, s_vn)
    return (xp, xs, p_k, p_v, chain_p[1], jnp.stack(p_pools), mk_all, mv_all, s_k, s_v, chain_s[1],
            jnp.stack(s_pools))
```

```python
import functools

import jax
import jax.numpy as jnp
from jax import lax
from jax.experimental import pallas as pl
from jax.experimental.pallas import tpu as pltpu

F32 = jnp.float32
BF16 = jnp.bfloat16
EPS = 1e-6

A_HEADS, A_HEAD_DIM = 4, 128
A_WIDTH = A_HEADS * A_HEAD_DIM
B_HEADS, B_HEAD_DIM = 4, 64
B_WIDTH = B_HEADS * B_HEAD_DIM
C_GROUPS, C_GROUP_DIM = 4, 64
C_WIDTH = C_GROUPS * C_GROUP_DIM
POOL_WINDOWS = (2, 4, 8, 16)
POOL_STATE = 15
X_HEADS = 4
CHUNK = 64
SUB = 16
SAFE_SPAN = 40.0
KEY_BLOCK = 128
SB_SEQS = 4
HGRN_ROWS = 512
POST_ROWS = 512
POST_SEQS = 4
SUBLANES = 8
LANES = 128
VMEM_LIMIT = 56 * 1024 * 1024

NT_DIMS = (((1,), (1,)), ((), ()))
TN_DIMS = (((0,), (0,)), ((), ()))


def _rms(x, g):
    ms = jnp.mean(x * x, axis=-1, keepdims=True)
    return x * lax.rsqrt(ms + EPS) * g


def _silu(x):
    return x * (1.0 / (1.0 + jnp.exp(-x)))


def _dot(a, b):
    return jnp.dot(a, b, preferred_element_type=F32)


def _params(n_grid, flags=None):
    return pltpu.CompilerParams(dimension_semantics=("arbitrary",) * n_grid,
                                vmem_limit_bytes=VMEM_LIMIT, flags=flags)


def _memkv_kernel(mem_ref, g_ref, wk_ref, wv_ref, k_ref, v_ref):
    nb, n_mem, d = mem_ref.shape
    dh = d // X_HEADS
    for b in range(nb):
        m = _rms(mem_ref[b], g_ref[0]).astype(BF16)
        k = _dot(m, wk_ref[0])
        v = _dot(m, wv_ref[0])
        for h in range(X_HEADS):
            k_ref[0, b, :, h, :] = k[:, h * dh:(h + 1) * dh]
            v_ref[0, b, :, h, :] = v[:, h * dh:(h + 1) * dh]


def _memkv(mem, norm_mem, wk, wv):
    depth, d = norm_mem.shape
    nb, n_mem, _ = mem.shape
    dh = d // X_HEADS
    out = jax.ShapeDtypeStruct((depth, nb, n_mem, X_HEADS, dh), F32)
    return pl.pallas_call(
        _memkv_kernel,
        grid=(depth,),
        in_specs=[pl.BlockSpec((nb, n_mem, d), lambda l: (0, 0, 0)),
                  pl.BlockSpec((1, 1, d), lambda l: (l, 0, 0)),
                  pl.BlockSpec((1, d, d), lambda l: (l, 0, 0)),
                  pl.BlockSpec((1, d, d), lambda l: (l, 0, 0))],
        out_specs=[pl.BlockSpec((1, nb, n_mem, X_HEADS, dh), lambda l: (l, 0, 0, 0, 0))] * 2,
        out_shape=[out, out],
        compiler_params=_params(1),
        name="mem_kv",
    )(mem, norm_mem.reshape(depth, 1, d), wk, wv)


def _proj_kernel(*refs, transposed, aliased):
    x_ref, g_ref, w_ref = refs[:3]
    outs = refs[3 + 2 * aliased:]
    pa_ref, q_ref, k_ref, v_ref, bg_ref, pc_ref = outs[:6]
    a_end = 4 * A_WIDTH
    b_end = a_end + 4 * B_WIDTH
    tm = x_ref.shape[0]
    n_groups = 2 if tm % (2 * LANES) == 0 else 1
    groups = [slice(g * tm // n_groups, (g + 1) * tm // n_groups) for g in range(n_groups)]
    hs = [_rms(x_ref[rows, :], g_ref[...]).astype(BF16) for rows in groups]
    for rows, h in zip(groups, hs):
        pa_ref[rows, :] = _dot(h, w_ref[:, 0:a_end])
        pb = _dot(h, w_ref[:, a_end:b_end])
        q_ref[rows, :] = pb[:, 0:B_WIDTH]
        k_ref[rows, :] = pb[:, B_WIDTH:2 * B_WIDTH]
        v_ref[rows, :] = pb[:, 2 * B_WIDTH:3 * B_WIDTH]
        bg_ref[rows, :] = pb[:, 3 * B_WIDTH:4 * B_WIDTH]
        pc_ref[rows, :] = _dot(h, w_ref[:, b_end:])
        if transposed:
            kt_ref, vt_ref = outs[6:]
            kt_ref[0, 0, :, rows] = pb[:, B_WIDTH:2 * B_WIDTH].T
            vt_ref[0, 0, :, rows] = pb[:, 2 * B_WIDTH:3 * B_WIDTH].T


def _proj(x2d, g, w_in, seq_len, layer, depth, kv_t):
    n, d = x2d.shape
    d_in = w_in.shape[1]
    tm = 512 if n % 512 == 0 else CHUNK
    transposed = seq_len % tm == 0 and tm % LANES == 0
    aliased = transposed and kv_t is not None
    row = lambda width: pl.BlockSpec((tm, width), lambda i: (i, 0))
    shapes = [4 * A_WIDTH, B_WIDTH, B_WIDTH, B_WIDTH, B_WIDTH, 2 * C_WIDTH]
    in_specs = [row(d), pl.BlockSpec((1, d), lambda i: (0, 0)), pl.BlockSpec((d, d_in), lambda i: (0, 0))]
    args = [x2d, g.reshape(1, d), w_in]
    out_specs = [row(w) for w in shapes]
    out_shape = [jax.ShapeDtypeStruct((n, w), F32) for w in shapes]
    aliases = {}
    if transposed:
        tiles = seq_len // tm
        t_spec = pl.BlockSpec((1, 1, B_WIDTH, tm), lambda i: (layer, i // tiles, 0, i % tiles))
        out_specs += [t_spec, t_spec]
        out_shape += [jax.ShapeDtypeStruct((depth, n // seq_len, B_WIDTH, seq_len), F32)] * 2
        if aliased:
            in_specs += [pl.BlockSpec(memory_space=pl.ANY)] * 2
            args += list(kv_t)
            aliases = {3: 6, 4: 7}
    res = pl.pallas_call(
        functools.partial(_proj_kernel, transposed=transposed, aliased=aliased),
        grid=(n // tm,),
        in_specs=in_specs,
        out_specs=out_specs,
        out_shape=out_shape,
        input_output_aliases=aliases,
        compiler_params=_params(1),
        name="in_proj",
    )(*args)
    return res[:6], (tuple(res[6:]) if transposed else None)


def _hgrn_kernel(*refs, nb, nch, aliased):
    pa_ref, loglb_ref, log1mlb_ref, omlb_ref, og_ref, s0_ref = refs[:6]
    a_ref, sout_ref, st_ref, g_ref, q_ref, k_ref = refs[6 + aliased:]
    _hgrn_body(pa_ref, loglb_ref, log1mlb_ref, omlb_ref, og_ref, s0_ref, a_ref, sout_ref,
               st_ref, g_ref, q_ref, k_ref, nb=nb, nch=nch)


def _hgrn_body(pa_ref, loglb_ref, log1mlb_ref, omlb_ref, og_ref, s0_ref, a_ref, sout_ref,
               st_ref, g_ref, q_ref, k_ref, *, nb, nch):
    j = pl.program_id(1)

    @pl.when(j == 0)
    def _():
        for n in range(nb):
            for h in range(A_HEADS):
                st_ref[n, h] = s0_ref[0, n, h].T

    rowc = lax.broadcasted_iota(jnp.int32, (CHUNK, CHUNK), 0)
    colc = lax.broadcasted_iota(jnp.int32, (CHUNK, CHUNK), 1)
    tri = (rowc >= colc).astype(F32)
    col_sub = colc // SUB
    row_k = lax.broadcasted_iota(jnp.int32, (CHUNK, A_HEAD_DIM), 0)
    lane_d = lax.broadcasted_iota(jnp.int32, (SUB, CHUNK), 1)
    n_sub = CHUNK // SUB
    head = [slice(h * A_HEAD_DIM, (h + 1) * A_HEAD_DIM) for h in range(A_HEADS)]

    def sub_starts(g_all):
        return [jnp.zeros((1, A_WIDTH), F32)] + [g_all[SUB * s - 1:SUB * s, :] for s in range(1, n_sub)]

    span = jnp.zeros((1, A_WIDTH), F32)
    for n in range(nb):
        for c in range(nch):
            rs = slice(c * CHUNK, (c + 1) * CHUNK)
            aq = pa_ref[n, rs, 0:A_WIDTH]
            af = pa_ref[n, rs, A_WIDTH:2 * A_WIDTH]
            e = jnp.exp(-jnp.abs(af))
            ope = 1.0 + e
            logsig = jnp.minimum(af, 0.0) - jnp.log(ope)
            r = 1.0 / ope
            sig_neg = jnp.where(af >= 0, e * r, r)
            la = loglb_ref[...]
            lb_ = log1mlb_ref[...] + logsig
            log_f = jnp.maximum(la, lb_) + jnp.log(1.0 + jnp.exp(-jnp.abs(la - lb_)))
            g_all = jnp.dot(tri, log_f, precision=lax.Precision.HIGHEST, preferred_element_type=F32)
            g_ref[n, rs, :] = g_all
            k_ref[n, rs, :] = omlb_ref[...] * sig_neg
            q_ref[n, rs, :] = _silu(aq) * (A_HEAD_DIM ** -0.5)
            starts = sub_starts(g_all)
            for s in range(n_sub):
                span = jnp.maximum(span, starts[s] - g_all[SUB * s + SUB - 1:SUB * s + SUB, :])
    small_decay = jnp.max(span) < SAFE_SPAN

    def att_factored(g_all, q_all, k_all, qg_all):
        starts = sub_starts(g_all)
        operands = []
        for sl in head:
            gh, qh, kh = g_all[:, sl], q_all[:, sl], k_all[:, sl]
            parts, krows = [qg_all[:, sl]], []
            for s in range(n_sub):
                rs = slice(SUB * s, SUB * (s + 1))
                st = starts[s][:, sl]
                krows.append(kh[rs] * jnp.exp(st - gh[rs]))
                if s:
                    parts.append(qh[SUB * s:, :] * jnp.exp(gh[SUB * s:, :] - st))
            operands.append((jnp.concatenate(parts, axis=0).astype(BF16),
                             jnp.concatenate(krows, axis=0).astype(BF16)))
        rrs = [lax.dot_general(qr, kr, NT_DIMS, preferred_element_type=F32) for qr, kr in operands]
        atts = []
        for rr in rrs:
            att = rr[0:CHUNK, :]
            r0 = CHUNK
            for s in range(1, n_sub):
                rows = CHUNK - SUB * s
                block = jnp.concatenate([jnp.zeros((SUB * s, CHUNK), F32), rr[r0:r0 + rows, :]], axis=0)
                att = jnp.where(col_sub == s, block, att)
                r0 += rows
            atts.append(jnp.where(rowc >= colc, att, 0.0))
        return atts

    def att_exact(g_all, q_all, k_all, qg_all):
        def one_head(sl):
            gh, qh, kh = g_all[:, sl], q_all[:, sl], k_all[:, sl]
            parts, refrows = [], []
            for jsub in range(n_sub):
                ref = gh[SUB * jsub + SUB - 1:SUB * jsub + SUB, :]
                refrows.append(jnp.broadcast_to(ref, (SUB, A_HEAD_DIM)))
                if jsub < n_sub - 1:
                    later = row_k >= SUB * (jsub + 1)
                    parts.append(jnp.where(later, qh * jnp.exp(jnp.minimum(gh - ref, 0.0)), 0.0))
            q_rel = jnp.concatenate(parts, axis=0).astype(BF16)
            k_rel = (kh * jnp.exp(jnp.minimum(jnp.concatenate(refrows, axis=0) - gh, 0.0))).astype(BF16)
            rr = lax.dot_general(q_rel, k_rel, NT_DIMS, preferred_element_type=F32)
            att = jnp.zeros((CHUNK, CHUNK), F32)
            for jsub in range(n_sub - 1):
                att = jnp.where(col_sub == jsub, rr[jsub * CHUNK:(jsub + 1) * CHUNK, :], att)

            dparts = []
            for csub in range(n_sub):
                rs = slice(SUB * csub, SUB * (csub + 1))
                gc, qc, kc = gh[rs], qh[rs], kh[rs]
                accd = jnp.zeros((SUB, CHUNK), F32)
                for s in range(SUB):
                    dec = jnp.exp(jnp.minimum(gc - gc[s:s + 1, :], 0.0))
                    col = jnp.sum(dec * qc * kc[s:s + 1, :], axis=-1, keepdims=True)
                    accd = jnp.where(lane_d == SUB * csub + s, col, accd)
                dparts.append(accd)
            return att + jnp.where(rowc >= colc, jnp.concatenate(dparts, axis=0), 0.0)

        return [one_head(sl) for sl in head]

    def chunk(n, r0, sts, att_fn):
        g_all = g_ref[n, pl.ds(r0, CHUNK), :]
        q_all = q_ref[n, pl.ds(r0, CHUNK), :]
        k_all = k_ref[n, pl.ds(r0, CHUNK), :]
        v_all = pa_ref[n, pl.ds(r0, CHUNK), 2 * A_WIDTH:3 * A_WIDTH].astype(BF16)
        ag = pa_ref[n, pl.ds(r0, CHUNK), 3 * A_WIDTH:4 * A_WIDTH]
        qg_all = q_all * jnp.exp(g_all)
        gl_all = g_all[CHUNK - 1:CHUNK, :]
        kd_all = (k_all * jnp.exp(gl_all - g_all)).astype(BF16)
        qg_bf = qg_all.astype(BF16)
        atts = att_fn(g_all, q_all, k_all, qg_all)

        o_state = [lax.dot_general(qg_bf[:, sl], st.astype(BF16), NT_DIMS, preferred_element_type=F32)
                   for sl, st in zip(head, sts)]
        o_pairs = [_dot(att.astype(BF16), v_all[:, sl]) for sl, att in zip(head, atts)]
        updates = [lax.dot_general(v_all[:, sl], kd_all[:, sl], TN_DIMS, preferred_element_type=F32)
                   for sl in head]
        new_sts = [st * jnp.exp(gl_all[:, sl]) + up for sl, st, up in zip(head, sts, updates)]
        o = jnp.concatenate([a + b for a, b in zip(o_state, o_pairs)], axis=1)
        sq = o * o
        inv = jnp.concatenate(
            [jnp.broadcast_to(lax.rsqrt(jnp.mean(sq[:, sl], axis=-1, keepdims=True) + EPS), (CHUNK, A_HEAD_DIM))
             for sl in head], axis=1)
        a_ref[n, pl.ds(r0, CHUNK), :] = (o * inv * og_ref[...] * _silu(ag)).astype(BF16)
        return new_sts

    @pl.when(small_decay)
    def _():
        for n in range(nb):
            sts = [st_ref[n, h] for h in range(A_HEADS)]
            for c in range(nch):
                sts = chunk(n, c * CHUNK, sts, att_factored)
            for h in range(A_HEADS):
                st_ref[n, h] = sts[h]

    @pl.when(jnp.logical_not(small_decay))
    def _():
        for n in range(nb):
            def body(c, carry):
                sts = chunk(n, pl.multiple_of(c * CHUNK, CHUNK), [st_ref[n, h] for h in range(A_HEADS)], att_exact)
                for h in range(A_HEADS):
                    st_ref[n, h] = sts[h]
                return carry
            lax.fori_loop(0, nch, body, 0)

    @pl.when(j == pl.num_programs(1) - 1)
    def _():
        for n in range(nb):
            for h in range(A_HEADS):
                sout_ref[0, n, h] = st_ref[n, h].T


def _hgrn(pa, lb, onorm_g, s0, s0_layer, layer, depth, s_all):
    b, l, wa = pa.shape
    tl = HGRN_ROWS if l % HGRN_ROWS == 0 else CHUNK
    nb = max(1, min(b, HGRN_ROWS // tl))
    while b % nb:
        nb -= 1
    vec = lambda: pl.BlockSpec((1, A_WIDTH), lambda i, j: (0, 0))
    st_block = (1, nb, A_HEADS, A_HEAD_DIM, A_HEAD_DIM)
    lb = lb.reshape(1, A_WIDTH)
    aliased = s_all is not None
    in_specs = [pl.BlockSpec((nb, tl, wa), lambda i, j: (i, j, 0)), vec(), vec(), vec(), vec(),
                pl.BlockSpec(st_block, lambda i, j: (s0_layer, i, 0, 0, 0))]
    args = [pa, jnp.log(lb), jnp.log1p(-lb), 1.0 - lb, onorm_g.reshape(1, A_WIDTH), s0]
    if aliased:
        in_specs.append(pl.BlockSpec(memory_space=pl.ANY))
        args.append(s_all)
    return pl.pallas_call(
        functools.partial(_hgrn_kernel, nb=nb, nch=tl // CHUNK, aliased=aliased),
        grid=(b // nb, l // tl),
        in_specs=in_specs,
        out_specs=[pl.BlockSpec((nb, tl, A_WIDTH), lambda i, j: (i, j, 0)),
                   pl.BlockSpec(st_block, lambda i, j: (layer, i, 0, 0, 0))],
        out_shape=[jax.ShapeDtypeStruct((b, l, A_WIDTH), BF16),
                   jax.ShapeDtypeStruct((depth, b, A_HEADS, A_HEAD_DIM, A_HEAD_DIM), F32)],
        scratch_shapes=[pltpu.VMEM((nb, A_HEADS, A_HEAD_DIM, A_HEAD_DIM), F32)]
                       + [pltpu.VMEM((nb, tl, A_WIDTH), F32)] * 3,
        input_output_aliases={6: 1} if aliased else {},
        compiler_params=_params(2),
        name="hgrn2",
    )(*args)


def _sb_kernel(*refs, ns, lq, n_past):
    if n_past:
        q_ref, k_ref, v_ref, g_ref, pk_ref, pv_ref, o_ref = refs[:7]
        scratch = refs[7:]
    else:
        q_ref, k_ref, v_ref, g_ref, o_ref = refs[:5]
        pk_ref = pv_ref = None
        scratch = refs[5:]
    kc_ref, vtc_ref, stk_ref, stv_ref, stq_ref, qtm_ref, carry_ref, acc_ref = scratch
    j = pl.program_id(1)
    kb_rows = KEY_BLOCK
    run_len = kb_rows // SUBLANES
    width = B_WIDTH
    n_halves = width // LANES
    streams = range(ns)

    def permuted(ref, s):
        return jnp.concatenate(
            [jnp.concatenate([ref[s, c, pl.ds(jj, SUBLANES, stride=run_len), :] for jj in range(run_len)], axis=0)
             for c in range(n_halves)], axis=1)

    def stage(dst_ref, s, rows):
        for c in range(n_halves):
            if rows.shape[0] < kb_rows:
                dst_ref[s, c, rows.shape[0]:kb_rows, :] = jnp.zeros((kb_rows - rows.shape[0], LANES), F32)
            dst_ref[s, c, 0:rows.shape[0], :] = rows[:, c * LANES:(c + 1) * LANES]

    row_head = lax.broadcasted_iota(jnp.int32, (width, kb_rows), 0) // B_HEAD_DIM
    for s in streams:
        stage(stk_ref, s, k_ref[s])
        stage(stv_ref, s, v_ref[s])
        stage(stq_ref, s, q_ref[s] * (0.5 * B_HEAD_DIM ** -0.5))
    for s in streams:
        kc_ref[s, j] = permuted(stk_ref, s).astype(BF16)
        vtc_ref[s, j] = permuted(stv_ref, s).T.astype(BF16)
        qt = jnp.concatenate([stq_ref[s, c] for c in range(n_halves)], axis=1).T
        qtm_ref[s] = jnp.concatenate([jnp.where(row_head == h, qt, 0.0) for h in range(B_HEADS)],
                                     axis=1).astype(BF16)

    row = lax.broadcasted_iota(jnp.int32, (kb_rows, kb_rows), 0)
    lane = lax.broadcasted_iota(jnp.int32, (kb_rows, kb_rows), 1)
    causal = (row % SUBLANES) * run_len + row // SUBLANES < lane
    sub = lax.broadcasted_iota(jnp.int32, (SUBLANES, kb_rows), 0)

    carry_ref[...] = jnp.ones(carry_ref.shape, F32)
    acc_ref[...] = jnp.zeros(acc_ref.shape, F32)

    def process(blocks, masked):
        zts = [_dot(kb, qtm_ref[s]) for s, (kb, _) in enumerate(blocks)]
        ws = []
        for s, zt in enumerate(zts):
            for h in range(B_HEADS):
                ls = slice(h * kb_rows, (h + 1) * kb_rows)
                half_t = 0.5 * jnp.tanh(zt[:, ls])
                beta = 0.5 + half_t
                omb = 0.5 - half_t
                if masked:
                    beta = jnp.where(causal, beta, 0.0)
                    omb = jnp.where(causal, omb, 1.0)
                run = jnp.ones((SUBLANES, kb_rows), F32)
                excl = [None] * run_len
                for jj in reversed(range(run_len)):
                    excl[jj] = run
                    run = run * omb[jj * SUBLANES:(jj + 1) * SUBLANES]
                inc = run
                for d in (1, 2, 4):
                    inc = jnp.where(sub + d < SUBLANES, inc * pltpu.roll(inc, SUBLANES - d, 0), inc)
                carry = carry_ref[s, :, ls]
                off = jnp.where(sub < SUBLANES - 1, pltpu.roll(inc, SUBLANES - 1, 0), 1.0) * carry
                ws.append(jnp.concatenate([beta[jj * SUBLANES:(jj + 1) * SUBLANES] * (excl[jj] * off)
                                           for jj in range(run_len)], axis=0).astype(BF16))
                carry_ref[s, :, ls] = carry * jnp.broadcast_to(inc[0:1, :], (SUBLANES, kb_rows))
        for s, (_, vtb) in enumerate(blocks):
            for h in range(B_HEADS):
                hs = slice(h * B_HEAD_DIM, (h + 1) * B_HEAD_DIM)
                acc_ref[s, hs, :] += _dot(vtb[hs, :], ws[s * B_HEADS + h])

    def alive():
        return (jnp.max(carry_ref[...]) > 0.0).astype(jnp.int32)

    process([(kc_ref[s, j], vtc_ref[s, j]) for s in streams], True)

    def cond(state):
        kb, live = state
        return jnp.logical_and(kb >= 0, live > 0)

    def body(state):
        kb, _ = state
        process([(kc_ref[s, kb], vtc_ref[s, kb]) for s in streams], False)
        return kb - 1, alive()

    lax.while_loop(cond, body, (j - 1, alive()))

    for kb in reversed(range(n_past)):
        @pl.when(alive() > 0)
        def _():
            cols = slice(kb * kb_rows, (kb + 1) * kb_rows)
            for s in streams:
                kn = pk_ref[0, s, :, cols].T
                vn = pv_ref[0, s, :, cols].T
                for c in range(n_halves):
                    stk_ref[s, c] = kn[:, c * LANES:(c + 1) * LANES]
                    stv_ref[s, c] = vn[:, c * LANES:(c + 1) * LANES]
            process([(permuted(stk_ref, s).astype(BF16), permuted(stv_ref, s).T.astype(BF16)) for s in streams],
                    False)

    for s in streams:
        o = acc_ref[s].T[0:lq, :]
        o_ref[s] = (o * _silu(g_ref[s])).astype(BF16)


def _sb(q, k, v, g, past_t, layer):
    b, l, w = q.shape
    lq = min(KEY_BLOCK, l)
    nblk = l // lq
    ns = min(b, SB_SEQS)
    while b % ns:
        ns -= 1
    p = 0 if past_t is None else past_t[0].shape[3]
    blk = pl.BlockSpec((ns, lq, w), lambda i, j: (i, j, 0))
    in_specs = [blk] * 4
    args = [q, k, v, g]
    if p:
        in_specs += [pl.BlockSpec((1, ns, w, p), lambda i, j: (layer, i, 0, 0))] * 2
        args += list(past_t)
    staging = pltpu.VMEM((ns, w // LANES, KEY_BLOCK, LANES), F32)
    return pl.pallas_call(
        functools.partial(_sb_kernel, ns=ns, lq=lq, n_past=p // KEY_BLOCK),
        grid=(b // ns, nblk),
        in_specs=in_specs,
        out_specs=blk,
        out_shape=jax.ShapeDtypeStruct((b, l, w), BF16),
        scratch_shapes=[pltpu.VMEM((ns, nblk, KEY_BLOCK, w), BF16),
                        pltpu.VMEM((ns, nblk, w, KEY_BLOCK), BF16),
                        staging, staging, staging,
                        pltpu.VMEM((ns, w, B_HEADS * KEY_BLOCK), BF16),
                        pltpu.VMEM((ns, SUBLANES, B_HEADS * KEY_BLOCK), F32),
                        pltpu.VMEM((ns, w, KEY_BLOCK), F32)],
        compiler_params=_params(2),
        name="sb_attn",
    )(*args)


def _post_kernel(x_ref, a_ref, b_ref, pc_ref, pprev_ref, mk_hbm, mv_hbm, wpool_ref, pscale_ref,
                 woa_ref, wob_ref, woc_ref, npost_ref, nxpre_ref, wxq_ref, wxo_ref, nxpost_ref,
                 xo_ref, pnew_ref, ext_ref, sum_ref, kv_ref, mkb_ref, mvb_ref, sem_ref,
                 *, nb, tl, n_split, past, layer):
    i = pl.program_id(0)
    j = pl.program_id(1)
    n_groups = pl.num_programs(0)
    halo = POOL_STATE + 1
    lead = SUBLANES
    dh = kv_ref.shape[-1]

    def kv_copies(group, slot):
        return [pltpu.make_async_copy(src.at[layer, group * nb + n, :, h, :], kv_ref.at[slot, t, n, h],
                                      sem_ref.at[slot, t])
                for t, src in enumerate((mk_hbm, mv_hbm)) for n in range(nb) for h in range(X_HEADS)]

    @pl.when(j == 0)
    def _():
        slot = i % 2

        @pl.when(i == 0)
        def _():
            for cp in kv_copies(0, 0):
                cp.start()

        @pl.when(i + 1 < n_groups)
        def _():
            for cp in kv_copies(i + 1, 1 - slot):
                cp.start()

        for cp in kv_copies(i, slot):
            cp.wait()
        for n in range(nb):
            ext_ref[n, 0:lead, :] = jnp.zeros((lead, C_WIDTH), F32)
            ext_ref[n, lead:lead + halo, :] = pprev_ref[n]
            for gi in range(len(POOL_WINDOWS) - 1):
                sum_ref[gi, n, 0:lead, :] = jnp.zeros((lead, C_WIDTH), F32)
            for h in range(X_HEADS):
                mkb_ref[n, :, h * dh:(h + 1) * dh] = kv_ref[slot, 0, n, h].astype(BF16)
                mvb_ref[n, :, h * dh:(h + 1) * dh] = kv_ref[slot, 1, n, h].astype(BF16)

    grp = lax.broadcasted_iota(jnp.int32, (tl, C_WIDTH), 1) // C_GROUP_DIM
    pos = past + j * tl + lax.broadcasted_iota(jnp.int32, (tl, C_WIDTH), 0)
    dds = []
    rows = halo + tl
    for n in range(nb):
        cu = pc_ref[n, :, 0:C_WIDTH]
        ext_ref[n, lead + halo:lead + rows, :] = cu
        win = jnp.zeros((tl, C_WIDTH), F32)
        wlen = jnp.zeros((tl, C_WIDTH), jnp.int32)
        src = ext_ref.at[n]
        for gi, w in enumerate(POOL_WINDOWS):
            assert w == 2 ** (gi + 1)
            wsum = src[lead:lead + rows, :] + src[lead - w // 2:lead - w // 2 + rows, :]
            if gi + 1 < len(POOL_WINDOWS):
                sum_ref[gi, n, lead:lead + rows, :] = wsum
                src = sum_ref.at[gi, n]
            win = jnp.where(grp == gi, wsum[halo:, :], win)
            wlen = jnp.where(grp == gi, w, wlen)
        cnt = jnp.minimum(pos + 1, wlen).astype(F32)
        dds.append((win / cnt - cu).astype(BF16))
        pnew_ref[n] = ext_ref[n, lead + tl + 1:lead + tl + halo, :]
        ext_ref[n, lead:lead + halo, :] = ext_ref[n, lead + tl:lead + tl + halo, :]

    seg = tl // n_split if nb == 1 else tl
    groups = [[(0, g * seg)] for g in range(n_split)] if nb == 1 else \
             [[(n, 0) for n in range(g * nb // n_split, (g + 1) * nb // n_split)] for g in range(n_split)]

    def rows(members, ref, w0=0, w1=None):
        return jnp.concatenate([ref[n, r0:r0 + seg, w0:w1] for n, r0 in members], axis=0)

    mixes = []
    for members in groups:
        dd = jnp.concatenate([dds[n][r0:r0 + seg] for n, r0 in members], axis=0)
        y = _dot(dd, wpool_ref[...]) * pscale_ref[...]
        c_out = (y * _silu(rows(members, pc_ref, C_WIDTH, 2 * C_WIDTH))).astype(BF16)
        mixes.append(_dot(rows(members, a_ref), woa_ref[...]) + _dot(rows(members, b_ref), wob_ref[...])
                     + _dot(c_out, woc_ref[...]))
    x1s = [rows(members, x_ref) + _rms(mix, npost_ref[...]) for members, mix in zip(groups, mixes)]

    qs = [_dot(_rms(x1, nxpre_ref[...]).astype(BF16), wxq_ref[...]).astype(BF16) for x1 in x1s]
    pairs = [(g, m, n, h) for g, members in enumerate(groups) for m, (n, _) in enumerate(members)
             for h in range(X_HEADS)]
    scores = [lax.dot_general(qs[g][m * seg:(m + 1) * seg, h * dh:(h + 1) * dh],
                              mkb_ref[n, :, h * dh:(h + 1) * dh], NT_DIMS,
                              preferred_element_type=F32) * (dh ** -0.5) for g, m, n, h in pairs]
    probs = []
    for s in scores:
        ex = jnp.exp(s - jnp.max(s, axis=-1, keepdims=True))
        probs.append((ex / jnp.sum(ex, axis=-1, keepdims=True)).astype(BF16))
    ctx = [_dot(p, mvb_ref[n, :, h * dh:(h + 1) * dh]).astype(BF16) for p, (g, m, n, h) in zip(probs, pairs)]
    xos = []
    for g, members in enumerate(groups):
        seqs = [jnp.concatenate([c for c, (g2, m2, _, _) in zip(ctx, pairs) if g2 == g and m2 == m], axis=1)
                for m in range(len(members))]
        xos.append(_dot(jnp.concatenate(seqs, axis=0), wxo_ref[...]))
    for members, x1, xo in zip(groups, x1s, xos):
        res = x1 + _rms(xo, nxpost_ref[...])
        for m, (n, r0) in enumerate(members):
            xo_ref[n, r0:r0 + seg, :] = res[m * seg:(m + 1) * seg]


def _post(x, a_out, b_out, pc, pool_prev, mk, mv, wts, past, layer):
    b, l, d = x.shape
    n_mem, dh = mk.shape[2], mk.shape[4]
    tl = POST_ROWS if l % POST_ROWS == 0 else CHUNK
    nb = max(1, min(b, POST_ROWS // tl, POST_SEQS))
    while b % nb:
        nb -= 1
    n_split = 2 if (nb * tl) % (2 * LANES) == 0 and (nb == 1 or nb % 2 == 0) else 1
    halo = POOL_STATE + 1
    pprev = jnp.concatenate([jnp.zeros((b, 1, C_WIDTH), F32), pool_prev], axis=1)
    tile = lambda w: pl.BlockSpec((nb, tl, w), lambda i, j: (i, j, 0))
    per_seq = lambda r, w: pl.BlockSpec((nb, r, w), lambda i, j: (i, 0, 0))
    full = lambda a: pl.BlockSpec(a.shape, lambda i, j: (0,) * a.ndim)
    hbm = pl.BlockSpec(memory_space=pl.ANY)
    return pl.pallas_call(
        functools.partial(_post_kernel, nb=nb, tl=tl, n_split=n_split, past=past, layer=layer),
        grid=(b // nb, l // tl),
        in_specs=[tile(d), tile(A_WIDTH), tile(B_WIDTH), tile(2 * C_WIDTH), per_seq(halo, C_WIDTH),
                  hbm, hbm] + [full(a) for a in wts],
        out_specs=[tile(d), per_seq(POOL_STATE, C_WIDTH)],
        out_shape=[jax.ShapeDtypeStruct((b, l, d), F32),
                   jax.ShapeDtypeStruct((b, POOL_STATE, C_WIDTH), F32)],
        scratch_shapes=[pltpu.VMEM((nb, SUBLANES + halo + tl, C_WIDTH), F32),
                        pltpu.VMEM((len(POOL_WINDOWS) - 1, nb, SUBLANES + halo + tl, C_WIDTH), F32),
                        pltpu.VMEM((2, 2, nb, X_HEADS, n_mem, dh), F32),
                        pltpu.VMEM((nb, n_mem, d), BF16),
                        pltpu.VMEM((nb, n_mem, d), BF16),
                        pltpu.SemaphoreType.DMA((2, 2))],
        compiler_params=_params(2),
        name="mix_out",
    )(x, a_out, b_out, pc, pprev, mk, mv, *wts)


def _layer(x, mk, mv, past_t, s0, s0_layer, pool_prev, lw, layer, depth, chain):
    kv_t, s_all = chain
    b, l, d = x.shape
    n = b * l
    (pa, bq, bk, bv, bg, pc), kv_t = _proj(x.reshape(n, d), lw["n_pre"], lw["w_in"], l, layer, depth, kv_t)
    a_out, s_all = _hgrn(pa.reshape(b, l, -1), lw["lb"], lw["onorm_g"], s0, s0_layer, layer, depth, s_all)
    seq = lambda t: t.reshape(b, l, -1)
    b_out = _sb(seq(bq), seq(bk), seq(bv), seq(bg), past_t, layer)
    past = 0 if past_t is None else past_t[0].shape[3]
    x_new, pool_new = _post(x, a_out, b_out, seq(pc), pool_prev, mk, mv, lw["post"], past, layer)
    heads = lambda t: t.reshape(b, l, B_HEADS, B_HEAD_DIM)
    return x_new, heads(bk), heads(bv), (kv_t, s_all), pool_new


def kernel(x_prompt, x_sample, mem_prompt, cache_sb_k, cache_sb_v, state_hgrn, state_pool, cache_mem_k, cache_mem_v, norm_mix_pre, norm_mix_post, w_in, hgrn_lb_logits, hgrn_onorm_g, w_pool, pool_scale, w_out, norm_x_pre, norm_x_post, norm_mem, w_xq, w_xk, w_xv, w_xo):
    depth, d = norm_mix_pre.shape
    bp = x_prompt.shape[0]
    bs, past = cache_sb_k.shape[1], cache_sb_k.shape[2]
    n_mem = mem_prompt.shape[1]

    lbs = jax.nn.softmax(hgrn_lb_logits.astype(F32), axis=0)
    lower_bounds = jnp.maximum(jnp.cumsum(lbs, axis=0) - lbs[0], 0.0)

    mk_all, mv_all = _memkv(mem_prompt, norm_mem, w_xk.astype(BF16), w_xv.astype(BF16))

    def seq_minor(c):
        return jnp.transpose(c, (0, 1, 3, 4, 2)).reshape(depth, bs, B_WIDTH, past)

    def seq_major(t):
        dp, b, _, l = t.shape
        return jnp.transpose(t.reshape(dp, b, B_HEADS, B_HEAD_DIM, l), (0, 1, 4, 2, 3))

    past_t = (seq_minor(cache_sb_k), seq_minor(cache_sb_v))

    row = lambda v: v.reshape(1, -1)
    s_zero = jnp.zeros((1, bp, A_HEADS, A_HEAD_DIM, A_HEAD_DIM), F32)
    pool_zero = jnp.zeros((bp, POOL_STATE, C_WIDTH), F32)

    xp, xs = x_prompt, x_sample
    p_kn, p_vn, p_pools, s_kn, s_vn, s_pools = [], [], [], [], [], []
    chain_p = chain_s = (None, None)
    for l in range(depth):
        wp = jnp.zeros((C_WIDTH, C_WIDTH), F32)
        for gi in range(C_GROUPS):
            gs = slice(gi * C_GROUP_DIM, (gi + 1) * C_GROUP_DIM)
            wp = wp.at[gs, gs].set(w_pool[l, gi])
        wo = w_out[l].astype(BF16)
        lw = {
            "n_pre": norm_mix_pre[l], "w_in": w_in[l].astype(BF16), "lb": lower_bounds[l],
            "onorm_g": hgrn_onorm_g[l],
            "post": (wp.astype(BF16), row(pool_scale[l]), wo[0:A_WIDTH], wo[A_WIDTH:A_WIDTH + B_WIDTH],
                     wo[A_WIDTH + B_WIDTH:], row(norm_mix_post[l]), row(norm_x_pre[l]),
                     w_xq[l].astype(BF16), w_xo[l].astype(BF16), row(norm_x_post[l])),
        }
        xp, kb, vb, chain_p, pn = _layer(xp, mk_all, mv_all, None, s_zero, 0, pool_zero, lw, l, depth, chain_p)
        p_kn.append(kb); p_vn.append(vb); p_pools.append(pn)
        xs, kb, vb, chain_s, pn = _layer(xs, cache_mem_k, cache_mem_v, past_t, state_hgrn, l, state_pool[l],
                                         lw, l, depth, chain_s)
        s_kn.append(kb); s_vn.append(vb); s_pools.append(pn)

    def new_kv(kvt, nat_k, nat_v):
        if kvt is not None:
            return seq_major(kvt[0]), seq_major(kvt[1])
        return jnp.stack(nat_k), jnp.stack(nat_v)

    p_k, p_v = new_kv(chain_p[0], p_kn, p_vn)
    s_k, s_v = new_kv(chain_s[0], s_kn, s_vn)
    return (xp, xs, p_k, p_v, chain_p[1], jnp.stack(p_pools), mk_all, mv_all, s_k, s_v, chain_s[1],
            jnp.stack(s_pools))
```

```python
import functools

import jax
import jax.numpy as jnp
from jax import lax
from jax.experimental import pallas as pl
from jax.experimental.pallas import tpu as pltpu

F32 = jnp.float32
BF16 = jnp.bfloat16
EPS = 1e-6

A_HEADS, A_HEAD_DIM = 4, 128
A_WIDTH = A_HEADS * A_HEAD_DIM
B_HEADS, B_HEAD_DIM = 4, 64
B_WIDTH = B_HEADS * B_HEAD_DIM
C_GROUPS, C_GROUP_DIM = 4, 64
C_WIDTH = C_GROUPS * C_GROUP_DIM
POOL_WINDOWS = (2, 4, 8, 16)
POOL_STATE = 15
X_HEADS = 4
CHUNK = 64
SUB = 16
SAFE_SPAN = 40.0
KEY_BLOCK = 128
SB_SEQS = 4
HGRN_ROWS = 512
POST_ROWS = 512
POST_SEQS = 4
SUBLANES = 8
LANES = 128
VMEM_LIMIT = 56 * 1024 * 1024

NT_DIMS = (((1,), (1,)), ((), ()))
TN_DIMS = (((0,), (0,)), ((), ()))


def _rms(x, g):
    ms = jnp.mean(x * x, axis=-1, keepdims=True)
    return x * lax.rsqrt(ms + EPS) * g


def _silu(x):
    return x * (1.0 / (1.0 + jnp.exp(-x)))


def _dot(a, b):
    return jnp.dot(a, b, preferred_element_type=F32)


def _params(n_grid, flags=None):
    return pltpu.CompilerParams(dimension_semantics=("arbitrary",) * n_grid,
                                vmem_limit_bytes=VMEM_LIMIT, flags=flags)


def _memkv_kernel(mem_ref, g_ref, wk_ref, wv_ref, k_ref, v_ref):
    nb, n_mem, d = mem_ref.shape
    dh = d // X_HEADS
    for b in range(nb):
        m = _rms(mem_ref[b], g_ref[0]).astype(BF16)
        k = _dot(m, wk_ref[0])
        v = _dot(m, wv_ref[0])
        for h in range(X_HEADS):
            k_ref[0, b, :, h, :] = k[:, h * dh:(h + 1) * dh]
            v_ref[0, b, :, h, :] = v[:, h * dh:(h + 1) * dh]


def _memkv(mem, norm_mem, wk, wv):
    depth, d = norm_mem.shape
    nb, n_mem, _ = mem.shape
    dh = d // X_HEADS
    out = jax.ShapeDtypeStruct((depth, nb, n_mem, X_HEADS, dh), F32)
    return pl.pallas_call(
        _memkv_kernel,
        grid=(depth,),
        in_specs=[pl.BlockSpec((nb, n_mem, d), lambda l: (0, 0, 0)),
                  pl.BlockSpec((1, 1, d), lambda l: (l, 0, 0)),
                  pl.BlockSpec((1, d, d), lambda l: (l, 0, 0)),
                  pl.BlockSpec((1, d, d), lambda l: (l, 0, 0))],
        out_specs=[pl.BlockSpec((1, nb, n_mem, X_HEADS, dh), lambda l: (l, 0, 0, 0, 0))] * 2,
        out_shape=[out, out],
        compiler_params=_params(1),
        name="mem_kv",
    )(mem, norm_mem.reshape(depth, 1, d), wk, wv)


N_PROJ_IN = 7
PROJ_GROUPS = 4
N_PROJ_OUT = 10


def _proj_kernel(*refs, transposed, aliased):
    x0_ref, xn_ref, g_ref, w_ref, loglb_ref, log1mlb_ref, omlb_ref = refs[:N_PROJ_IN]
    outs = refs[N_PROJ_IN + 2 * aliased:]
    gcum_ref, qa_ref, ka_ref, ai_ref, ag_ref, q_ref, k_ref, v_ref, bg_ref, pc_ref = outs[:N_PROJ_OUT]
    h_refs = outs[-2:]
    i = pl.program_id(0)
    a_end = 4 * A_WIDTH
    b_end = a_end + 4 * B_WIDTH
    tm = xn_ref.shape[0]
    n_groups = PROJ_GROUPS if tm % (PROJ_GROUPS * LANES) == 0 else 1
    groups = [(g * tm // n_groups, tm // n_groups) for g in range(n_groups)]
    sub = lax.broadcasted_iota(jnp.int32, (SUBLANES, A_WIDTH), 0)

    @pl.when(i == 0)
    def _():
        h_refs[0][...] = _rms(x0_ref[...], g_ref[...]).astype(BF16)

    def step(h_ref, h_next_ref):
        h_next_ref[...] = _rms(xn_ref[...], g_ref[...]).astype(BF16)
        hs = [h_ref[r0:r0 + nr, :] for r0, nr in groups]
        qfs = [_dot(h, w_ref[:, 0:2 * A_WIDTH]) for h in hs]
        for (r0, nr), h, qf in zip(groups, hs, qfs):
            rows = slice(r0, r0 + nr)
            aq, af = qf[:, 0:A_WIDTH], qf[:, A_WIDTH:2 * A_WIDTH]
            e = jnp.exp(-jnp.abs(af))
            ope = 1.0 + e
            logsig = jnp.minimum(af, 0.0) - jnp.log(ope)
            r = 1.0 / ope
            sig_neg = jnp.where(af >= 0, e * r, r)
            la = loglb_ref[...]
            lb_ = log1mlb_ref[...] + logsig
            log_f = jnp.maximum(la, lb_) + jnp.log(1.0 + jnp.exp(-jnp.abs(la - lb_)))
            ka_ref[rows, :] = omlb_ref[...] * sig_neg
            qa_ref[rows, :] = _silu(aq) * (A_HEAD_DIM ** -0.5)

            ig = _dot(h, w_ref[:, 2 * A_WIDTH:a_end])
            ai_ref[rows, :] = ig[:, 0:A_WIDTH].astype(BF16)
            ag_ref[rows, :] = ig[:, A_WIDTH:2 * A_WIDTH]
            pb = _dot(h, w_ref[:, a_end:b_end])
            q_ref[rows, :] = pb[:, 0:B_WIDTH]
            k_ref[rows, :] = pb[:, B_WIDTH:2 * B_WIDTH]
            v_ref[rows, :] = pb[:, 2 * B_WIDTH:3 * B_WIDTH]
            bg_ref[rows, :] = pb[:, 3 * B_WIDTH:4 * B_WIDTH]
            pc_ref[rows, :] = _dot(h, w_ref[:, b_end:])
            if transposed:
                kt_ref, vt_ref = outs[N_PROJ_OUT:N_PROJ_OUT + 2]
                kt_ref[0, 0, :, rows] = pb[:, B_WIDTH:2 * B_WIDTH].T
                vt_ref[0, 0, :, rows] = pb[:, 2 * B_WIDTH:3 * B_WIDTH].T
            for c in range(nr // CHUNK):
                carry = None
                for jv in range(CHUNK // SUBLANES):
                    lo = c * CHUNK + jv * SUBLANES
                    s = log_f[lo:lo + SUBLANES, :]
                    for dist in (1, 2, 4):
                        s = s + jnp.where(sub >= dist, pltpu.roll(s, dist, 0), 0.0)
                    if carry is not None:
                        s = s + carry
                    carry = jnp.broadcast_to(s[SUBLANES - 1:SUBLANES, :], s.shape)
                    gcum_ref[r0 + lo:r0 + lo + SUBLANES, :] = s

    @pl.when(i % 2 == 0)
    def _():
        step(h_refs[0], h_refs[1])

    @pl.when(i % 2 == 1)
    def _():
        step(h_refs[1], h_refs[0])


def _proj(x2d, g, w_in, lb, seq_len, layer, depth, kv_t):
    n, d = x2d.shape
    d_in = w_in.shape[1]
    tm = 512 if n % 512 == 0 else CHUNK
    transposed = seq_len % tm == 0 and tm % LANES == 0
    aliased = transposed and kv_t is not None
    row = lambda width: pl.BlockSpec((tm, width), lambda i: (i, 0))
    vec = pl.BlockSpec((1, A_WIDTH), lambda i: (0, 0))
    shapes = [A_WIDTH] * 5 + [B_WIDTH] * 4 + [2 * C_WIDTH]
    dtypes = [F32, F32, F32, BF16, F32] + [F32] * 5
    lb = lb.reshape(1, A_WIDTH)
    last = n // tm - 1
    in_specs = [pl.BlockSpec((tm, d), lambda i: (0, 0)),
                pl.BlockSpec((tm, d), lambda i: (jnp.minimum(i + 1, last), 0)),
                pl.BlockSpec((1, d), lambda i: (0, 0)), pl.BlockSpec((d, d_in), lambda i: (0, 0)),
                vec, vec, vec]
    args = [x2d, x2d, g.reshape(1, d), w_in, jnp.log(lb), jnp.log1p(-lb), 1.0 - lb]
    out_specs = [row(w) for w in shapes]
    out_shape = [jax.ShapeDtypeStruct((n, w), t) for w, t in zip(shapes, dtypes)]
    aliases = {}
    if transposed:
        tiles = seq_len // tm
        t_spec = pl.BlockSpec((1, 1, B_WIDTH, tm), lambda i: (layer, i // tiles, 0, i % tiles))
        out_specs += [t_spec, t_spec]
        out_shape += [jax.ShapeDtypeStruct((depth, n // seq_len, B_WIDTH, seq_len), F32)] * 2
        if aliased:
            in_specs += [pl.BlockSpec(memory_space=pl.ANY)] * 2
            args += list(kv_t)
            aliases = {N_PROJ_IN: N_PROJ_OUT, N_PROJ_IN + 1: N_PROJ_OUT + 1}
    res = pl.pallas_call(
        functools.partial(_proj_kernel, transposed=transposed, aliased=aliased),
        grid=(n // tm,),
        in_specs=in_specs,
        out_specs=out_specs,
        out_shape=out_shape,
        scratch_shapes=[pltpu.VMEM((tm, d), BF16)] * 2,
        input_output_aliases=aliases,
        compiler_params=_params(1),
        name="in_proj",
    )(*args)
    return res[:N_PROJ_OUT], (tuple(res[N_PROJ_OUT:]) if transposed else None)


def _hgrn_kernel(*refs, nb, nch, aliased):
    g_ref, q_ref, k_ref, v_ref, gate_ref, og_ref, s0_ref = refs[:7]
    a_ref, sout_ref, st_ref = refs[7 + aliased:]
    j = pl.program_id(1)

    @pl.when(j == 0)
    def _():
        for n in range(nb):
            for h in range(A_HEADS):
                st_ref[n, h] = s0_ref[0, n, h].T

    rowc = lax.broadcasted_iota(jnp.int32, (CHUNK, CHUNK), 0)
    colc = lax.broadcasted_iota(jnp.int32, (CHUNK, CHUNK), 1)
    col_sub = colc // SUB
    row_k = lax.broadcasted_iota(jnp.int32, (CHUNK, A_HEAD_DIM), 0)
    lane_d = lax.broadcasted_iota(jnp.int32, (SUB, CHUNK), 1)
    n_sub = CHUNK // SUB
    head = [slice(h * A_HEAD_DIM, (h + 1) * A_HEAD_DIM) for h in range(A_HEADS)]

    def sub_starts(g_all):
        return [jnp.zeros((1, A_WIDTH), F32)] + [g_all[SUB * s - 1:SUB * s, :] for s in range(1, n_sub)]

    span = jnp.zeros((1, A_WIDTH), F32)
    for n in range(nb):
        for c in range(nch):
            for s in range(n_sub):
                last = c * CHUNK + SUB * s + SUB - 1
                end = g_ref[n, last:last + 1, :]
                span = jnp.maximum(span, -end if s == 0 else g_ref[n, last - SUB:last - SUB + 1, :] - end)
    small_decay = jnp.max(span) < SAFE_SPAN

    def att_factored(g_all, q_all, k_all, qg_all):
        starts = sub_starts(g_all)
        operands = []
        for sl in head:
            gh, qh, kh = g_all[:, sl], q_all[:, sl], k_all[:, sl]
            parts, krows = [qg_all[:, sl]], []
            for s in range(n_sub):
                rs = slice(SUB * s, SUB * (s + 1))
                st = starts[s][:, sl]
                krows.append(kh[rs] * jnp.exp(st - gh[rs]))
                if s:
                    parts.append(qh[SUB * s:, :] * jnp.exp(gh[SUB * s:, :] - st))
            operands.append((jnp.concatenate(parts, axis=0).astype(BF16),
                             jnp.concatenate(krows, axis=0).astype(BF16)))
        rrs = [lax.dot_general(qr, kr, NT_DIMS, preferred_element_type=F32) for qr, kr in operands]
        atts = []
        for rr in rrs:
            att = rr[0:CHUNK, :]
            r0 = CHUNK
            for s in range(1, n_sub):
                rows = CHUNK - SUB * s
                block = jnp.concatenate([jnp.zeros((SUB * s, CHUNK), F32), rr[r0:r0 + rows, :]], axis=0)
                att = jnp.where(col_sub == s, block, att)
                r0 += rows
            atts.append(jnp.where(rowc >= colc, att, 0.0))
        return atts

    def att_exact(g_all, q_all, k_all, qg_all):
        def one_head(sl):
            gh, qh, kh = g_all[:, sl], q_all[:, sl], k_all[:, sl]
            parts, refrows = [], []
            for jsub in range(n_sub):
                ref = gh[SUB * jsub + SUB - 1:SUB * jsub + SUB, :]
                refrows.append(jnp.broadcast_to(ref, (SUB, A_HEAD_DIM)))
                if jsub < n_sub - 1:
                    later = row_k >= SUB * (jsub + 1)
                    parts.append(jnp.where(later, qh * jnp.exp(jnp.minimum(gh - ref, 0.0)), 0.0))
            q_rel = jnp.concatenate(parts, axis=0).astype(BF16)
            k_rel = (kh * jnp.exp(jnp.minimum(jnp.concatenate(refrows, axis=0) - gh, 0.0))).astype(BF16)
            rr = lax.dot_general(q_rel, k_rel, NT_DIMS, preferred_element_type=F32)
            att = jnp.zeros((CHUNK, CHUNK), F32)
            for jsub in range(n_sub - 1):
                att = jnp.where(col_sub == jsub, rr[jsub * CHUNK:(jsub + 1) * CHUNK, :], att)

            dparts = []
            for csub in range(n_sub):
                rs = slice(SUB * csub, SUB * (csub + 1))
                gc, qc, kc = gh[rs], qh[rs], kh[rs]
                accd = jnp.zeros((SUB, CHUNK), F32)
                for s in range(SUB):
                    dec = jnp.exp(jnp.minimum(gc - gc[s:s + 1, :], 0.0))
                    col = jnp.sum(dec * qc * kc[s:s + 1, :], axis=-1, keepdims=True)
                    accd = jnp.where(lane_d == SUB * csub + s, col, accd)
                dparts.append(accd)
            return att + jnp.where(rowc >= colc, jnp.concatenate(dparts, axis=0), 0.0)

        return [one_head(sl) for sl in head]

    def chunk(n, r0, sts, att_fn):
        g_all = g_ref[n, pl.ds(r0, CHUNK), :]
        q_all = q_ref[n, pl.ds(r0, CHUNK), :]
        k_all = k_ref[n, pl.ds(r0, CHUNK), :]
        v_all = v_ref[n, pl.ds(r0, CHUNK), :]
        ag = gate_ref[n, pl.ds(r0, CHUNK), :]
        qg_all = q_all * jnp.exp(g_all)
        gl_all = g_all[CHUNK - 1:CHUNK, :]
        kd_all = (k_all * jnp.exp(gl_all - g_all)).astype(BF16)
        qg_bf = qg_all.astype(BF16)
        atts = att_fn(g_all, q_all, k_all, qg_all)

        o_state = [lax.dot_general(qg_bf[:, sl], st.astype(BF16), NT_DIMS, preferred_element_type=F32)
                   for sl, st in zip(head, sts)]
        o_pairs = [_dot(att.astype(BF16), v_all[:, sl]) for sl, att in zip(head, atts)]
        updates = [lax.dot_general(v_all[:, sl], kd_all[:, sl], TN_DIMS, preferred_element_type=F32)
                   for sl in head]
        new_sts = [st * jnp.exp(gl_all[:, sl]) + up for sl, st, up in zip(head, sts, updates)]
        o = jnp.concatenate([a + b for a, b in zip(o_state, o_pairs)], axis=1)
        sq = o * o
        inv = jnp.concatenate(
            [jnp.broadcast_to(lax.rsqrt(jnp.mean(sq[:, sl], axis=-1, keepdims=True) + EPS), (CHUNK, A_HEAD_DIM))
             for sl in head], axis=1)
        a_ref[n, pl.ds(r0, CHUNK), :] = (o * inv * og_ref[...] * _silu(ag)).astype(BF16)
        return new_sts

    @pl.when(small_decay)
    def _():
        for n in range(nb):
            sts = [st_ref[n, h] for h in range(A_HEADS)]
            for c in range(nch):
                sts = chunk(n, c * CHUNK, sts, att_factored)
            for h in range(A_HEADS):
                st_ref[n, h] = sts[h]

    @pl.when(jnp.logical_not(small_decay))
    def _():
        for n in range(nb):
            def body(c, carry):
                sts = chunk(n, pl.multiple_of(c * CHUNK, CHUNK), [st_ref[n, h] for h in range(A_HEADS)], att_exact)
                for h in range(A_HEADS):
                    st_ref[n, h] = sts[h]
                return carry
            lax.fori_loop(0, nch, body, 0)

    @pl.when(j == pl.num_programs(1) - 1)
    def _():
        for n in range(nb):
            for h in range(A_HEADS):
                sout_ref[0, n, h] = st_ref[n, h].T


def _hgrn(gates, onorm_g, s0, s0_layer, layer, depth, s_all):
    b, l, _ = gates[0].shape
    tl = HGRN_ROWS if l % HGRN_ROWS == 0 else CHUNK
    nb = max(1, min(b, HGRN_ROWS // tl))
    while b % nb:
        nb -= 1
    tile = pl.BlockSpec((nb, tl, A_WIDTH), lambda i, j: (i, j, 0))
    st_block = (1, nb, A_HEADS, A_HEAD_DIM, A_HEAD_DIM)
    aliased = s_all is not None
    in_specs = [tile] * 5 + [pl.BlockSpec((1, A_WIDTH), lambda i, j: (0, 0)),
                             pl.BlockSpec(st_block, lambda i, j: (s0_layer, i, 0, 0, 0))]
    args = list(gates) + [onorm_g.reshape(1, A_WIDTH), s0]
    if aliased:
        in_specs.append(pl.BlockSpec(memory_space=pl.ANY))
        args.append(s_all)
    return pl.pallas_call(
        functools.partial(_hgrn_kernel, nb=nb, nch=tl // CHUNK, aliased=aliased),
        grid=(b // nb, l // tl),
        in_specs=in_specs,
        out_specs=[pl.BlockSpec((nb, tl, A_WIDTH), lambda i, j: (i, j, 0)),
                   pl.BlockSpec(st_block, lambda i, j: (layer, i, 0, 0, 0))],
        out_shape=[jax.ShapeDtypeStruct((b, l, A_WIDTH), BF16),
                   jax.ShapeDtypeStruct((depth, b, A_HEADS, A_HEAD_DIM, A_HEAD_DIM), F32)],
        scratch_shapes=[pltpu.VMEM((nb, A_HEADS, A_HEAD_DIM, A_HEAD_DIM), F32)],
        input_output_aliases={7: 1} if aliased else {},
        compiler_params=_params(2),
        name="hgrn2",
    )(*args)


def _sb_kernel(*refs, ns, lq, n_past):
    if n_past:
        q_ref, k_ref, v_ref, g_ref, pk_ref, pv_ref, o_ref = refs[:7]
        scratch = refs[7:]
    else:
        q_ref, k_ref, v_ref, g_ref, o_ref = refs[:5]
        pk_ref = pv_ref = None
        scratch = refs[5:]
    kc_ref, vtc_ref, stk_ref, stv_ref, stq_ref, qtm_ref, carry_ref, acc_ref = scratch
    j = pl.program_id(1)
    kb_rows = KEY_BLOCK
    run_len = kb_rows // SUBLANES
    width = B_WIDTH
    n_halves = width // LANES
    streams = range(ns)

    def permuted(ref, s):
        return jnp.concatenate(
            [jnp.concatenate([ref[s, c, pl.ds(jj, SUBLANES, stride=run_len), :] for jj in range(run_len)], axis=0)
             for c in range(n_halves)], axis=1)

    def stage(dst_ref, s, rows):
        for c in range(n_halves):
            if rows.shape[0] < kb_rows:
                dst_ref[s, c, rows.shape[0]:kb_rows, :] = jnp.zeros((kb_rows - rows.shape[0], LANES), F32)
            dst_ref[s, c, 0:rows.shape[0], :] = rows[:, c * LANES:(c + 1) * LANES]

    row_head = lax.broadcasted_iota(jnp.int32, (width, kb_rows), 0) // B_HEAD_DIM
    for s in streams:
        stage(stk_ref, s, k_ref[s])
        stage(stv_ref, s, v_ref[s])
        stage(stq_ref, s, q_ref[s] * (0.5 * B_HEAD_DIM ** -0.5))
    for s in streams:
        kc_ref[s, j] = permuted(stk_ref, s).astype(BF16)
        vtc_ref[s, j] = permuted(stv_ref, s).T.astype(BF16)
        qt = jnp.concatenate([stq_ref[s, c] for c in range(n_halves)], axis=1).T
        qtm_ref[s] = jnp.concatenate([jnp.where(row_head == h, qt, 0.0) for h in range(B_HEADS)],
                                     axis=1).astype(BF16)

    row = lax.broadcasted_iota(jnp.int32, (kb_rows, kb_rows), 0)
    lane = lax.broadcasted_iota(jnp.int32, (kb_rows, kb_rows), 1)
    causal = (row % SUBLANES) * run_len + row // SUBLANES < lane
    sub = lax.broadcasted_iota(jnp.int32, (SUBLANES, kb_rows), 0)

    carry_ref[...] = jnp.ones(carry_ref.shape, F32)
    acc_ref[...] = jnp.zeros(acc_ref.shape, F32)

    def process(blocks, masked):
        zts = [_dot(kb, qtm_ref[s]) for s, (kb, _) in enumerate(blocks)]
        ws = []
        for s, zt in enumerate(zts):
            for h in range(B_HEADS):
                ls = slice(h * kb_rows, (h + 1) * kb_rows)
                half_t = 0.5 * jnp.tanh(zt[:, ls])
                beta = 0.5 + half_t
                omb = 0.5 - half_t
                if masked:
                    beta = jnp.where(causal, beta, 0.0)
                    omb = jnp.where(causal, omb, 1.0)
                run = jnp.ones((SUBLANES, kb_rows), F32)
                excl = [None] * run_len
                for jj in reversed(range(run_len)):
                    excl[jj] = run
                    run = run * omb[jj * SUBLANES:(jj + 1) * SUBLANES]
                inc = run
                for d in (1, 2, 4):
                    inc = jnp.where(sub + d < SUBLANES, inc * pltpu.roll(inc, SUBLANES - d, 0), inc)
                carry = carry_ref[s, :, ls]
                off = jnp.where(sub < SUBLANES - 1, pltpu.roll(inc, SUBLANES - 1, 0), 1.0) * carry
                ws.append(jnp.concatenate([beta[jj * SUBLANES:(jj + 1) * SUBLANES] * (excl[jj] * off)
                                           for jj in range(run_len)], axis=0).astype(BF16))
                carry_ref[s, :, ls] = carry * jnp.broadcast_to(inc[0:1, :], (SUBLANES, kb_rows))
        for s, (_, vtb) in enumerate(blocks):
            for h in range(B_HEADS):
                hs = slice(h * B_HEAD_DIM, (h + 1) * B_HEAD_DIM)
                acc_ref[s, hs, :] += _dot(vtb[hs, :], ws[s * B_HEADS + h])

    def alive():
        return (jnp.max(carry_ref[...]) > 0.0).astype(jnp.int32)

    process([(kc_ref[s, j], vtc_ref[s, j]) for s in streams], True)

    def cond(state):
        kb, live = state
        return jnp.logical_and(kb >= 0, live > 0)

    def body(state):
        kb, _ = state
        process([(kc_ref[s, kb], vtc_ref[s, kb]) for s in streams], False)
        return kb - 1, alive()

    lax.while_loop(cond, body, (j - 1, alive()))

    for kb in reversed(range(n_past)):
        @pl.when(alive() > 0)
        def _():
            cols = slice(kb * kb_rows, (kb + 1) * kb_rows)
            for s in streams:
                kn = pk_ref[0, s, :, cols].T
                vn = pv_ref[0, s, :, cols].T
                for c in range(n_halves):
                    stk_ref[s, c] = kn[:, c * LANES:(c + 1) * LANES]
                    stv_ref[s, c] = vn[:, c * LANES:(c + 1) * LANES]
            process([(permuted(stk_ref, s).astype(BF16), permuted(stv_ref, s).T.astype(BF16)) for s in streams],
                    False)

    for s in streams:
        o = acc_ref[s].T[0:lq, :]
        o_ref[s] = (o * _silu(g_ref[s])).astype(BF16)


def _sb(q, k, v, g, past_t, layer):
    b, l, w = q.shape
    lq = min(KEY_BLOCK, l)
    nblk = l // lq
    ns = min(b, SB_SEQS)
    while b % ns:
        ns -= 1
    p = 0 if past_t is None else past_t[0].shape[3]
    blk = pl.BlockSpec((ns, lq, w), lambda i, j: (i, j, 0))
    in_specs = [blk] * 4
    args = [q, k, v, g]
    if p:
        in_specs += [pl.BlockSpec((1, ns, w, p), lambda i, j: (layer, i, 0, 0))] * 2
        args += list(past_t)
    staging = pltpu.VMEM((ns, w // LANES, KEY_BLOCK, LANES), F32)
    return pl.pallas_call(
        functools.partial(_sb_kernel, ns=ns, lq=lq, n_past=p // KEY_BLOCK),
        grid=(b // ns, nblk),
        in_specs=in_specs,
        out_specs=blk,
        out_shape=jax.ShapeDtypeStruct((b, l, w), BF16),
        scratch_shapes=[pltpu.VMEM((ns, nblk, KEY_BLOCK, w), BF16),
                        pltpu.VMEM((ns, nblk, w, KEY_BLOCK), BF16),
                        staging, staging, staging,
                        pltpu.VMEM((ns, w, B_HEADS * KEY_BLOCK), BF16),
                        pltpu.VMEM((ns, SUBLANES, B_HEADS * KEY_BLOCK), F32),
                        pltpu.VMEM((ns, w, KEY_BLOCK), F32)],
        compiler_params=_params(2),
        name="sb_attn",
    )(*args)


def _post_kernel(x_ref, a_ref, b_ref, pc_ref, pprev_ref, mk_hbm, mv_hbm, wpool_ref, pscale_ref,
                 woa_ref, wob_ref, woc_ref, npost_ref, nxpre_ref, wxq_ref, wxo_ref, nxpost_ref,
                 xo_ref, pnew_ref, ext_ref, sum_ref, kv_ref, mkb_ref, mvb_ref, sem_ref,
                 *, nb, tl, n_split, past, layer):
    i = pl.program_id(0)
    j = pl.program_id(1)
    n_groups = pl.num_programs(0)
    halo = POOL_STATE + 1
    lead = SUBLANES
    dh = kv_ref.shape[-1]

    def kv_copies(group, slot):
        return [pltpu.make_async_copy(src.at[layer, group * nb + n, :, h, :], kv_ref.at[slot, t, n, h],
                                      sem_ref.at[slot, t])
                for t, src in enumerate((mk_hbm, mv_hbm)) for n in range(nb) for h in range(X_HEADS)]

    @pl.when(j == 0)
    def _():
        slot = i % 2

        @pl.when(i == 0)
        def _():
            for cp in kv_copies(0, 0):
                cp.start()

        @pl.when(i + 1 < n_groups)
        def _():
            for cp in kv_copies(i + 1, 1 - slot):
                cp.start()

        for cp in kv_copies(i, slot):
            cp.wait()
        for n in range(nb):
            ext_ref[n, 0:lead, :] = jnp.zeros((lead, C_WIDTH), F32)
            ext_ref[n, lead:lead + halo, :] = pprev_ref[n]
            for gi in range(len(POOL_WINDOWS) - 1):
                sum_ref[gi, n, 0:lead, :] = jnp.zeros((lead, C_WIDTH), F32)
            for h in range(X_HEADS):
                mkb_ref[n, :, h * dh:(h + 1) * dh] = kv_ref[slot, 0, n, h].astype(BF16)
                mvb_ref[n, :, h * dh:(h + 1) * dh] = kv_ref[slot, 1, n, h].astype(BF16)

    grp = lax.broadcasted_iota(jnp.int32, (tl, C_WIDTH), 1) // C_GROUP_DIM
    pos = past + j * tl + lax.broadcasted_iota(jnp.int32, (tl, C_WIDTH), 0)
    dds = []
    rows = halo + tl
    for n in range(nb):
        cu = pc_ref[n, :, 0:C_WIDTH]
        ext_ref[n, lead + halo:lead + rows, :] = cu
        win = jnp.zeros((tl, C_WIDTH), F32)
        wlen = jnp.zeros((tl, C_WIDTH), jnp.int32)
        src = ext_ref.at[n]
        for gi, w in enumerate(POOL_WINDOWS):
            assert w == 2 ** (gi + 1)
            wsum = src[lead:lead + rows, :] + src[lead - w // 2:lead - w // 2 + rows, :]
            if gi + 1 < len(POOL_WINDOWS):
                sum_ref[gi, n, lead:lead + rows, :] = wsum
                src = sum_ref.at[gi, n]
            win = jnp.where(grp == gi, wsum[halo:, :], win)
            wlen = jnp.where(grp == gi, w, wlen)
        cnt = jnp.minimum(pos + 1, wlen).astype(F32)
        dds.append((win / cnt - cu).astype(BF16))
        pnew_ref[n] = ext_ref[n, lead + tl + 1:lead + tl + halo, :]
        ext_ref[n, lead:lead + halo, :] = ext_ref[n, lead + tl:lead + tl + halo, :]

    seg = tl // n_split if nb == 1 else tl
    groups = [[(0, g * seg)] for g in range(n_split)] if nb == 1 else \
             [[(n, 0) for n in range(g * nb // n_split, (g + 1) * nb // n_split)] for g in range(n_split)]

    def rows(members, ref, w0=0, w1=None):
        return jnp.concatenate([ref[n, r0:r0 + seg, w0:w1] for n, r0 in members], axis=0)

    mixes = []
    for members in groups:
        dd = jnp.concatenate([dds[n][r0:r0 + seg] for n, r0 in members], axis=0)
        y = _dot(dd, wpool_ref[...]) * pscale_ref[...]
        c_out = (y * _silu(rows(members, pc_ref, C_WIDTH, 2 * C_WIDTH))).astype(BF16)
        mixes.append(_dot(rows(members, a_ref), woa_ref[...]) + _dot(rows(members, b_ref), wob_ref[...])
                     + _dot(c_out, woc_ref[...]))
    x1s = [rows(members, x_ref) + _rms(mix, npost_ref[...]) for members, mix in zip(groups, mixes)]

    qs = [_dot(_rms(x1, nxpre_ref[...]).astype(BF16), wxq_ref[...]).astype(BF16) for x1 in x1s]
    pairs = [(g, m, n, h) for g, members in enumerate(groups) for m, (n, _) in enumerate(members)
             for h in range(X_HEADS)]
    scores = [lax.dot_general(qs[g][m * seg:(m + 1) * seg, h * dh:(h + 1) * dh],
                              mkb_ref[n, :, h * dh:(h + 1) * dh], NT_DIMS,
                              preferred_element_type=F32) * (dh ** -0.5) for g, m, n, h in pairs]
    probs = []
    for s in scores:
        ex = jnp.exp(s - jnp.max(s, axis=-1, keepdims=True))
        probs.append((ex / jnp.sum(ex, axis=-1, keepdims=True)).astype(BF16))
    ctx = [_dot(p, mvb_ref[n, :, h * dh:(h + 1) * dh]).astype(BF16) for p, (g, m, n, h) in zip(probs, pairs)]
    xos = []
    for g, members in enumerate(groups):
        seqs = [jnp.concatenate([c for c, (g2, m2, _, _) in zip(ctx, pairs) if g2 == g and m2 == m], axis=1)
                for m in range(len(members))]
        xos.append(_dot(jnp.concatenate(seqs, axis=0), wxo_ref[...]))
    for members, x1, xo in zip(groups, x1s, xos):
        res = x1 + _rms(xo, nxpost_ref[...])
        for m, (n, r0) in enumerate(members):
            xo_ref[n, r0:r0 + seg, :] = res[m * seg:(m + 1) * seg]


def _post(x, a_out, b_out, pc, pool_prev, mk, mv, wts, past, layer):
    b, l, d = x.shape
    n_mem, dh = mk.shape[2], mk.shape[4]
    tl = POST_ROWS if l % POST_ROWS == 0 else CHUNK
    nb = max(1, min(b, POST_ROWS // tl, POST_SEQS))
    while b % nb:
        nb -= 1
    n_split = 2 if (nb * tl) % (2 * LANES) == 0 and (nb == 1 or nb % 2 == 0) else 1
    halo = POOL_STATE + 1
    pprev = jnp.concatenate([jnp.zeros((b, 1, C_WIDTH), F32), pool_prev], axis=1)
    tile = lambda w: pl.BlockSpec((nb, tl, w), lambda i, j: (i, j, 0))
    per_seq = lambda r, w: pl.BlockSpec((nb, r, w), lambda i, j: (i, 0, 0))
    full = lambda a: pl.BlockSpec(a.shape, lambda i, j: (0,) * a.ndim)
    hbm = pl.BlockSpec(memory_space=pl.ANY)
    return pl.pallas_call(
        functools.partial(_post_kernel, nb=nb, tl=tl, n_split=n_split, past=past, layer=layer),
        grid=(b // nb, l // tl),
        in_specs=[tile(d), tile(A_WIDTH), tile(B_WIDTH), tile(2 * C_WIDTH), per_seq(halo, C_WIDTH),
                  hbm, hbm] + [full(a) for a in wts],
        out_specs=[tile(d), per_seq(POOL_STATE, C_WIDTH)],
        out_shape=[jax.ShapeDtypeStruct((b, l, d), F32),
                   jax.ShapeDtypeStruct((b, POOL_STATE, C_WIDTH), F32)],
        scratch_shapes=[pltpu.VMEM((nb, SUBLANES + halo + tl, C_WIDTH), F32),
                        pltpu.VMEM((len(POOL_WINDOWS) - 1, nb, SUBLANES + halo + tl, C_WIDTH), F32),
                        pltpu.VMEM((2, 2, nb, X_HEADS, n_mem, dh), F32),
                        pltpu.VMEM((nb, n_mem, d), BF16),
                        pltpu.VMEM((nb, n_mem, d), BF16),
                        pltpu.SemaphoreType.DMA((2, 2))],
        compiler_params=_params(2),
        name="mix_out",
    )(x, a_out, b_out, pc, pprev, mk, mv, *wts)


def _layer(x, mk, mv, past_t, s0, s0_layer, pool_prev, lw, layer, depth, chain):
    kv_t, s_all = chain
    b, l, d = x.shape
    n = b * l
    proj, kv_t = _proj(x.reshape(n, d), lw["n_pre"], lw["w_in"], lw["lb"], l, layer, depth, kv_t)
    seq = lambda t: t.reshape(b, l, -1)
    gates, (bq, bk, bv, bg, pc) = [seq(t) for t in proj[:5]], proj[5:]
    a_out, s_all = _hgrn(gates, lw["onorm_g"], s0, s0_layer, layer, depth, s_all)
    b_out = _sb(seq(bq), seq(bk), seq(bv), seq(bg), past_t, layer)
    past = 0 if past_t is None else past_t[0].shape[3]
    x_new, pool_new = _post(x, a_out, b_out, seq(pc), pool_prev, mk, mv, lw["post"], past, layer)
    heads = lambda t: t.reshape(b, l, B_HEADS, B_HEAD_DIM)
    return x_new, heads(bk), heads(bv), (kv_t, s_all), pool_new


def kernel(x_prompt, x_sample, mem_prompt, cache_sb_k, cache_sb_v, state_hgrn, state_pool, cache_mem_k, cache_mem_v, norm_mix_pre, norm_mix_post, w_in, hgrn_lb_logits, hgrn_onorm_g, w_pool, pool_scale, w_out, norm_x_pre, norm_x_post, norm_mem, w_xq, w_xk, w_xv, w_xo):
    depth, d = norm_mix_pre.shape
    bp = x_prompt.shape[0]
    bs, past = cache_sb_k.shape[1], cache_sb_k.shape[2]
    n_mem = mem_prompt.shape[1]

    lbs = jax.nn.softmax(hgrn_lb_logits.astype(F32), axis=0)
    lower_bounds = jnp.maximum(jnp.cumsum(lbs, axis=0) - lbs[0], 0.0)

    mk_all, mv_all = _memkv(mem_prompt, norm_mem, w_xk.astype(BF16), w_xv.astype(BF16))

    def seq_minor(c):
        return jnp.transpose(c, (0, 1, 3, 4, 2)).reshape(depth, bs, B_WIDTH, past)

    def seq_major(t):
        dp, b, _, l = t.shape
        return jnp.transpose(t.reshape(dp, b, B_HEADS, B_HEAD_DIM, l), (0, 1, 4, 2, 3))

    past_t = (seq_minor(cache_sb_k), seq_minor(cache_sb_v))

    row = lambda v: v.reshape(1, -1)
    s_zero = jnp.zeros((1, bp, A_HEADS, A_HEAD_DIM, A_HEAD_DIM), F32)
    pool_zero = jnp.zeros((bp, POOL_STATE, C_WIDTH), F32)

    xp, xs = x_prompt, x_sample
    p_kn, p_vn, p_pools, s_kn, s_vn, s_pools = [], [], [], [], [], []
    chain_p = chain_s = (None, None)
    for l in range(depth):
        wp = jnp.zeros((C_WIDTH, C_WIDTH), F32)
        for gi in range(C_GROUPS):
            gs = slice(gi * C_GROUP_DIM, (gi + 1) * C_GROUP_DIM)
            wp = wp.at[gs, gs].set(w_pool[l, gi])
        wo = w_out[l].astype(BF16)
        lw = {
            "n_pre": norm_mix_pre[l], "w_in": w_in[l].astype(BF16), "lb": lower_bounds[l],
            "onorm_g": hgrn_onorm_g[l],
            "post": (wp.astype(BF16), row(pool_scale[l]), wo[0:A_WIDTH], wo[A_WIDTH:A_WIDTH + B_WIDTH],
                     wo[A_WIDTH + B_WIDTH:], row(norm_mix_post[l]), row(norm_x_pre[l]),
                     w_xq[l].astype(BF16), w_xo[l].astype(BF16), row(norm_x_post[l])),
        }
        xp, kb, vb, chain_p, pn = _layer(xp, mk_all, mv_all, None, s_zero, 0, pool_zero, lw, l, depth, chain_p)
        p_kn.append(kb); p_vn.append(vb); p_pools.append(pn)
        xs, kb, vb, chain_s, pn = _layer(xs, cache_mem_k, cache_mem_v, past_t, state_hgrn, l, state_pool[l],
                                         lw, l, depth, chain_s)
        s_kn.append(kb); s_vn.append(vb); s_pools.append(pn)

    def new_kv(kvt, nat_k, nat_v):
        if kvt is not None:
            return seq_major(kvt[0]), seq_major(kvt[1])
        return jnp.stack(nat_k), jnp.stack(nat_v)

    p_k, p_v = new_kv(chain_p[0], p_kn, p_vn)
    s_k, s_v = new_kv(chain_s[0], s_kn, s_vn)
    return (xp, xs, p_k, p_v, chain_p[1], jnp.stack(p_pools), mk_all, mv_all, s_k, s_v, chain_s[1],
            jnp.stack(s_pools))
```

```python
import functools

import jax
import jax.numpy as jnp
from jax import lax
from jax.experimental import pallas as pl
from jax.experimental.pallas import tpu as pltpu

F32 = jnp.float32
BF16 = jnp.bfloat16
EPS = 1e-6

A_HEADS, A_HEAD_DIM = 4, 128
A_WIDTH = A_HEADS * A_HEAD_DIM
B_HEADS, B_HEAD_DIM = 4, 64
B_WIDTH = B_HEADS * B_HEAD_DIM
C_GROUPS, C_GROUP_DIM = 4, 64
C_WIDTH = C_GROUPS * C_GROUP_DIM
POOL_WINDOWS = (2, 4, 8, 16)
POOL_STATE = 15
X_HEADS = 4
CHUNK = 64
SUB = 16
SAFE_SPAN = 40.0
KEY_BLOCK = 128
SB_SEQS = 4
HGRN_ROWS = 512
POST_ROWS = 512
POST_SEQS = 4
SUBLANES = 8
LANES = 128
VMEM_LIMIT = 56 * 1024 * 1024

NT_DIMS = (((1,), (1,)), ((), ()))
TN_DIMS = (((0,), (0,)), ((), ()))


def _rms(x, g):
    ms = jnp.mean(x * x, axis=-1, keepdims=True)
    return x * lax.rsqrt(ms + EPS) * g


def _silu(x):
    return x * (1.0 / (1.0 + jnp.exp(-x)))


def _dot(a, b):
    return jnp.dot(a, b, preferred_element_type=F32)


def _params(n_grid, flags=None):
    return pltpu.CompilerParams(dimension_semantics=("arbitrary",) * n_grid,
                                vmem_limit_bytes=VMEM_LIMIT, flags=flags)


def _memkv_kernel(mem_ref, g_ref, wk_ref, wv_ref, k_ref, v_ref):
    nb, n_mem, d = mem_ref.shape
    dh = d // X_HEADS
    for b in range(nb):
        m = _rms(mem_ref[b], g_ref[0]).astype(BF16)
        k = _dot(m, wk_ref[0])
        v = _dot(m, wv_ref[0])
        for h in range(X_HEADS):
            k_ref[0, b, :, h, :] = k[:, h * dh:(h + 1) * dh]
            v_ref[0, b, :, h, :] = v[:, h * dh:(h + 1) * dh]


def _memkv(mem, norm_mem, wk, wv):
    depth, d = norm_mem.shape
    nb, n_mem, _ = mem.shape
    dh = d // X_HEADS
    out = jax.ShapeDtypeStruct((depth, nb, n_mem, X_HEADS, dh), F32)
    return pl.pallas_call(
        _memkv_kernel,
        grid=(depth,),
        in_specs=[pl.BlockSpec((nb, n_mem, d), lambda l: (0, 0, 0)),
                  pl.BlockSpec((1, 1, d), lambda l: (l, 0, 0)),
                  pl.BlockSpec((1, d, d), lambda l: (l, 0, 0)),
                  pl.BlockSpec((1, d, d), lambda l: (l, 0, 0))],
        out_specs=[pl.BlockSpec((1, nb, n_mem, X_HEADS, dh), lambda l: (l, 0, 0, 0, 0))] * 2,
        out_shape=[out, out],
        compiler_params=_params(1),
        name="mem_kv",
    )(mem, norm_mem.reshape(depth, 1, d), wk, wv)


N_PROJ_IN = 7
PROJ_GROUPS = 4
N_PROJ_OUT = 10


def _proj_kernel(*refs, transposed, aliased):
    x0_ref, xn_ref, g_ref, w_ref, loglb_ref, log1mlb_ref, omlb_ref = refs[:N_PROJ_IN]
    outs = refs[N_PROJ_IN + 2 * aliased:]
    gcum_ref, qa_ref, ka_ref, ai_ref, ag_ref, q_ref, k_ref, v_ref, bg_ref, pc_ref = outs[:N_PROJ_OUT]
    h_ref = outs[-1]
    i = pl.program_id(0)
    a_end = 4 * A_WIDTH
    b_end = a_end + 4 * B_WIDTH
    tm = xn_ref.shape[0]
    n_groups = PROJ_GROUPS if tm % (PROJ_GROUPS * LANES) == 0 else 1
    groups = [(g * tm // n_groups, tm // n_groups) for g in range(n_groups)]
    sub = lax.broadcasted_iota(jnp.int32, (SUBLANES, A_WIDTH), 0)

    @pl.when(i == 0)
    def _():
        h_ref[...] = _rms(x0_ref[...], g_ref[...]).astype(BF16)

    def step():
        h_next = _rms(xn_ref[...], g_ref[...]).astype(BF16)
        hs = [h_ref[r0:r0 + nr, :] for r0, nr in groups]
        qfs = [_dot(h, w_ref[:, 0:2 * A_WIDTH]) for h in hs]
        for (r0, nr), h, qf in zip(groups, hs, qfs):
            rows = slice(r0, r0 + nr)
            aq, af = qf[:, 0:A_WIDTH], qf[:, A_WIDTH:2 * A_WIDTH]
            e = jnp.exp(-jnp.abs(af))
            ope = 1.0 + e
            logsig = jnp.minimum(af, 0.0) - jnp.log(ope)
            r = 1.0 / ope
            sig_neg = jnp.where(af >= 0, e * r, r)
            la = loglb_ref[...]
            lb_ = log1mlb_ref[...] + logsig
            log_f = jnp.maximum(la, lb_) + jnp.log(1.0 + jnp.exp(-jnp.abs(la - lb_)))
            ka_ref[rows, :] = (omlb_ref[...] * sig_neg).astype(BF16)
            qa_ref[rows, :] = (_silu(aq) * (A_HEAD_DIM ** -0.5)).astype(BF16)

            ig = _dot(h, w_ref[:, 2 * A_WIDTH:a_end])
            ai_ref[rows, :] = ig[:, 0:A_WIDTH].astype(BF16)
            ag_ref[rows, :] = ig[:, A_WIDTH:2 * A_WIDTH].astype(BF16)
            pb = _dot(h, w_ref[:, a_end:b_end])
            q_ref[rows, :] = pb[:, 0:B_WIDTH]
            k_ref[rows, :] = pb[:, B_WIDTH:2 * B_WIDTH]
            v_ref[rows, :] = pb[:, 2 * B_WIDTH:3 * B_WIDTH]
            bg_ref[rows, :] = pb[:, 3 * B_WIDTH:4 * B_WIDTH].astype(BF16)
            pc_ref[rows, :] = _dot(h, w_ref[:, b_end:])
            if transposed:
                kt_ref, vt_ref = outs[N_PROJ_OUT:N_PROJ_OUT + 2]
                kt_ref[0, 0, :, rows] = pb[:, B_WIDTH:2 * B_WIDTH].T
                vt_ref[0, 0, :, rows] = pb[:, 2 * B_WIDTH:3 * B_WIDTH].T
            for c in range(nr // CHUNK):
                carry = None
                for jv in range(CHUNK // SUBLANES):
                    lo = c * CHUNK + jv * SUBLANES
                    s = log_f[lo:lo + SUBLANES, :]
                    for dist in (1, 2, 4):
                        s = s + jnp.where(sub >= dist, pltpu.roll(s, dist, 0), 0.0)
                    if carry is not None:
                        s = s + carry
                    carry = jnp.broadcast_to(s[SUBLANES - 1:SUBLANES, :], s.shape)
                    gcum_ref[r0 + lo:r0 + lo + SUBLANES, :] = s

        h_ref[...] = h_next

    step()


def _proj(x2d, g, w_in, lb, seq_len, layer, depth, kv_t):
    n, d = x2d.shape
    d_in = w_in.shape[1]
    tm = 512 if n % 512 == 0 else CHUNK
    transposed = seq_len % tm == 0 and tm % LANES == 0
    aliased = transposed and kv_t is not None
    row = lambda width: pl.BlockSpec((tm, width), lambda i: (i, 0))
    vec = pl.BlockSpec((1, A_WIDTH), lambda i: (0, 0))
    shapes = [A_WIDTH] * 5 + [B_WIDTH] * 4 + [2 * C_WIDTH]
    dtypes = [F32, BF16, BF16, BF16, BF16] + [F32, F32, F32, BF16, F32]
    lb = lb.reshape(1, A_WIDTH)
    last = n // tm - 1
    in_specs = [pl.BlockSpec((tm, d), lambda i: (0, 0)),
                pl.BlockSpec((tm, d), lambda i: (jnp.minimum(i + 1, last), 0)),
                pl.BlockSpec((1, d), lambda i: (0, 0)), pl.BlockSpec((d, d_in), lambda i: (0, 0)),
                vec, vec, vec]
    args = [x2d, x2d, g.reshape(1, d), w_in, jnp.log(lb), jnp.log1p(-lb), 1.0 - lb]
    out_specs = [row(w) for w in shapes]
    out_shape = [jax.ShapeDtypeStruct((n, w), t) for w, t in zip(shapes, dtypes)]
    aliases = {}
    if transposed:
        tiles = seq_len // tm
        t_spec = pl.BlockSpec((1, 1, B_WIDTH, tm), lambda i: (layer, i // tiles, 0, i % tiles))
        out_specs += [t_spec, t_spec]
        out_shape += [jax.ShapeDtypeStruct((depth, n // seq_len, B_WIDTH, seq_len), F32)] * 2
        if aliased:
            in_specs += [pl.BlockSpec(memory_space=pl.ANY)] * 2
            args += list(kv_t)
            aliases = {N_PROJ_IN: N_PROJ_OUT, N_PROJ_IN + 1: N_PROJ_OUT + 1}
    res = pl.pallas_call(
        functools.partial(_proj_kernel, transposed=transposed, aliased=aliased),
        grid=(n // tm,),
        in_specs=in_specs,
        out_specs=out_specs,
        out_shape=out_shape,
        scratch_shapes=[pltpu.VMEM((tm, d), BF16)],
        input_output_aliases=aliases,
        compiler_params=_params(1),
        name="in_proj",
    )(*args)
    return res[:N_PROJ_OUT], (tuple(res[N_PROJ_OUT:]) if transposed else None)


def _hgrn_kernel(*refs, nb, nch, aliased):
    g_ref, q_ref, k_ref, v_ref, gate_ref, og_ref, s0_ref = refs[:7]
    a_ref, sout_ref, st_ref = refs[7 + aliased:]
    j = pl.program_id(1)

    @pl.when(j == 0)
    def _():
        for n in range(nb):
            for h in range(A_HEADS):
                st_ref[n, h] = s0_ref[0, n, h].T

    rowc = lax.broadcasted_iota(jnp.int32, (CHUNK, CHUNK), 0)
    colc = lax.broadcasted_iota(jnp.int32, (CHUNK, CHUNK), 1)
    col_sub = colc // SUB
    row_k = lax.broadcasted_iota(jnp.int32, (CHUNK, A_HEAD_DIM), 0)
    lane_d = lax.broadcasted_iota(jnp.int32, (SUB, CHUNK), 1)
    n_sub = CHUNK // SUB
    head = [slice(h * A_HEAD_DIM, (h + 1) * A_HEAD_DIM) for h in range(A_HEADS)]

    def sub_starts(g_all):
        return [jnp.zeros((1, A_WIDTH), F32)] + [g_all[SUB * s - 1:SUB * s, :] for s in range(1, n_sub)]

    span = jnp.zeros((1, A_WIDTH), F32)
    for n in range(nb):
        for c in range(nch):
            for s in range(n_sub):
                last = c * CHUNK + SUB * s + SUB - 1
                end = g_ref[n, last:last + 1, :]
                span = jnp.maximum(span, -end if s == 0 else g_ref[n, last - SUB:last - SUB + 1, :] - end)
    small_decay = jnp.max(span) < SAFE_SPAN

    def att_factored(g_all, q_all, k_all, qg_all):
        starts = sub_starts(g_all)
        operands = []
        for sl in head:
            gh, qh, kh = g_all[:, sl], q_all[:, sl], k_all[:, sl]
            parts, krows = [qg_all[:, sl]], []
            for s in range(n_sub):
                rs = slice(SUB * s, SUB * (s + 1))
                st = starts[s][:, sl]
                krows.append(kh[rs] * jnp.exp(st - gh[rs]))
                if s:
                    parts.append(qh[SUB * s:, :] * jnp.exp(gh[SUB * s:, :] - st))
            operands.append((jnp.concatenate(parts, axis=0).astype(BF16),
                             jnp.concatenate(krows, axis=0).astype(BF16)))
        rrs = [lax.dot_general(qr, kr, NT_DIMS, preferred_element_type=F32) for qr, kr in operands]
        atts = []
        for rr in rrs:
            att = rr[0:CHUNK, :]
            r0 = CHUNK
            for s in range(1, n_sub):
                rows = CHUNK - SUB * s
                block = jnp.concatenate([jnp.zeros((SUB * s, CHUNK), F32), rr[r0:r0 + rows, :]], axis=0)
                att = jnp.where(col_sub == s, block, att)
                r0 += rows
            atts.append(jnp.where(rowc >= colc, att, 0.0))
        return atts

    def att_exact(g_all, q_all, k_all, qg_all):
        def one_head(sl):
            gh, qh, kh = g_all[:, sl], q_all[:, sl], k_all[:, sl]
            parts, refrows = [], []
            for jsub in range(n_sub):
                ref = gh[SUB * jsub + SUB - 1:SUB * jsub + SUB, :]
                refrows.append(jnp.broadcast_to(ref, (SUB, A_HEAD_DIM)))
                if jsub < n_sub - 1:
                    later = row_k >= SUB * (jsub + 1)
                    parts.append(jnp.where(later, qh * jnp.exp(jnp.minimum(gh - ref, 0.0)), 0.0))
            q_rel = jnp.concatenate(parts, axis=0).astype(BF16)
            k_rel = (kh * jnp.exp(jnp.minimum(jnp.concatenate(refrows, axis=0) - gh, 0.0))).astype(BF16)
            rr = lax.dot_general(q_rel, k_rel, NT_DIMS, preferred_element_type=F32)
            att = jnp.zeros((CHUNK, CHUNK), F32)
            for jsub in range(n_sub - 1):
                att = jnp.where(col_sub == jsub, rr[jsub * CHUNK:(jsub + 1) * CHUNK, :], att)

            dparts = []
            for csub in range(n_sub):
                rs = slice(SUB * csub, SUB * (csub + 1))
                gc, qc, kc = gh[rs], qh[rs], kh[rs]
                accd = jnp.zeros((SUB, CHUNK), F32)
                for s in range(SUB):
                    dec = jnp.exp(jnp.minimum(gc - gc[s:s + 1, :], 0.0))
                    col = jnp.sum(dec * qc * kc[s:s + 1, :], axis=-1, keepdims=True)
                    accd = jnp.where(lane_d == SUB * csub + s, col, accd)
                dparts.append(accd)
            return att + jnp.where(rowc >= colc, jnp.concatenate(dparts, axis=0), 0.0)

        return [one_head(sl) for sl in head]

    def chunk(n, r0, sts, att_fn):
        g_all = g_ref[n, pl.ds(r0, CHUNK), :]
        q_all = q_ref[n, pl.ds(r0, CHUNK), :].astype(F32)
        k_all = k_ref[n, pl.ds(r0, CHUNK), :].astype(F32)
        v_all = v_ref[n, pl.ds(r0, CHUNK), :]
        ag = gate_ref[n, pl.ds(r0, CHUNK), :].astype(F32)
        qg_all = q_all * jnp.exp(g_all)
        gl_all = g_all[CHUNK - 1:CHUNK, :]
        kd_all = (k_all * jnp.exp(gl_all - g_all)).astype(BF16)
        qg_bf = qg_all.astype(BF16)
        atts = att_fn(g_all, q_all, k_all, qg_all)

        o_state = [lax.dot_general(qg_bf[:, sl], st.astype(BF16), NT_DIMS, preferred_element_type=F32)
                   for sl, st in zip(head, sts)]
        o_pairs = [_dot(att.astype(BF16), v_all[:, sl]) for sl, att in zip(head, atts)]
        updates = [lax.dot_general(v_all[:, sl], kd_all[:, sl], TN_DIMS, preferred_element_type=F32)
                   for sl in head]
        new_sts = [st * jnp.exp(gl_all[:, sl]) + up for sl, st, up in zip(head, sts, updates)]
        o = jnp.concatenate([a + b for a, b in zip(o_state, o_pairs)], axis=1)
        sq = o * o
        inv = jnp.concatenate(
            [jnp.broadcast_to(lax.rsqrt(jnp.mean(sq[:, sl], axis=-1, keepdims=True) + EPS), (CHUNK, A_HEAD_DIM))
             for sl in head], axis=1)
        a_ref[n, pl.ds(r0, CHUNK), :] = (o * inv * og_ref[...] * _silu(ag)).astype(BF16)
        return new_sts

    @pl.when(small_decay)
    def _():
        for n in range(nb):
            sts = [st_ref[n, h] for h in range(A_HEADS)]
            for c in range(nch):
                sts = chunk(n, c * CHUNK, sts, att_factored)
            for h in range(A_HEADS):
                st_ref[n, h] = sts[h]

    @pl.when(jnp.logical_not(small_decay))
    def _():
        for n in range(nb):
            def body(c, carry):
                sts = chunk(n, pl.multiple_of(c * CHUNK, CHUNK), [st_ref[n, h] for h in range(A_HEADS)], att_exact)
                for h in range(A_HEADS):
                    st_ref[n, h] = sts[h]
                return carry
            lax.fori_loop(0, nch, body, 0)

    @pl.when(j == pl.num_programs(1) - 1)
    def _():
        for n in range(nb):
            for h in range(A_HEADS):
                sout_ref[0, n, h] = st_ref[n, h].T


def _hgrn(gates, onorm_g, s0, s0_layer, layer, depth, s_all):
    b, l, _ = gates[0].shape
    tl = HGRN_ROWS if l % HGRN_ROWS == 0 else CHUNK
    nb = max(1, min(b, HGRN_ROWS // tl))
    while b % nb:
        nb -= 1
    tile = pl.BlockSpec((nb, tl, A_WIDTH), lambda i, j: (i, j, 0))
    st_block = (1, nb, A_HEADS, A_HEAD_DIM, A_HEAD_DIM)
    aliased = s_all is not None
    in_specs = [tile] * 5 + [pl.BlockSpec((1, A_WIDTH), lambda i, j: (0, 0)),
                             pl.BlockSpec(st_block, lambda i, j: (s0_layer, i, 0, 0, 0))]
    args = list(gates) + [onorm_g.reshape(1, A_WIDTH), s0]
    if aliased:
        in_specs.append(pl.BlockSpec(memory_space=pl.ANY))
        args.append(s_all)
    return pl.pallas_call(
        functools.partial(_hgrn_kernel, nb=nb, nch=tl // CHUNK, aliased=aliased),
        grid=(b // nb, l // tl),
        in_specs=in_specs,
        out_specs=[pl.BlockSpec((nb, tl, A_WIDTH), lambda i, j: (i, j, 0)),
                   pl.BlockSpec(st_block, lambda i, j: (layer, i, 0, 0, 0))],
        out_shape=[jax.ShapeDtypeStruct((b, l, A_WIDTH), BF16),
                   jax.ShapeDtypeStruct((depth, b, A_HEADS, A_HEAD_DIM, A_HEAD_DIM), F32)],
        scratch_shapes=[pltpu.VMEM((nb, A_HEADS, A_HEAD_DIM, A_HEAD_DIM), F32)],
        input_output_aliases={7: 1} if aliased else {},
        compiler_params=_params(2),
        name="hgrn2",
    )(*args)


def _sb_kernel(*refs, ns, lq, n_past):
    if n_past:
        q_ref, k_ref, v_ref, g_ref, pk_ref, pv_ref, o_ref = refs[:7]
        scratch = refs[7:]
    else:
        q_ref, k_ref, v_ref, g_ref, o_ref = refs[:5]
        pk_ref = pv_ref = None
        scratch = refs[5:]
    kc_ref, vtc_ref, stk_ref, stv_ref, stq_ref, qtm_ref, carry_ref, acc_ref = scratch
    j = pl.program_id(1)
    kb_rows = KEY_BLOCK
    run_len = kb_rows // SUBLANES
    width = B_WIDTH
    n_halves = width // LANES
    streams = range(ns)

    def permuted(ref, s):
        return jnp.concatenate(
            [jnp.concatenate([ref[s, c, pl.ds(jj, SUBLANES, stride=run_len), :] for jj in range(run_len)], axis=0)
             for c in range(n_halves)], axis=1)

    def stage(dst_ref, s, rows):
        for c in range(n_halves):
            if rows.shape[0] < kb_rows:
                dst_ref[s, c, rows.shape[0]:kb_rows, :] = jnp.zeros((kb_rows - rows.shape[0], LANES), F32)
            dst_ref[s, c, 0:rows.shape[0], :] = rows[:, c * LANES:(c + 1) * LANES]

    row_head = lax.broadcasted_iota(jnp.int32, (width, kb_rows), 0) // B_HEAD_DIM
    for s in streams:
        stage(stk_ref, s, k_ref[s])
        stage(stv_ref, s, v_ref[s])
        stage(stq_ref, s, q_ref[s] * (0.5 * B_HEAD_DIM ** -0.5))
    for s in streams:
        kc_ref[s, j] = permuted(stk_ref, s).astype(BF16)
        vtc_ref[s, j] = permuted(stv_ref, s).T.astype(BF16)
        qt = jnp.concatenate([stq_ref[s, c] for c in range(n_halves)], axis=1).T
        qtm_ref[s] = jnp.concatenate([jnp.where(row_head == h, qt, 0.0) for h in range(B_HEADS)],
                                     axis=1).astype(BF16)

    row = lax.broadcasted_iota(jnp.int32, (kb_rows, kb_rows), 0)
    lane = lax.broadcasted_iota(jnp.int32, (kb_rows, kb_rows), 1)
    causal = (row % SUBLANES) * run_len + row // SUBLANES < lane
    sub = lax.broadcasted_iota(jnp.int32, (SUBLANES, kb_rows), 0)

    carry_ref[...] = jnp.ones(carry_ref.shape, F32)
    acc_ref[...] = jnp.zeros(acc_ref.shape, F32)

    def process(blocks, masked):
        zts = [_dot(kb, qtm_ref[s]) for s, (kb, _) in enumerate(blocks)]
        ws = []
        for s, zt in enumerate(zts):
            for h in range(B_HEADS):
                ls = slice(h * kb_rows, (h + 1) * kb_rows)
                half_t = 0.5 * jnp.tanh(zt[:, ls])
                beta = 0.5 + half_t
                omb = 0.5 - half_t
                if masked:
                    beta = jnp.where(causal, beta, 0.0)
                    omb = jnp.where(causal, omb, 1.0)
                run = jnp.ones((SUBLANES, kb_rows), F32)
                excl = [None] * run_len
                for jj in reversed(range(run_len)):
                    excl[jj] = run
                    run = run * omb[jj * SUBLANES:(jj + 1) * SUBLANES]
                inc = run
                for d in (1, 2, 4):
                    inc = jnp.where(sub + d < SUBLANES, inc * pltpu.roll(inc, SUBLANES - d, 0), inc)
                carry = carry_ref[s, :, ls]
                off = jnp.where(sub < SUBLANES - 1, pltpu.roll(inc, SUBLANES - 1, 0), 1.0) * carry
                ws.append(jnp.concatenate([beta[jj * SUBLANES:(jj + 1) * SUBLANES] * (excl[jj] * off)
                                           for jj in range(run_len)], axis=0).astype(BF16))
                carry_ref[s, :, ls] = carry * jnp.broadcast_to(inc[0:1, :], (SUBLANES, kb_rows))
        for s, (_, vtb) in enumerate(blocks):
            for h in range(B_HEADS):
                hs = slice(h * B_HEAD_DIM, (h + 1) * B_HEAD_DIM)
                acc_ref[s, hs, :] += _dot(vtb[hs, :], ws[s * B_HEADS + h])

    def alive():
        return (jnp.max(carry_ref[...]) > 0.0).astype(jnp.int32)

    process([(kc_ref[s, j], vtc_ref[s, j]) for s in streams], True)

    def cond(state):
        kb, live = state
        return jnp.logical_and(kb >= 0, live > 0)

    def body(state):
        kb, _ = state
        process([(kc_ref[s, kb], vtc_ref[s, kb]) for s in streams], False)
        return kb - 1, alive()

    lax.while_loop(cond, body, (j - 1, alive()))

    for kb in reversed(range(n_past)):
        @pl.when(alive() > 0)
        def _():
            cols = slice(kb * kb_rows, (kb + 1) * kb_rows)
            for s in streams:
                kn = pk_ref[0, s, :, cols].T
                vn = pv_ref[0, s, :, cols].T
                for c in range(n_halves):
                    stk_ref[s, c] = kn[:, c * LANES:(c + 1) * LANES]
                    stv_ref[s, c] = vn[:, c * LANES:(c + 1) * LANES]
            process([(permuted(stk_ref, s).astype(BF16), permuted(stv_ref, s).T.astype(BF16)) for s in streams],
                    False)

    for s in streams:
        o = acc_ref[s].T[0:lq, :]
        o_ref[s] = (o * _silu(g_ref[s].astype(F32))).astype(BF16)


def _sb(q, k, v, g, past_t, layer):
    b, l, w = q.shape
    lq = min(KEY_BLOCK, l)
    nblk = l // lq
    ns = min(b, SB_SEQS)
    while b % ns:
        ns -= 1
    p = 0 if past_t is None else past_t[0].shape[3]
    blk = pl.BlockSpec((ns, lq, w), lambda i, j: (i, j, 0))
    in_specs = [blk] * 4
    args = [q, k, v, g]
    if p:
        in_specs += [pl.BlockSpec((1, ns, w, p), lambda i, j: (layer, i, 0, 0))] * 2
        args += list(past_t)
    staging = pltpu.VMEM((ns, w // LANES, KEY_BLOCK, LANES), F32)
    return pl.pallas_call(
        functools.partial(_sb_kernel, ns=ns, lq=lq, n_past=p // KEY_BLOCK),
        grid=(b // ns, nblk),
        in_specs=in_specs,
        out_specs=blk,
        out_shape=jax.ShapeDtypeStruct((b, l, w), BF16),
        scratch_shapes=[pltpu.VMEM((ns, nblk, KEY_BLOCK, w), BF16),
                        pltpu.VMEM((ns, nblk, w, KEY_BLOCK), BF16),
                        staging, staging, staging,
                        pltpu.VMEM((ns, w, B_HEADS * KEY_BLOCK), BF16),
                        pltpu.VMEM((ns, SUBLANES, B_HEADS * KEY_BLOCK), F32),
                        pltpu.VMEM((ns, w, KEY_BLOCK), F32)],
        compiler_params=_params(2),
        name="sb_attn",
    )(*args)


def _post_kernel(x_ref, a_ref, b_ref, pc_ref, pprev_ref, mk_hbm, mv_hbm, wpool_ref, pscale_ref,
                 woa_ref, wob_ref, woc_ref, npost_ref, nxpre_ref, wxq_ref, wxo_ref, nxpost_ref,
                 xo_ref, pnew_ref, ext_ref, sum_ref, kv_ref, mkb_ref, mvb_ref, sem_ref,
                 *, nb, tl, n_split, past, layer):
    i = pl.program_id(0)
    j = pl.program_id(1)
    n_groups = pl.num_programs(0)
    halo = POOL_STATE + 1
    lead = SUBLANES
    dh = kv_ref.shape[-1]

    def kv_copies(group, slot):
        return [pltpu.make_async_copy(src.at[layer, group * nb + n, :, h, :], kv_ref.at[slot, t, n, h],
                                      sem_ref.at[slot, t])
                for t, src in enumerate((mk_hbm, mv_hbm)) for n in range(nb) for h in range(X_HEADS)]

    @pl.when(j == 0)
    def _():
        slot = i % 2

        @pl.when(i == 0)
        def _():
            for cp in kv_copies(0, 0):
                cp.start()

        @pl.when(i + 1 < n_groups)
        def _():
            for cp in kv_copies(i + 1, 1 - slot):
                cp.start()

        for cp in kv_copies(i, slot):
            cp.wait()
        for n in range(nb):
            ext_ref[n, 0:lead, :] = jnp.zeros((lead, C_WIDTH), F32)
            ext_ref[n, lead:lead + halo, :] = pprev_ref[n]
            for gi in range(len(POOL_WINDOWS) - 1):
                sum_ref[gi, n, 0:lead, :] = jnp.zeros((lead, C_WIDTH), F32)
            for h in range(X_HEADS):
                mkb_ref[n, :, h * dh:(h + 1) * dh] = kv_ref[slot, 0, n, h].astype(BF16)
                mvb_ref[n, :, h * dh:(h + 1) * dh] = kv_ref[slot, 1, n, h].astype(BF16)

    grp = lax.broadcasted_iota(jnp.int32, (tl, C_WIDTH), 1) // C_GROUP_DIM
    pos = past + j * tl + lax.broadcasted_iota(jnp.int32, (tl, C_WIDTH), 0)
    dds = []
    rows = halo + tl
    for n in range(nb):
        cu = pc_ref[n, :, 0:C_WIDTH]
        ext_ref[n, lead + halo:lead + rows, :] = cu
        win = jnp.zeros((tl, C_WIDTH), F32)
        wlen = jnp.zeros((tl, C_WIDTH), jnp.int32)
        src = ext_ref.at[n]
        for gi, w in enumerate(POOL_WINDOWS):
            assert w == 2 ** (gi + 1)
            wsum = src[lead:lead + rows, :] + src[lead - w // 2:lead - w // 2 + rows, :]
            if gi + 1 < len(POOL_WINDOWS):
                sum_ref[gi, n, lead:lead + rows, :] = wsum
                src = sum_ref.at[gi, n]
            win = jnp.where(grp == gi, wsum[halo:, :], win)
            wlen = jnp.where(grp == gi, w, wlen)
        cnt = jnp.minimum(pos + 1, wlen).astype(F32)
        dds.append((win / cnt - cu).astype(BF16))
        pnew_ref[n] = ext_ref[n, lead + tl + 1:lead + tl + halo, :]
        ext_ref[n, lead:lead + halo, :] = ext_ref[n, lead + tl:lead + tl + halo, :]

    seg = tl // n_split if nb == 1 else tl
    groups = [[(0, g * seg)] for g in range(n_split)] if nb == 1 else \
             [[(n, 0) for n in range(g * nb // n_split, (g + 1) * nb // n_split)] for g in range(n_split)]

    def rows(members, ref, w0=0, w1=None):
        return jnp.concatenate([ref[n, r0:r0 + seg, w0:w1] for n, r0 in members], axis=0)

    mixes = []
    for members in groups:
        dd = jnp.concatenate([dds[n][r0:r0 + seg] for n, r0 in members], axis=0)
        y = _dot(dd, wpool_ref[...]) * pscale_ref[...]
        c_out = (y * _silu(rows(members, pc_ref, C_WIDTH, 2 * C_WIDTH))).astype(BF16)
        mixes.append(_dot(rows(members, a_ref), woa_ref[...]) + _dot(rows(members, b_ref), wob_ref[...])
                     + _dot(c_out, woc_ref[...]))
    x1s = [rows(members, x_ref) + _rms(mix, npost_ref[...]) for members, mix in zip(groups, mixes)]

    qs = [_dot(_rms(x1, nxpre_ref[...]).astype(BF16), wxq_ref[...]).astype(BF16) for x1 in x1s]
    pairs = [(g, m, n, h) for g, members in enumerate(groups) for m, (n, _) in enumerate(members)
             for h in range(X_HEADS)]
    scores = [lax.dot_general(qs[g][m * seg:(m + 1) * seg, h * dh:(h + 1) * dh],
                              mkb_ref[n, :, h * dh:(h + 1) * dh], NT_DIMS,
                              preferred_element_type=F32) * (dh ** -0.5) for g, m, n, h in pairs]
    probs = []
    for s in scores:
        ex = jnp.exp(s - jnp.max(s, axis=-1, keepdims=True))
        probs.append((ex / jnp.sum(ex, axis=-1, keepdims=True)).astype(BF16))
    ctx = [_dot(p, mvb_ref[n, :, h * dh:(h + 1) * dh]).astype(BF16) for p, (g, m, n, h) in zip(probs, pairs)]
    xos = []
    for g, members in enumerate(groups):
        seqs = [jnp.concatenate([c for c, (g2, m2, _, _) in zip(ctx, pairs) if g2 == g and m2 == m], axis=1)
                for m in range(len(members))]
        xos.append(_dot(jnp.concatenate(seqs, axis=0), wxo_ref[...]))
    for members, x1, xo in zip(groups, x1s, xos):
        res = x1 + _rms(xo, nxpost_ref[...])
        for m, (n, r0) in enumerate(members):
            xo_ref[n, r0:r0 + seg, :] = res[m * seg:(m + 1) * seg]


def _post(x, a_out, b_out, pc, pool_prev, mk, mv, wts, past, layer):
    b, l, d = x.shape
    n_mem, dh = mk.shape[2], mk.shape[4]
    tl = POST_ROWS if l % POST_ROWS == 0 else CHUNK
    nb = max(1, min(b, POST_ROWS // tl, POST_SEQS))
    while b % nb:
        nb -= 1
    n_split = 2 if (nb * tl) % (2 * LANES) == 0 and (nb == 1 or nb % 2 == 0) else 1
    halo = POOL_STATE + 1
    pprev = jnp.concatenate([jnp.zeros((b, 1, C_WIDTH), F32), pool_prev], axis=1)
    tile = lambda w: pl.BlockSpec((nb, tl, w), lambda i, j: (i, j, 0))
    per_seq = lambda r, w: pl.BlockSpec((nb, r, w), lambda i, j: (i, 0, 0))
    full = lambda a: pl.BlockSpec(a.shape, lambda i, j: (0,) * a.ndim)
    hbm = pl.BlockSpec(memory_space=pl.ANY)
    return pl.pallas_call(
        functools.partial(_post_kernel, nb=nb, tl=tl, n_split=n_split, past=past, layer=layer),
        grid=(b // nb, l // tl),
        in_specs=[tile(d), tile(A_WIDTH), tile(B_WIDTH), tile(2 * C_WIDTH), per_seq(halo, C_WIDTH),
                  hbm, hbm] + [full(a) for a in wts],
        out_specs=[tile(d), per_seq(POOL_STATE, C_WIDTH)],
        out_shape=[jax.ShapeDtypeStruct((b, l, d), F32),
                   jax.ShapeDtypeStruct((b, POOL_STATE, C_WIDTH), F32)],
        scratch_shapes=[pltpu.VMEM((nb, SUBLANES + halo + tl, C_WIDTH), F32),
                        pltpu.VMEM((len(POOL_WINDOWS) - 1, nb, SUBLANES + halo + tl, C_WIDTH), F32),
                        pltpu.VMEM((2, 2, nb, X_HEADS, n_mem, dh), F32),
                        pltpu.VMEM((nb, n_mem, d), BF16),
                        pltpu.VMEM((nb, n_mem, d), BF16),
                        pltpu.SemaphoreType.DMA((2, 2))],
        compiler_params=_params(2),
        name="mix_out",
    )(x, a_out, b_out, pc, pprev, mk, mv, *wts)


def _layer(x, mk, mv, past_t, s0, s0_layer, pool_prev, lw, layer, depth, chain):
    kv_t, s_all = chain
    b, l, d = x.shape
    n = b * l
    proj, kv_t = _proj(x.reshape(n, d), lw["n_pre"], lw["w_in"], lw["lb"], l, layer, depth, kv_t)
    seq = lambda t: t.reshape(b, l, -1)
    gates, (bq, bk, bv, bg, pc) = [seq(t) for t in proj[:5]], proj[5:]
    a_out, s_all = _hgrn(gates, lw["onorm_g"], s0, s0_layer, layer, depth, s_all)
    b_out = _sb(seq(bq), seq(bk), seq(bv), seq(bg), past_t, layer)
    past = 0 if past_t is None else past_t[0].shape[3]
    x_new, pool_new = _post(x, a_out, b_out, seq(pc), pool_prev, mk, mv, lw["post"], past, layer)
    heads = lambda t: t.reshape(b, l, B_HEADS, B_HEAD_DIM)
    return x_new, heads(bk), heads(bv), (kv_t, s_all), pool_new


def kernel(x_prompt, x_sample, mem_prompt, cache_sb_k, cache_sb_v, state_hgrn, state_pool, cache_mem_k, cache_mem_v, norm_mix_pre, norm_mix_post, w_in, hgrn_lb_logits, hgrn_onorm_g, w_pool, pool_scale, w_out, norm_x_pre, norm_x_post, norm_mem, w_xq, w_xk, w_xv, w_xo):
    depth, d = norm_mix_pre.shape
    bp = x_prompt.shape[0]
    bs, past = cache_sb_k.shape[1], cache_sb_k.shape[2]
    n_mem = mem_prompt.shape[1]

    lbs = jax.nn.softmax(hgrn_lb_logits.astype(F32), axis=0)
    lower_bounds = jnp.maximum(jnp.cumsum(lbs, axis=0) - lbs[0], 0.0)

    mk_all, mv_all = _memkv(mem_prompt, norm_mem, w_xk.astype(BF16), w_xv.astype(BF16))

    def seq_minor(c):
        return jnp.transpose(c, (0, 1, 3, 4, 2)).reshape(depth, bs, B_WIDTH, past)

    def seq_major(t):
        dp, b, _, l = t.shape
        return jnp.transpose(t.reshape(dp, b, B_HEADS, B_HEAD_DIM, l), (0, 1, 4, 2, 3))

    past_t = (seq_minor(cache_sb_k), seq_minor(cache_sb_v))

    row = lambda v: v.reshape(1, -1)
    s_zero = jnp.zeros((1, bp, A_HEADS, A_HEAD_DIM, A_HEAD_DIM), F32)
    pool_zero = jnp.zeros((bp, POOL_STATE, C_WIDTH), F32)

    xp, xs = x_prompt, x_sample
    p_kn, p_vn, p_pools, s_kn, s_vn, s_pools = [], [], [], [], [], []
    chain_p = chain_s = (None, None)
    for l in range(depth):
        wp = jnp.zeros((C_WIDTH, C_WIDTH), F32)
        for gi in range(C_GROUPS):
            gs = slice(gi * C_GROUP_DIM, (gi + 1) * C_GROUP_DIM)
            wp = wp.at[gs, gs].set(w_pool[l, gi])
        wo = w_out[l].astype(BF16)
        lw = {
            "n_pre": norm_mix_pre[l], "w_in": w_in[l].astype(BF16), "lb": lower_bounds[l],
            "onorm_g": hgrn_onorm_g[l],
            "post": (wp.astype(BF16), row(pool_scale[l]), wo[0:A_WIDTH], wo[A_WIDTH:A_WIDTH + B_WIDTH],
                     wo[A_WIDTH + B_WIDTH:], row(norm_mix_post[l]), row(norm_x_pre[l]),
                     w_xq[l].astype(BF16), w_xo[l].astype(BF16), row(norm_x_post[l])),
        }
        xp, kb, vb, chain_p, pn = _layer(xp, mk_all, mv_all, None, s_zero, 0, pool_zero, lw, l, depth, chain_p)
        p_kn.append(kb); p_vn.append(vb); p_pools.append(pn)
        xs, kb, vb, chain_s, pn = _layer(xs, cache_mem_k, cache_mem_v, past_t, state_hgrn, l, state_pool[l],
                                         lw, l, depth, chain_s)
        s_kn.append(kb); s_vn.append(vb); s_pools.append(pn)

    def new_kv(kvt, nat_k, nat_v):
        if kvt is not None:
            return seq_major(kvt[0]), seq_major(kvt[1])
        return jnp.stack(nat_k), jnp.stack(nat_v)

    p_k, p_v = new_kv(chain_p[0], p_kn, p_vn)
    s_k, s_v = new_kv(chain_s[0], s_kn, s_vn)
    return (xp, xs, p_k, p_v, chain_p[1], jnp.stack(p_pools), mk_all, mv_all, s_k, s_v, chain_s[1],
            jnp.stack(s_pools))
```

```python
import functools

import jax
import jax.numpy as jnp
from jax import lax
from jax.experimental import pallas as pl
from jax.experimental.pallas import tpu as pltpu

F32 = jnp.float32
BF16 = jnp.bfloat16
EPS = 1e-6

A_HEADS, A_HEAD_DIM = 4, 128
A_WIDTH = A_HEADS * A_HEAD_DIM
B_HEADS, B_HEAD_DIM = 4, 64
B_WIDTH = B_HEADS * B_HEAD_DIM
C_GROUPS, C_GROUP_DIM = 4, 64
C_WIDTH = C_GROUPS * C_GROUP_DIM
POOL_WINDOWS = (2, 4, 8, 16)
POOL_STATE = 15
X_HEADS = 4
CHUNK = 64
SUB = 16
LOG2_E = 1.4426950408889634
SAFE_SPAN = 40.0
KEY_BLOCK = 128
SB_SEQS = 4
HGRN_ROWS = 512
POST_ROWS = 512
POST_SEQS = 4
SUBLANES = 8
LANES = 128
VMEM_LIMIT = 56 * 1024 * 1024

NT_DIMS = (((1,), (1,)), ((), ()))
TN_DIMS = (((0,), (0,)), ((), ()))


def _rms(x, g):
    ms = jnp.mean(x * x, axis=-1, keepdims=True)
    return x * lax.rsqrt(ms + EPS) * g


def _silu(x):
    return x * (1.0 / (1.0 + jnp.exp(-x)))


def _dot(a, b):
    return jnp.dot(a, b, preferred_element_type=F32)


def _params(n_grid, flags=None):
    return pltpu.CompilerParams(dimension_semantics=("arbitrary",) * n_grid,
                                vmem_limit_bytes=VMEM_LIMIT, flags=flags)


def _memkv_kernel(mem_ref, g_ref, wk_ref, wv_ref, k_ref, v_ref):
    nb, n_mem, d = mem_ref.shape
    dh = d // X_HEADS
    for b in range(nb):
        m = _rms(mem_ref[b], g_ref[0]).astype(BF16)
        k = _dot(m, wk_ref[0])
        v = _dot(m, wv_ref[0])
        for h in range(X_HEADS):
            k_ref[0, b, :, h, :] = k[:, h * dh:(h + 1) * dh]
            v_ref[0, b, :, h, :] = v[:, h * dh:(h + 1) * dh]


def _memkv(mem, norm_mem, wk, wv):
    depth, d = norm_mem.shape
    nb, n_mem, _ = mem.shape
    dh = d // X_HEADS
    out = jax.ShapeDtypeStruct((depth, nb, n_mem, X_HEADS, dh), F32)
    return pl.pallas_call(
        _memkv_kernel,
        grid=(depth,),
        in_specs=[pl.BlockSpec((nb, n_mem, d), lambda l: (0, 0, 0)),
                  pl.BlockSpec((1, 1, d), lambda l: (l, 0, 0)),
                  pl.BlockSpec((1, d, d), lambda l: (l, 0, 0)),
                  pl.BlockSpec((1, d, d), lambda l: (l, 0, 0))],
        out_specs=[pl.BlockSpec((1, nb, n_mem, X_HEADS, dh), lambda l: (l, 0, 0, 0, 0))] * 2,
        out_shape=[out, out],
        compiler_params=_params(1),
        name="mem_kv",
    )(mem, norm_mem.reshape(depth, 1, d), wk, wv)


N_PROJ_IN = 7
PROJ_GROUPS = 4
N_PROJ_OUT = 10


def _proj_kernel(*refs, transposed, aliased):
    x0_ref, xn_ref, g_ref, w_ref, loglb_ref, log1mlb_ref, omlb_ref = refs[:N_PROJ_IN]
    outs = refs[N_PROJ_IN + 2 * aliased:]
    gcum_ref, qa_ref, ka_ref, ai_ref, ag_ref, q_ref, k_ref, v_ref, bg_ref, pc_ref = outs[:N_PROJ_OUT]
    h_ref = outs[-1]
    i = pl.program_id(0)
    a_end = 4 * A_WIDTH
    b_end = a_end + 4 * B_WIDTH
    tm = xn_ref.shape[0]
    n_groups = PROJ_GROUPS if tm % (PROJ_GROUPS * LANES) == 0 else 1
    groups = [(g * tm // n_groups, tm // n_groups) for g in range(n_groups)]
    sub = lax.broadcasted_iota(jnp.int32, (SUBLANES, A_WIDTH), 0)

    @pl.when(i == 0)
    def _():
        h_ref[...] = _rms(x0_ref[...], g_ref[...]).astype(BF16)

    def step():
        h_next = _rms(xn_ref[...], g_ref[...]).astype(BF16)
        hs = [h_ref[r0:r0 + nr, :] for r0, nr in groups]
        qfs = [_dot(h, w_ref[:, 0:2 * A_WIDTH]) for h in hs]
        for (r0, nr), h, qf in zip(groups, hs, qfs):
            rows = slice(r0, r0 + nr)
            aq, af = qf[:, 0:A_WIDTH], qf[:, A_WIDTH:2 * A_WIDTH]
            e = jnp.exp(-jnp.abs(af))
            ope = 1.0 + e
            logsig = jnp.minimum(af, 0.0) - jnp.log(ope)
            r = 1.0 / ope
            sig_neg = jnp.where(af >= 0, e * r, r)
            la = loglb_ref[...]
            lb_ = log1mlb_ref[...] + logsig
            log_f = jnp.maximum(la, lb_) + jnp.log(1.0 + jnp.exp(-jnp.abs(la - lb_)))
            ka_ref[rows, :] = omlb_ref[...] * sig_neg
            qa_ref[rows, :] = _silu(aq) * (A_HEAD_DIM ** -0.5)

            ig = _dot(h, w_ref[:, 2 * A_WIDTH:a_end])
            ai_ref[rows, :] = ig[:, 0:A_WIDTH].astype(BF16)
            ag_ref[rows, :] = ig[:, A_WIDTH:2 * A_WIDTH]
            pb = _dot(h, w_ref[:, a_end:b_end])
            q_ref[rows, :] = pb[:, 0:B_WIDTH]
            k_ref[rows, :] = pb[:, B_WIDTH:2 * B_WIDTH]
            v_ref[rows, :] = pb[:, 2 * B_WIDTH:3 * B_WIDTH]
            bg_ref[rows, :] = pb[:, 3 * B_WIDTH:4 * B_WIDTH]
            pc_ref[rows, :] = _dot(h, w_ref[:, b_end:])
            if transposed:
                kt_ref, vt_ref = outs[N_PROJ_OUT:N_PROJ_OUT + 2]
                kt_ref[0, 0, :, rows] = pb[:, B_WIDTH:2 * B_WIDTH].T
                vt_ref[0, 0, :, rows] = pb[:, 2 * B_WIDTH:3 * B_WIDTH].T
            for c in range(nr // CHUNK):
                carry = None
                for jv in range(CHUNK // SUBLANES):
                    lo = c * CHUNK + jv * SUBLANES
                    s = log_f[lo:lo + SUBLANES, :]
                    for dist in (1, 2, 4):
                        s = s + jnp.where(sub >= dist, pltpu.roll(s, dist, 0), 0.0)
                    if carry is not None:
                        s = s + carry
                    carry = jnp.broadcast_to(s[SUBLANES - 1:SUBLANES, :], s.shape)
                    gcum_ref[r0 + lo:r0 + lo + SUBLANES, :] = s * LOG2_E

        h_ref[...] = h_next

    step()


def _proj(x2d, g, w_in, lb, seq_len, layer, depth, kv_t):
    n, d = x2d.shape
    d_in = w_in.shape[1]
    tm = 512 if n % 512 == 0 else CHUNK
    transposed = seq_len % tm == 0 and tm % LANES == 0
    aliased = transposed and kv_t is not None
    row = lambda width: pl.BlockSpec((tm, width), lambda i: (i, 0))
    vec = pl.BlockSpec((1, A_WIDTH), lambda i: (0, 0))
    shapes = [A_WIDTH] * 5 + [B_WIDTH] * 4 + [2 * C_WIDTH]
    dtypes = [F32, F32, F32, BF16, F32] + [F32] * 5
    lb = lb.reshape(1, A_WIDTH)
    last = n // tm - 1
    in_specs = [pl.BlockSpec((tm, d), lambda i: (0, 0)),
                pl.BlockSpec((tm, d), lambda i: (jnp.minimum(i + 1, last), 0)),
                pl.BlockSpec((1, d), lambda i: (0, 0)), pl.BlockSpec((d, d_in), lambda i: (0, 0)),
                vec, vec, vec]
    args = [x2d, x2d, g.reshape(1, d), w_in, jnp.log(lb), jnp.log1p(-lb), 1.0 - lb]
    out_specs = [row(w) for w in shapes]
    out_shape = [jax.ShapeDtypeStruct((n, w), t) for w, t in zip(shapes, dtypes)]
    aliases = {}
    if transposed:
        tiles = seq_len // tm
        t_spec = pl.BlockSpec((1, 1, B_WIDTH, tm), lambda i: (layer, i // tiles, 0, i % tiles))
        out_specs += [t_spec, t_spec]
        out_shape += [jax.ShapeDtypeStruct((depth, n // seq_len, B_WIDTH, seq_len), F32)] * 2
        if aliased:
            in_specs += [pl.BlockSpec(memory_space=pl.ANY)] * 2
            args += list(kv_t)
            aliases = {N_PROJ_IN: N_PROJ_OUT, N_PROJ_IN + 1: N_PROJ_OUT + 1}
    res = pl.pallas_call(
        functools.partial(_proj_kernel, transposed=transposed, aliased=aliased),
        grid=(n // tm,),
        in_specs=in_specs,
        out_specs=out_specs,
        out_shape=out_shape,
        scratch_shapes=[pltpu.VMEM((tm, d), BF16)],
        input_output_aliases=aliases,
        compiler_params=_params(1),
        name="in_proj",
    )(*args)
    return res[:N_PROJ_OUT], (tuple(res[N_PROJ_OUT:]) if transposed else None)


def _hgrn_kernel(*refs, nb, nch, aliased):
    g_ref, q_ref, k_ref, v_ref, gate_ref, og_ref, s0_ref = refs[:7]
    a_ref, sout_ref, st_ref = refs[7 + aliased:]
    j = pl.program_id(1)

    @pl.when(j == 0)
    def _():
        for n in range(nb):
            for h in range(A_HEADS):
                st_ref[n, h] = s0_ref[0, n, h].T

    rowc = lax.broadcasted_iota(jnp.int32, (CHUNK, CHUNK), 0)
    colc = lax.broadcasted_iota(jnp.int32, (CHUNK, CHUNK), 1)
    col_sub = colc // SUB
    row_k = lax.broadcasted_iota(jnp.int32, (CHUNK, A_HEAD_DIM), 0)
    lane_d = lax.broadcasted_iota(jnp.int32, (SUB, CHUNK), 1)
    n_sub = CHUNK // SUB
    head = [slice(h * A_HEAD_DIM, (h + 1) * A_HEAD_DIM) for h in range(A_HEADS)]

    def sub_starts(g_all):
        return [jnp.zeros((1, A_WIDTH), F32)] + [g_all[SUB * s - 1:SUB * s, :] for s in range(1, n_sub)]

    span = jnp.zeros((1, A_WIDTH), F32)
    for n in range(nb):
        for c in range(nch):
            for s in range(n_sub):
                last = c * CHUNK + SUB * s + SUB - 1
                end = g_ref[n, last:last + 1, :]
                span = jnp.maximum(span, -end if s == 0 else g_ref[n, last - SUB:last - SUB + 1, :] - end)
    small_decay = jnp.max(span) < SAFE_SPAN * LOG2_E

    def att_factored(g_all, q_all, k_all, qg_all):
        starts = sub_starts(g_all)
        operands = []
        for sl in head:
            gh, qh, kh = g_all[:, sl], q_all[:, sl], k_all[:, sl]
            parts, krows = [qg_all[:, sl]], []
            for s in range(n_sub):
                rs = slice(SUB * s, SUB * (s + 1))
                st = starts[s][:, sl]
                krows.append(kh[rs] * jnp.exp2(st - gh[rs]))
                if s:
                    parts.append(qh[SUB * s:, :] * jnp.exp2(gh[SUB * s:, :] - st))
            operands.append((jnp.concatenate(parts, axis=0).astype(BF16),
                             jnp.concatenate(krows, axis=0).astype(BF16)))
        rrs = [lax.dot_general(qr, kr, NT_DIMS, preferred_element_type=F32) for qr, kr in operands]
        atts = []
        for rr in rrs:
            att = rr[0:CHUNK, :]
            r0 = CHUNK
            for s in range(1, n_sub):
                rows = CHUNK - SUB * s
                block = jnp.concatenate([jnp.zeros((SUB * s, CHUNK), F32), rr[r0:r0 + rows, :]], axis=0)
                att = jnp.where(col_sub == s, block, att)
                r0 += rows
            atts.append(jnp.where(rowc >= colc, att, 0.0))
        return atts

    def att_exact(g_all, q_all, k_all, qg_all):
        def one_head(sl):
            gh, qh, kh = g_all[:, sl], q_all[:, sl], k_all[:, sl]
            parts, refrows = [], []
            for jsub in range(n_sub):
                ref = gh[SUB * jsub + SUB - 1:SUB * jsub + SUB, :]
                refrows.append(jnp.broadcast_to(ref, (SUB, A_HEAD_DIM)))
                if jsub < n_sub - 1:
                    later = row_k >= SUB * (jsub + 1)
                    parts.append(jnp.where(later, qh * jnp.exp2(jnp.minimum(gh - ref, 0.0)), 0.0))
            q_rel = jnp.concatenate(parts, axis=0).astype(BF16)
            k_rel = (kh * jnp.exp2(jnp.minimum(jnp.concatenate(refrows, axis=0) - gh, 0.0))).astype(BF16)
            rr = lax.dot_general(q_rel, k_rel, NT_DIMS, preferred_element_type=F32)
            att = jnp.zeros((CHUNK, CHUNK), F32)
            for jsub in range(n_sub - 1):
                att = jnp.where(col_sub == jsub, rr[jsub * CHUNK:(jsub + 1) * CHUNK, :], att)

            dparts = []
            for csub in range(n_sub):
                rs = slice(SUB * csub, SUB * (csub + 1))
                gc, qc, kc = gh[rs], qh[rs], kh[rs]
                accd = jnp.zeros((SUB, CHUNK), F32)
                for s in range(SUB):
                    dec = jnp.exp2(jnp.minimum(gc - gc[s:s + 1, :], 0.0))
                    col = jnp.sum(dec * qc * kc[s:s + 1, :], axis=-1, keepdims=True)
                    accd = jnp.where(lane_d == SUB * csub + s, col, accd)
                dparts.append(accd)
            return att + jnp.where(rowc >= colc, jnp.concatenate(dparts, axis=0), 0.0)

        return [one_head(sl) for sl in head]

    def chunk(n, r0, sts, att_fn):
        g_all = g_ref[n, pl.ds(r0, CHUNK), :]
        q_all = q_ref[n, pl.ds(r0, CHUNK), :]
        k_all = k_ref[n, pl.ds(r0, CHUNK), :]
        v_all = v_ref[n, pl.ds(r0, CHUNK), :]
        ag = gate_ref[n, pl.ds(r0, CHUNK), :]
        qg_all = q_all * jnp.exp2(g_all)
        gl_all = g_all[CHUNK - 1:CHUNK, :]
        kd_all = (k_all * jnp.exp2(gl_all - g_all)).astype(BF16)
        qg_bf = qg_all.astype(BF16)
        atts = att_fn(g_all, q_all, k_all, qg_all)

        o_state = [lax.dot_general(qg_bf[:, sl], st.astype(BF16), NT_DIMS, preferred_element_type=F32)
                   for sl, st in zip(head, sts)]
        o_pairs = [_dot(att.astype(BF16), v_all[:, sl]) for sl, att in zip(head, atts)]
        updates = [lax.dot_general(v_all[:, sl], kd_all[:, sl], TN_DIMS, preferred_element_type=F32)
                   for sl in head]
        new_sts = [st * jnp.exp2(gl_all[:, sl]) + up for sl, st, up in zip(head, sts, updates)]
        o = jnp.concatenate([a + b for a, b in zip(o_state, o_pairs)], axis=1)
        sq = o * o
        inv = jnp.concatenate(
            [jnp.broadcast_to(lax.rsqrt(jnp.mean(sq[:, sl], axis=-1, keepdims=True) + EPS), (CHUNK, A_HEAD_DIM))
             for sl in head], axis=1)
        a_ref[n, pl.ds(r0, CHUNK), :] = (o * inv * og_ref[...] * _silu(ag)).astype(BF16)
        return new_sts

    @pl.when(small_decay)
    def _():
        for n in range(nb):
            sts = [st_ref[n, h] for h in range(A_HEADS)]
            for c in range(nch):
                sts = chunk(n, c * CHUNK, sts, att_factored)
            for h in range(A_HEADS):
                st_ref[n, h] = sts[h]

    @pl.when(jnp.logical_not(small_decay))
    def _():
        for n in range(nb):
            def body(c, carry):
                sts = chunk(n, pl.multiple_of(c * CHUNK, CHUNK), [st_ref[n, h] for h in range(A_HEADS)], att_exact)
                for h in range(A_HEADS):
                    st_ref[n, h] = sts[h]
                return carry
            lax.fori_loop(0, nch, body, 0)

    @pl.when(j == pl.num_programs(1) - 1)
    def _():
        for n in range(nb):
            for h in range(A_HEADS):
                sout_ref[0, n, h] = st_ref[n, h].T


def _hgrn(gates, onorm_g, s0, s0_layer, layer, depth, s_all):
    b, l, _ = gates[0].shape
    tl = HGRN_ROWS if l % HGRN_ROWS == 0 else CHUNK
    nb = max(1, min(b, HGRN_ROWS // tl))
    while b % nb:
        nb -= 1
    tile = pl.BlockSpec((nb, tl, A_WIDTH), lambda i, j: (i, j, 0))
    st_block = (1, nb, A_HEADS, A_HEAD_DIM, A_HEAD_DIM)
    aliased = s_all is not None
    in_specs = [tile] * 5 + [pl.BlockSpec((1, A_WIDTH), lambda i, j: (0, 0)),
                             pl.BlockSpec(st_block, lambda i, j: (s0_layer, i, 0, 0, 0))]
    args = list(gates) + [onorm_g.reshape(1, A_WIDTH), s0]
    if aliased:
        in_specs.append(pl.BlockSpec(memory_space=pl.ANY))
        args.append(s_all)
    return pl.pallas_call(
        functools.partial(_hgrn_kernel, nb=nb, nch=tl // CHUNK, aliased=aliased),
        grid=(b // nb, l // tl),
        in_specs=in_specs,
        out_specs=[pl.BlockSpec((nb, tl, A_WIDTH), lambda i, j: (i, j, 0)),
                   pl.BlockSpec(st_block, lambda i, j: (layer, i, 0, 0, 0))],
        out_shape=[jax.ShapeDtypeStruct((b, l, A_WIDTH), BF16),
                   jax.ShapeDtypeStruct((depth, b, A_HEADS, A_HEAD_DIM, A_HEAD_DIM), F32)],
        scratch_shapes=[pltpu.VMEM((nb, A_HEADS, A_HEAD_DIM, A_HEAD_DIM), F32)],
        input_output_aliases={7: 1} if aliased else {},
        compiler_params=_params(2),
        name="hgrn2",
    )(*args)


def _sb_kernel(*refs, ns, lq, n_past):
    if n_past:
        q_ref, k_ref, v_ref, g_ref, pk_ref, pv_ref, o_ref = refs[:7]
        scratch = refs[7:]
    else:
        q_ref, k_ref, v_ref, g_ref, o_ref = refs[:5]
        pk_ref = pv_ref = None
        scratch = refs[5:]
    kc_ref, vtc_ref, stk_ref, stv_ref, stq_ref, qtm_ref, carry_ref, acc_ref = scratch
    j = pl.program_id(1)
    kb_rows = KEY_BLOCK
    run_len = kb_rows // SUBLANES
    width = B_WIDTH
    n_halves = width // LANES
    streams = range(ns)

    def permuted(ref, s):
        return jnp.concatenate(
            [jnp.concatenate([ref[s, c, pl.ds(jj, SUBLANES, stride=run_len), :] for jj in range(run_len)], axis=0)
             for c in range(n_halves)], axis=1)

    def stage(dst_ref, s, rows):
        for c in range(n_halves):
            if rows.shape[0] < kb_rows:
                dst_ref[s, c, rows.shape[0]:kb_rows, :] = jnp.zeros((kb_rows - rows.shape[0], LANES), F32)
            dst_ref[s, c, 0:rows.shape[0], :] = rows[:, c * LANES:(c + 1) * LANES]

    row_head = lax.broadcasted_iota(jnp.int32, (width, kb_rows), 0) // B_HEAD_DIM
    for s in streams:
        stage(stk_ref, s, k_ref[s])
        stage(stv_ref, s, v_ref[s])
        stage(stq_ref, s, q_ref[s] * (0.5 * B_HEAD_DIM ** -0.5))
    for s in streams:
        kc_ref[s, j] = permuted(stk_ref, s).astype(BF16)
        vtc_ref[s, j] = permuted(stv_ref, s).T.astype(BF16)
        qt = jnp.concatenate([stq_ref[s, c] for c in range(n_halves)], axis=1).T
        qtm_ref[s] = jnp.concatenate([jnp.where(row_head == h, qt, 0.0) for h in range(B_HEADS)],
                                     axis=1).astype(BF16)

    row = lax.broadcasted_iota(jnp.int32, (kb_rows, kb_rows), 0)
    lane = lax.broadcasted_iota(jnp.int32, (kb_rows, kb_rows), 1)
    causal = (row % SUBLANES) * run_len + row // SUBLANES < lane
    sub = lax.broadcasted_iota(jnp.int32, (SUBLANES, kb_rows), 0)

    carry_ref[...] = jnp.ones(carry_ref.shape, F32)
    acc_ref[...] = jnp.zeros(acc_ref.shape, F32)

    def process(blocks, masked):
        zts = [_dot(kb, qtm_ref[s]) for s, (kb, _) in enumerate(blocks)]
        ws = []
        for s, zt in enumerate(zts):
            for h in range(B_HEADS):
                ls = slice(h * kb_rows, (h + 1) * kb_rows)
                half_t = 0.5 * jnp.tanh(zt[:, ls])
                beta = 0.5 + half_t
                omb = 0.5 - half_t
                if masked:
                    beta = jnp.where(causal, beta, 0.0)
                    omb = jnp.where(causal, omb, 1.0)
                run = jnp.ones((SUBLANES, kb_rows), F32)
                excl = [None] * run_len
                for jj in reversed(range(run_len)):
                    excl[jj] = run
                    run = run * omb[jj * SUBLANES:(jj + 1) * SUBLANES]
                inc = run
                for d in (1, 2, 4):
                    inc = jnp.where(sub + d < SUBLANES, inc * pltpu.roll(inc, SUBLANES - d, 0), inc)
                carry = carry_ref[s, :, ls]
                off = jnp.where(sub < SUBLANES - 1, pltpu.roll(inc, SUBLANES - 1, 0), 1.0) * carry
                ws.append(jnp.concatenate([beta[jj * SUBLANES:(jj + 1) * SUBLANES] * (excl[jj] * off)
                                           for jj in range(run_len)], axis=0).astype(BF16))
                carry_ref[s, :, ls] = carry * jnp.broadcast_to(inc[0:1, :], (SUBLANES, kb_rows))
        for s, (_, vtb) in enumerate(blocks):
            for h in range(B_HEADS):
                hs = slice(h * B_HEAD_DIM, (h + 1) * B_HEAD_DIM)
                acc_ref[s, hs, :] += _dot(vtb[hs, :], ws[s * B_HEADS + h])

    def alive():
        return (jnp.max(carry_ref[...]) > 0.0).astype(jnp.int32)

    process([(kc_ref[s, j], vtc_ref[s, j]) for s in streams], True)

    def cond(state):
        kb, live = state
        return jnp.logical_and(kb >= 0, live > 0)

    def body(state):
        kb, _ = state
        process([(kc_ref[s, kb], vtc_ref[s, kb]) for s in streams], False)
        return kb - 1, alive()

    lax.while_loop(cond, body, (j - 1, alive()))

    for kb in reversed(range(n_past)):
        @pl.when(alive() > 0)
        def _():
            cols = slice(kb * kb_rows, (kb + 1) * kb_rows)
            for s in streams:
                kn = pk_ref[0, s, :, cols].T
                vn = pv_ref[0, s, :, cols].T
                for c in range(n_halves):
                    stk_ref[s, c] = kn[:, c * LANES:(c + 1) * LANES]
                    stv_ref[s, c] = vn[:, c * LANES:(c + 1) * LANES]
            process([(permuted(stk_ref, s).astype(BF16), permuted(stv_ref, s).T.astype(BF16)) for s in streams],
                    False)

    for s in streams:
        o = acc_ref[s].T[0:lq, :]
        o_ref[s] = (o * _silu(g_ref[s])).astype(BF16)


def _sb(q, k, v, g, past_t, layer):
    b, l, w = q.shape
    lq = min(KEY_BLOCK, l)
    nblk = l // lq
    ns = min(b, SB_SEQS)
    while b % ns:
        ns -= 1
    p = 0 if past_t is None else past_t[0].shape[3]
    blk = pl.BlockSpec((ns, lq, w), lambda i, j: (i, j, 0))
    in_specs = [blk] * 4
    args = [q, k, v, g]
    if p:
        in_specs += [pl.BlockSpec((1, ns, w, p), lambda i, j: (layer, i, 0, 0))] * 2
        args += list(past_t)
    staging = pltpu.VMEM((ns, w // LANES, KEY_BLOCK, LANES), F32)
    return pl.pallas_call(
        functools.partial(_sb_kernel, ns=ns, lq=lq, n_past=p // KEY_BLOCK),
        grid=(b // ns, nblk),
        in_specs=in_specs,
        out_specs=blk,
        out_shape=jax.ShapeDtypeStruct((b, l, w), BF16),
        scratch_shapes=[pltpu.VMEM((ns, nblk, KEY_BLOCK, w), BF16),
                        pltpu.VMEM((ns, nblk, w, KEY_BLOCK), BF16),
                        staging, staging, staging,
                        pltpu.VMEM((ns, w, B_HEADS * KEY_BLOCK), BF16),
                        pltpu.VMEM((ns, SUBLANES, B_HEADS * KEY_BLOCK), F32),
                        pltpu.VMEM((ns, w, KEY_BLOCK), F32)],
        compiler_params=_params(2),
        name="sb_attn",
    )(*args)


def _post_kernel(x_ref, a_ref, b_ref, pc_ref, pprev_ref, mk_hbm, mv_hbm, wpool_ref, pscale_ref,
                 woa_ref, wob_ref, woc_ref, npost_ref, nxpre_ref, wxq_ref, wxo_ref, nxpost_ref,
                 xo_ref, pnew_ref, ext_ref, sum_ref, kv_ref, mkb_ref, mvb_ref, sem_ref,
                 *, nb, tl, n_split, past, layer):
    i = pl.program_id(0)
    j = pl.program_id(1)
    n_groups = pl.num_programs(0)
    halo = POOL_STATE + 1
    lead = SUBLANES
    dh = kv_ref.shape[-1]

    def kv_copies(group, slot):
        return [pltpu.make_async_copy(src.at[layer, group * nb + n, :, h, :], kv_ref.at[slot, t, n, h],
                                      sem_ref.at[slot, t])
                for t, src in enumerate((mk_hbm, mv_hbm)) for n in range(nb) for h in range(X_HEADS)]

    @pl.when(j == 0)
    def _():
        slot = i % 2

        @pl.when(i == 0)
        def _():
            for cp in kv_copies(0, 0):
                cp.start()

        @pl.when(i + 1 < n_groups)
        def _():
            for cp in kv_copies(i + 1, 1 - slot):
                cp.start()

        for cp in kv_copies(i, slot):
            cp.wait()
        for n in range(nb):
            ext_ref[n, 0:lead, :] = jnp.zeros((lead, C_WIDTH), F32)
            ext_ref[n, lead:lead + halo, :] = pprev_ref[n]
            for gi in range(len(POOL_WINDOWS) - 1):
                sum_ref[gi, n, 0:lead, :] = jnp.zeros((lead, C_WIDTH), F32)
            for h in range(X_HEADS):
                mkb_ref[n, :, h * dh:(h + 1) * dh] = kv_ref[slot, 0, n, h].astype(BF16)
                mvb_ref[n, :, h * dh:(h + 1) * dh] = kv_ref[slot, 1, n, h].astype(BF16)

    grp = lax.broadcasted_iota(jnp.int32, (tl, C_WIDTH), 1) // C_GROUP_DIM
    pos = past + j * tl + lax.broadcasted_iota(jnp.int32, (tl, C_WIDTH), 0)
    dds = []
    rows = halo + tl
    for n in range(nb):
        cu = pc_ref[n, :, 0:C_WIDTH]
        ext_ref[n, lead + halo:lead + rows, :] = cu
        win = jnp.zeros((tl, C_WIDTH), F32)
        wlen = jnp.zeros((tl, C_WIDTH), jnp.int32)
        src = ext_ref.at[n]
        for gi, w in enumerate(POOL_WINDOWS):
            assert w == 2 ** (gi + 1)
            wsum = src[lead:lead + rows, :] + src[lead - w // 2:lead - w // 2 + rows, :]
            if gi + 1 < len(POOL_WINDOWS):
                sum_ref[gi, n, lead:lead + rows, :] = wsum
                src = sum_ref.at[gi, n]
            win = jnp.where(grp == gi, wsum[halo:, :], win)
            wlen = jnp.where(grp == gi, w, wlen)
        cnt = jnp.minimum(pos + 1, wlen).astype(F32)
        dds.append((win / cnt - cu).astype(BF16))
        pnew_ref[n] = ext_ref[n, lead + tl + 1:lead + tl + halo, :]
        ext_ref[n, lead:lead + halo, :] = ext_ref[n, lead + tl:lead + tl + halo, :]

    seg = tl // n_split if nb == 1 else tl
    groups = [[(0, g * seg)] for g in range(n_split)] if nb == 1 else \
             [[(n, 0) for n in range(g * nb // n_split, (g + 1) * nb // n_split)] for g in range(n_split)]

    def rows(members, ref, w0=0, w1=None):
        return jnp.concatenate([ref[n, r0:r0 + seg, w0:w1] for n, r0 in members], axis=0)

    mixes = []
    for members in groups:
        dd = jnp.concatenate([dds[n][r0:r0 + seg] for n, r0 in members], axis=0)
        y = _dot(dd, wpool_ref[...]) * pscale_ref[...]
        c_out = (y * _silu(rows(members, pc_ref, C_WIDTH, 2 * C_WIDTH))).astype(BF16)
        mixes.append(_dot(rows(members, a_ref), woa_ref[...]) + _dot(rows(members, b_ref), wob_ref[...])
                     + _dot(c_out, woc_ref[...]))
    x1s = [rows(members, x_ref) + _rms(mix, npost_ref[...]) for members, mix in zip(groups, mixes)]

    qs = [_dot(_rms(x1, nxpre_ref[...]).astype(BF16), wxq_ref[...]).astype(BF16) for x1 in x1s]
    pairs = [(g, m, n, h) for g, members in enumerate(groups) for m, (n, _) in enumerate(members)
             for h in range(X_HEADS)]
    scores = [lax.dot_general(qs[g][m * seg:(m + 1) * seg, h * dh:(h + 1) * dh],
                              mkb_ref[n, :, h * dh:(h + 1) * dh], NT_DIMS,
                              preferred_element_type=F32) * (dh ** -0.5) for g, m, n, h in pairs]
    probs = []
    for s in scores:
        ex = jnp.exp(s - jnp.max(s, axis=-1, keepdims=True))
        probs.append((ex / jnp.sum(ex, axis=-1, keepdims=True)).astype(BF16))
    ctx = [_dot(p, mvb_ref[n, :, h * dh:(h + 1) * dh]).astype(BF16) for p, (g, m, n, h) in zip(probs, pairs)]
    xos = []
    for g, members in enumerate(groups):
        seqs = [jnp.concatenate([c for c, (g2, m2, _, _) in zip(ctx, pairs) if g2 == g and m2 == m], axis=1)
                for m in range(len(members))]
        xos.append(_dot(jnp.concatenate(seqs, axis=0), wxo_ref[...]))
    for members, x1, xo in zip(groups, x1s, xos):
        res = x1 + _rms(xo, nxpost_ref[...])
        for m, (n, r0) in enumerate(members):
            xo_ref[n, r0:r0 + seg, :] = res[m * seg:(m + 1) * seg]


def _post(x, a_out, b_out, pc, pool_prev, mk, mv, wts, past, layer):
    b, l, d = x.shape
    n_mem, dh = mk.shape[2], mk.shape[4]
    tl = POST_ROWS if l % POST_ROWS == 0 else CHUNK
    nb = max(1, min(b, POST_ROWS // tl, POST_SEQS))
    while b % nb:
        nb -= 1
    n_split = 2 if (nb * tl) % (2 * LANES) == 0 and (nb == 1 or nb % 2 == 0) else 1
    halo = POOL_STATE + 1
    pprev = jnp.concatenate([jnp.zeros((b, 1, C_WIDTH), F32), pool_prev], axis=1)
    tile = lambda w: pl.BlockSpec((nb, tl, w), lambda i, j: (i, j, 0))
    per_seq = lambda r, w: pl.BlockSpec((nb, r, w), lambda i, j: (i, 0, 0))
    full = lambda a: pl.BlockSpec(a.shape, lambda i, j: (0,) * a.ndim)
    hbm = pl.BlockSpec(memory_space=pl.ANY)
    return pl.pallas_call(
        functools.partial(_post_kernel, nb=nb, tl=tl, n_split=n_split, past=past, layer=layer),
        grid=(b // nb, l // tl),
        in_specs=[tile(d), tile(A_WIDTH), tile(B_WIDTH), tile(2 * C_WIDTH), per_seq(halo, C_WIDTH),
                  hbm, hbm] + [full(a) for a in wts],
        out_specs=[tile(d), per_seq(POOL_STATE, C_WIDTH)],
        out_shape=[jax.ShapeDtypeStruct((b, l, d), F32),
                   jax.ShapeDtypeStruct((b, POOL_STATE, C_WIDTH), F32)],
        scratch_shapes=[pltpu.VMEM((nb, SUBLANES + halo + tl, C_WIDTH), F32),
                        pltpu.VMEM((len(POOL_WINDOWS) - 1, nb, SUBLANES + halo + tl, C_WIDTH), F32),
                        pltpu.VMEM((2, 2, nb, X_HEADS, n_mem, dh), F32),
                        pltpu.VMEM((nb, n_mem, d), BF16),
                        pltpu.VMEM((nb, n_mem, d), BF16),
                        pltpu.SemaphoreType.DMA((2, 2))],
        compiler_params=_params(2),
        name="mix_out",
    )(x, a_out, b_out, pc, pprev, mk, mv, *wts)


def _layer(x, mk, mv, past_t, s0, s0_layer, pool_prev, lw, layer, depth, chain):
    kv_t, s_all = chain
    b, l, d = x.shape
    n = b * l
    proj, kv_t = _proj(x.reshape(n, d), lw["n_pre"], lw["w_in"], lw["lb"], l, layer, depth, kv_t)
    seq = lambda t: t.reshape(b, l, -1)
    gates, (bq, bk, bv, bg, pc) = [seq(t) for t in proj[:5]], proj[5:]
    a_out, s_all = _hgrn(gates, lw["onorm_g"], s0, s0_layer, layer, depth, s_all)
    b_out = _sb(seq(bq), seq(bk), seq(bv), seq(bg), past_t, layer)
    past = 0 if past_t is None else past_t[0].shape[3]
    x_new, pool_new = _post(x, a_out, b_out, seq(pc), pool_prev, mk, mv, lw["post"], past, layer)
    heads = lambda t: t.reshape(b, l, B_HEADS, B_HEAD_DIM)
    return x_new, heads(bk), heads(bv), (kv_t, s_all), pool_new


def kernel(x_prompt, x_sample, mem_prompt, cache_sb_k, cache_sb_v, state_hgrn, state_pool, cache_mem_k, cache_mem_v, norm_mix_pre, norm_mix_post, w_in, hgrn_lb_logits, hgrn_onorm_g, w_pool, pool_scale, w_out, norm_x_pre, norm_x_post, norm_mem, w_xq, w_xk, w_xv, w_xo):
    depth, d = norm_mix_pre.shape
    bp = x_prompt.shape[0]
    bs, past = cache_sb_k.shape[1], cache_sb_k.shape[2]
    n_mem = mem_prompt.shape[1]

    lbs = jax.nn.softmax(hgrn_lb_logits.astype(F32), axis=0)
    lower_bounds = jnp.maximum(jnp.cumsum(lbs, axis=0) - lbs[0], 0.0)

    mk_all, mv_all = _memkv(mem_prompt, norm_mem, w_xk.astype(BF16), w_xv.astype(BF16))

    def seq_minor(c):
        return jnp.transpose(c, (0, 1, 3, 4, 2)).reshape(depth, bs, B_WIDTH, past)

    def seq_major(t):
        dp, b, _, l = t.shape
        return jnp.transpose(t.reshape(dp, b, B_HEADS, B_HEAD_DIM, l), (0, 1, 4, 2, 3))

    past_t = (seq_minor(cache_sb_k), seq_minor(cache_sb_v))

    row = lambda v: v.reshape(1, -1)
    s_zero = jnp.zeros((1, bp, A_HEADS, A_HEAD_DIM, A_HEAD_DIM), F32)
    pool_zero = jnp.zeros((bp, POOL_STATE, C_WIDTH), F32)

    xp, xs = x_prompt, x_sample
    p_kn, p_vn, p_pools, s_kn, s_vn, s_pools = [], [], [], [], [], []
    chain_p = chain_s = (None, None)
    for l in range(depth):
        wp = jnp.zeros((C_WIDTH, C_WIDTH), F32)
        for gi in range(C_GROUPS):
            gs = slice(gi * C_GROUP_DIM, (gi + 1) * C_GROUP_DIM)
            wp = wp.at[gs, gs].set(w_pool[l, gi])
        wo = w_out[l].astype(BF16)
        lw = {
            "n_pre": norm_mix_pre[l], "w_in": w_in[l].astype(BF16), "lb": lower_bounds[l],
            "onorm_g": hgrn_onorm_g[l],
            "post": (wp.astype(BF16), row(pool_scale[l]), wo[0:A_WIDTH], wo[A_WIDTH:A_WIDTH + B_WIDTH],
                     wo[A_WIDTH + B_WIDTH:], row(norm_mix_post[l]), row(norm_x_pre[l]),
                     w_xq[l].astype(BF16), w_xo[l].astype(BF16), row(norm_x_post[l])),
        }
        xp, kb, vb, chain_p, pn = _layer(xp, mk_all, mv_all, None, s_zero, 0, pool_zero, lw, l, depth, chain_p)
        p_kn.append(kb); p_vn.append(vb); p_pools.append(pn)
        xs, kb, vb, chain_s, pn = _layer(xs, cache_mem_k, cache_mem_v, past_t, state_hgrn, l, state_pool[l],
                                         lw, l, depth, chain_s)
        s_kn.append(kb); s_vn.append(vb); s_pools.append(pn)

    def new_kv(kvt, nat_k, nat_v):
        if kvt is not None:
            return seq_major(kvt[0]), seq_major(kvt[1])
        return jnp.stack(nat_k), jnp.stack(nat_v)

    p_k, p_v = new_kv(chain_p[0], p_kn, p_vn)
    s_k, s_v = new_kv(chain_s[0], s_kn, s_vn)
    return (xp, xs, p_k, p_v, chain_p[1], jnp.stack(p_pools), mk_all, mv_all, s_k, s_v, chain_s[1],
            jnp.stack(s_pools))
```

```python
import functools

import jax
import jax.numpy as jnp
from jax import lax
from jax.experimental import pallas as pl
from jax.experimental.pallas import tpu as pltpu

F32 = jnp.float32
BF16 = jnp.bfloat16
EPS = 1e-6

A_HEADS, A_HEAD_DIM = 4, 128
A_WIDTH = A_HEADS * A_HEAD_DIM
B_HEADS, B_HEAD_DIM = 4, 64
B_WIDTH = B_HEADS * B_HEAD_DIM
C_GROUPS, C_GROUP_DIM = 4, 64
C_WIDTH = C_GROUPS * C_GROUP_DIM
POOL_WINDOWS = (2, 4, 8, 16)
POOL_STATE = 15
X_HEADS = 4
CHUNK = 64
SUB = 16
LOG2_E = 1.4426950408889634
SAFE_SPAN = 40.0
KEY_BLOCK = 128
SB_SEQS = 4
HGRN_ROWS = 1024
POST_ROWS = 512
POST_SEQS = 4
SUBLANES = 8
LANES = 128
VMEM_LIMIT = 56 * 1024 * 1024

NT_DIMS = (((1,), (1,)), ((), ()))
TN_DIMS = (((0,), (0,)), ((), ()))


def _rms(x, g):
    ms = jnp.mean(x * x, axis=-1, keepdims=True)
    return x * lax.rsqrt(ms + EPS) * g


def _silu(x):
    return x * (1.0 / (1.0 + jnp.exp(-x)))


def _dot(a, b):
    return jnp.dot(a, b, preferred_element_type=F32)


def _params(n_grid, flags=None):
    return pltpu.CompilerParams(dimension_semantics=("arbitrary",) * n_grid,
                                vmem_limit_bytes=VMEM_LIMIT, flags=flags)


def _memkv_kernel(mem_ref, g_ref, wk_ref, wv_ref, k_ref, v_ref):
    nb, n_mem, d = mem_ref.shape
    dh = d // X_HEADS
    for b in range(nb):
        m = _rms(mem_ref[b], g_ref[0]).astype(BF16)
        k = _dot(m, wk_ref[0])
        v = _dot(m, wv_ref[0])
        for h in range(X_HEADS):
            k_ref[0, b, :, h, :] = k[:, h * dh:(h + 1) * dh]
            v_ref[0, b, :, h, :] = v[:, h * dh:(h + 1) * dh]


def _memkv(mem, norm_mem, wk, wv):
    depth, d = norm_mem.shape
    nb, n_mem, _ = mem.shape
    dh = d // X_HEADS
    out = jax.ShapeDtypeStruct((depth, nb, n_mem, X_HEADS, dh), F32)
    return pl.pallas_call(
        _memkv_kernel,
        grid=(depth,),
        in_specs=[pl.BlockSpec((nb, n_mem, d), lambda l: (0, 0, 0)),
                  pl.BlockSpec((1, 1, d), lambda l: (l, 0, 0)),
                  pl.BlockSpec((1, d, d), lambda l: (l, 0, 0)),
                  pl.BlockSpec((1, d, d), lambda l: (l, 0, 0))],
        out_specs=[pl.BlockSpec((1, nb, n_mem, X_HEADS, dh), lambda l: (l, 0, 0, 0, 0))] * 2,
        out_shape=[out, out],
        compiler_params=_params(1),
        name="mem_kv",
    )(mem, norm_mem.reshape(depth, 1, d), wk, wv)


N_PROJ_IN = 7
PROJ_GROUPS = 4
N_PROJ_OUT = 10


def _proj_kernel(*refs, transposed, aliased):
    x0_ref, xn_ref, g_ref, w_ref, loglb_ref, log1mlb_ref, omlb_ref = refs[:N_PROJ_IN]
    outs = refs[N_PROJ_IN + 2 * aliased:]
    gcum_ref, qa_ref, ka_ref, ai_ref, ag_ref, q_ref, k_ref, v_ref, bg_ref, pc_ref = outs[:N_PROJ_OUT]
    h_ref = outs[-1]
    i = pl.program_id(0)
    a_end = 4 * A_WIDTH
    b_end = a_end + 4 * B_WIDTH
    tm = xn_ref.shape[0]
    n_groups = PROJ_GROUPS if tm % (PROJ_GROUPS * LANES) == 0 else 1
    groups = [(g * tm // n_groups, tm // n_groups) for g in range(n_groups)]
    sub = lax.broadcasted_iota(jnp.int32, (SUBLANES, A_WIDTH), 0)

    @pl.when(i == 0)
    def _():
        h_ref[...] = _rms(x0_ref[...], g_ref[...]).astype(BF16)

    def step():
        h_next = _rms(xn_ref[...], g_ref[...]).astype(BF16)
        hs = [h_ref[r0:r0 + nr, :] for r0, nr in groups]
        qfs = [_dot(h, w_ref[:, 0:2 * A_WIDTH]) for h in hs]
        for (r0, nr), h, qf in zip(groups, hs, qfs):
            rows = slice(r0, r0 + nr)
            aq, af = qf[:, 0:A_WIDTH], qf[:, A_WIDTH:2 * A_WIDTH]
            e = jnp.exp(-jnp.abs(af))
            ope = 1.0 + e
            logsig = jnp.minimum(af, 0.0) - jnp.log(ope)
            r = 1.0 / ope
            sig_neg = jnp.where(af >= 0, e * r, r)
            la = loglb_ref[...]
            lb_ = log1mlb_ref[...] + logsig
            log_f = jnp.maximum(la, lb_) + jnp.log(1.0 + jnp.exp(-jnp.abs(la - lb_)))
            ka_ref[rows, :] = omlb_ref[...] * sig_neg
            qa_ref[rows, :] = _silu(aq) * (A_HEAD_DIM ** -0.5)

            ig = _dot(h, w_ref[:, 2 * A_WIDTH:a_end])
            ai_ref[rows, :] = ig[:, 0:A_WIDTH].astype(BF16)
            ag_ref[rows, :] = ig[:, A_WIDTH:2 * A_WIDTH]
            pb = _dot(h, w_ref[:, a_end:b_end])
            q_ref[rows, :] = pb[:, 0:B_WIDTH]
            k_ref[rows, :] = pb[:, B_WIDTH:2 * B_WIDTH]
            v_ref[rows, :] = pb[:, 2 * B_WIDTH:3 * B_WIDTH]
            bg_ref[rows, :] = pb[:, 3 * B_WIDTH:4 * B_WIDTH]
            pc_ref[rows, :] = _dot(h, w_ref[:, b_end:])
            if transposed:
                kt_ref, vt_ref = outs[N_PROJ_OUT:N_PROJ_OUT + 2]
                kt_ref[0, 0, :, rows] = pb[:, B_WIDTH:2 * B_WIDTH].T
                vt_ref[0, 0, :, rows] = pb[:, 2 * B_WIDTH:3 * B_WIDTH].T
            for c in range(nr // CHUNK):
                carry = None
                for jv in range(CHUNK // SUBLANES):
                    lo = c * CHUNK + jv * SUBLANES
                    s = log_f[lo:lo + SUBLANES, :]
                    for dist in (1, 2, 4):
                        s = s + jnp.where(sub >= dist, pltpu.roll(s, dist, 0), 0.0)
                    if carry is not None:
                        s = s + carry
                    carry = jnp.broadcast_to(s[SUBLANES - 1:SUBLANES, :], s.shape)
                    gcum_ref[r0 + lo:r0 + lo + SUBLANES, :] = s * LOG2_E

        h_ref[...] = h_next

    step()


def _proj(x2d, g, w_in, lb, seq_len, layer, depth, kv_t):
    n, d = x2d.shape
    d_in = w_in.shape[1]
    tm = 512 if n % 512 == 0 else CHUNK
    transposed = seq_len % tm == 0 and tm % LANES == 0
    aliased = transposed and kv_t is not None
    row = lambda width: pl.BlockSpec((tm, width), lambda i: (i, 0))
    vec = pl.BlockSpec((1, A_WIDTH), lambda i: (0, 0))
    shapes = [A_WIDTH] * 5 + [B_WIDTH] * 4 + [2 * C_WIDTH]
    dtypes = [F32, F32, F32, BF16, F32] + [F32] * 5
    lb = lb.reshape(1, A_WIDTH)
    last = n // tm - 1
    in_specs = [pl.BlockSpec((tm, d), lambda i: (0, 0)),
                pl.BlockSpec((tm, d), lambda i: (jnp.minimum(i + 1, last), 0)),
                pl.BlockSpec((1, d), lambda i: (0, 0)), pl.BlockSpec((d, d_in), lambda i: (0, 0)),
                vec, vec, vec]
    args = [x2d, x2d, g.reshape(1, d), w_in, jnp.log(lb), jnp.log1p(-lb), 1.0 - lb]
    out_specs = [row(w) for w in shapes]
    out_shape = [jax.ShapeDtypeStruct((n, w), t) for w, t in zip(shapes, dtypes)]
    aliases = {}
    if transposed:
        tiles = seq_len // tm
        t_spec = pl.BlockSpec((1, 1, B_WIDTH, tm), lambda i: (layer, i // tiles, 0, i % tiles))
        out_specs += [t_spec, t_spec]
        out_shape += [jax.ShapeDtypeStruct((depth, n // seq_len, B_WIDTH, seq_len), F32)] * 2
        if aliased:
            in_specs += [pl.BlockSpec(memory_space=pl.ANY)] * 2
            args += list(kv_t)
            aliases = {N_PROJ_IN: N_PROJ_OUT, N_PROJ_IN + 1: N_PROJ_OUT + 1}
    res = pl.pallas_call(
        functools.partial(_proj_kernel, transposed=transposed, aliased=aliased),
        grid=(n // tm,),
        in_specs=in_specs,
        out_specs=out_specs,
        out_shape=out_shape,
        scratch_shapes=[pltpu.VMEM((tm, d), BF16)],
        input_output_aliases=aliases,
        compiler_params=_params(1),
        name="in_proj",
    )(*args)
    return res[:N_PROJ_OUT], (tuple(res[N_PROJ_OUT:]) if transposed else None)


def _hgrn_kernel(*refs, nb, nch, aliased):
    g_ref, q_ref, k_ref, v_ref, gate_ref, og_ref, s0_ref = refs[:7]
    a_ref, sout_ref, st_ref = refs[7 + aliased:]
    j = pl.program_id(1)

    @pl.when(j == 0)
    def _():
        for n in range(nb):
            for h in range(A_HEADS):
                st_ref[n, h] = s0_ref[0, n, h].T

    rowc = lax.broadcasted_iota(jnp.int32, (CHUNK, CHUNK), 0)
    colc = lax.broadcasted_iota(jnp.int32, (CHUNK, CHUNK), 1)
    col_sub = colc // SUB
    row_k = lax.broadcasted_iota(jnp.int32, (CHUNK, A_HEAD_DIM), 0)
    lane_d = lax.broadcasted_iota(jnp.int32, (SUB, CHUNK), 1)
    n_sub = CHUNK // SUB
    head = [slice(h * A_HEAD_DIM, (h + 1) * A_HEAD_DIM) for h in range(A_HEADS)]

    def sub_starts(g_all):
        return [jnp.zeros((1, A_WIDTH), F32)] + [g_all[SUB * s - 1:SUB * s, :] for s in range(1, n_sub)]

    span = jnp.zeros((1, A_WIDTH), F32)
    for n in range(nb):
        for c in range(nch):
            for s in range(n_sub):
                last = c * CHUNK + SUB * s + SUB - 1
                end = g_ref[n, last:last + 1, :]
                span = jnp.maximum(span, -end if s == 0 else g_ref[n, last - SUB:last - SUB + 1, :] - end)
    small_decay = jnp.max(span) < SAFE_SPAN * LOG2_E

    def att_factored(g_all, q_all, k_all, qg_all):
        starts = sub_starts(g_all)
        operands = []
        for sl in head:
            gh, qh, kh = g_all[:, sl], q_all[:, sl], k_all[:, sl]
            parts, krows = [qg_all[:, sl]], []
            for s in range(n_sub):
                rs = slice(SUB * s, SUB * (s + 1))
                st = starts[s][:, sl]
                krows.append(kh[rs] * jnp.exp2(st - gh[rs]))
                if s:
                    parts.append(qh[SUB * s:, :] * jnp.exp2(gh[SUB * s:, :] - st))
            operands.append((jnp.concatenate(parts, axis=0).astype(BF16),
                             jnp.concatenate(krows, axis=0).astype(BF16)))
        rrs = [lax.dot_general(qr, kr, NT_DIMS, preferred_element_type=F32) for qr, kr in operands]
        atts = []
        for rr in rrs:
            att = rr[0:CHUNK, :]
            r0 = CHUNK
            for s in range(1, n_sub):
                rows = CHUNK - SUB * s
                block = jnp.concatenate([jnp.zeros((SUB * s, CHUNK), F32), rr[r0:r0 + rows, :]], axis=0)
                att = jnp.where(col_sub == s, block, att)
                r0 += rows
            atts.append(jnp.where(rowc >= colc, att, 0.0))
        return atts

    def att_exact(g_all, q_all, k_all, qg_all):
        def one_head(sl):
            gh, qh, kh = g_all[:, sl], q_all[:, sl], k_all[:, sl]
            parts, refrows = [], []
            for jsub in range(n_sub):
                ref = gh[SUB * jsub + SUB - 1:SUB * jsub + SUB, :]
                refrows.append(jnp.broadcast_to(ref, (SUB, A_HEAD_DIM)))
                if jsub < n_sub - 1:
                    later = row_k >= SUB * (jsub + 1)
                    parts.append(jnp.where(later, qh * jnp.exp2(jnp.minimum(gh - ref, 0.0)), 0.0))
            q_rel = jnp.concatenate(parts, axis=0).astype(BF16)
            k_rel = (kh * jnp.exp2(jnp.minimum(jnp.concatenate(refrows, axis=0) - gh, 0.0))).astype(BF16)
            rr = lax.dot_general(q_rel, k_rel, NT_DIMS, preferred_element_type=F32)
            att = jnp.zeros((CHUNK, CHUNK), F32)
            for jsub in range(n_sub - 1):
                att = jnp.where(col_sub == jsub, rr[jsub * CHUNK:(jsub + 1) * CHUNK, :], att)

            dparts = []
            for csub in range(n_sub):
                rs = slice(SUB * csub, SUB * (csub + 1))
                gc, qc, kc = gh[rs], qh[rs], kh[rs]
                accd = jnp.zeros((SUB, CHUNK), F32)
                for s in range(SUB):
                    dec = jnp.exp2(jnp.minimum(gc - gc[s:s + 1, :], 0.0))
                    col = jnp.sum(dec * qc * kc[s:s + 1, :], axis=-1, keepdims=True)
                    accd = jnp.where(lane_d == SUB * csub + s, col, accd)
                dparts.append(accd)
            return att + jnp.where(rowc >= colc, jnp.concatenate(dparts, axis=0), 0.0)

        return [one_head(sl) for sl in head]

    def chunk(n, r0, sts, att_fn):
        g_all = g_ref[n, pl.ds(r0, CHUNK), :]
        q_all = q_ref[n, pl.ds(r0, CHUNK), :]
        k_all = k_ref[n, pl.ds(r0, CHUNK), :]
        v_all = v_ref[n, pl.ds(r0, CHUNK), :]
        ag = gate_ref[n, pl.ds(r0, CHUNK), :]
        qg_all = q_all * jnp.exp2(g_all)
        gl_all = g_all[CHUNK - 1:CHUNK, :]
        kd_all = (k_all * jnp.exp2(gl_all - g_all)).astype(BF16)
        qg_bf = qg_all.astype(BF16)
        atts = att_fn(g_all, q_all, k_all, qg_all)

        o_state = [lax.dot_general(qg_bf[:, sl], st.astype(BF16), NT_DIMS, preferred_element_type=F32)
                   for sl, st in zip(head, sts)]
        o_pairs = [_dot(att.astype(BF16), v_all[:, sl]) for sl, att in zip(head, atts)]
        updates = [lax.dot_general(v_all[:, sl], kd_all[:, sl], TN_DIMS, preferred_element_type=F32)
                   for sl in head]
        new_sts = [st * jnp.exp2(gl_all[:, sl]) + up for sl, st, up in zip(head, sts, updates)]
        o = jnp.concatenate([a + b for a, b in zip(o_state, o_pairs)], axis=1)
        sq = o * o
        inv = jnp.concatenate(
            [jnp.broadcast_to(lax.rsqrt(jnp.mean(sq[:, sl], axis=-1, keepdims=True) + EPS), (CHUNK, A_HEAD_DIM))
             for sl in head], axis=1)
        a_ref[n, pl.ds(r0, CHUNK), :] = (o * inv * og_ref[...] * _silu(ag)).astype(BF16)
        return new_sts

    @pl.when(small_decay)
    def _():
        for n in range(nb):
            sts = [st_ref[n, h] for h in range(A_HEADS)]
            for c in range(nch):
                sts = chunk(n, c * CHUNK, sts, att_factored)
            for h in range(A_HEADS):
                st_ref[n, h] = sts[h]

    @pl.when(jnp.logical_not(small_decay))
    def _():
        for n in range(nb):
            def body(c, carry):
                sts = chunk(n, pl.multiple_of(c * CHUNK, CHUNK), [st_ref[n, h] for h in range(A_HEADS)], att_exact)
                for h in range(A_HEADS):
                    st_ref[n, h] = sts[h]
                return carry
            lax.fori_loop(0, nch, body, 0)

    @pl.when(j == pl.num_programs(1) - 1)
    def _():
        for n in range(nb):
            for h in range(A_HEADS):
                sout_ref[0, n, h] = st_ref[n, h].T


def _hgrn(gates, onorm_g, s0, s0_layer, layer, depth, s_all):
    b, l, _ = gates[0].shape
    tl = HGRN_ROWS if l % HGRN_ROWS == 0 else CHUNK
    nb = max(1, min(b, HGRN_ROWS // tl))
    while b % nb:
        nb -= 1
    tile = pl.BlockSpec((nb, tl, A_WIDTH), lambda i, j: (i, j, 0))
    st_block = (1, nb, A_HEADS, A_HEAD_DIM, A_HEAD_DIM)
    aliased = s_all is not None
    in_specs = [tile] * 5 + [pl.BlockSpec((1, A_WIDTH), lambda i, j: (0, 0)),
                             pl.BlockSpec(st_block, lambda i, j: (s0_layer, i, 0, 0, 0))]
    args = list(gates) + [onorm_g.reshape(1, A_WIDTH), s0]
    if aliased:
        in_specs.append(pl.BlockSpec(memory_space=pl.ANY))
        args.append(s_all)
    return pl.pallas_call(
        functools.partial(_hgrn_kernel, nb=nb, nch=tl // CHUNK, aliased=aliased),
        grid=(b // nb, l // tl),
        in_specs=in_specs,
        out_specs=[pl.BlockSpec((nb, tl, A_WIDTH), lambda i, j: (i, j, 0)),
                   pl.BlockSpec(st_block, lambda i, j: (layer, i, 0, 0, 0))],
        out_shape=[jax.ShapeDtypeStruct((b, l, A_WIDTH), BF16),
                   jax.ShapeDtypeStruct((depth, b, A_HEADS, A_HEAD_DIM, A_HEAD_DIM), F32)],
        scratch_shapes=[pltpu.VMEM((nb, A_HEADS, A_HEAD_DIM, A_HEAD_DIM), F32)],
        input_output_aliases={7: 1} if aliased else {},
        compiler_params=_params(2),
        name="hgrn2",
    )(*args)


def _sb_kernel(*refs, ns, lq, n_past):
    if n_past:
        q_ref, k_ref, v_ref, g_ref, pk_ref, pv_ref, o_ref = refs[:7]
        scratch = refs[7:]
    else:
        q_ref, k_ref, v_ref, g_ref, o_ref = refs[:5]
        pk_ref = pv_ref = None
        scratch = refs[5:]
    kc_ref, vtc_ref, qtm_ref, carry_ref, acc_ref = scratch
    j = pl.program_id(1)
    kb_rows = KEY_BLOCK
    run_len = kb_rows // SUBLANES
    width = B_WIDTH
    streams = range(ns)

    row = lax.broadcasted_iota(jnp.int32, (kb_rows, kb_rows), 0)
    lane = lax.broadcasted_iota(jnp.int32, (kb_rows, kb_rows), 1)
    key_of_row = (row % SUBLANES) * run_len + row // SUBLANES
    causal = key_of_row < lane
    perm = (lane == key_of_row).astype(F32)
    sub = lax.broadcasted_iota(jnp.int32, (SUBLANES, kb_rows), 0)

    def padded(rows):
        if rows.shape[0] == kb_rows:
            return rows
        return jnp.concatenate([rows, jnp.zeros((kb_rows - rows.shape[0], rows.shape[1]), rows.dtype)], axis=0)

    row_head = lax.broadcasted_iota(jnp.int32, (width, kb_rows), 0) // B_HEAD_DIM
    k_perm = [_dot(perm, padded(k_ref[s])) for s in streams]
    v_perm = [_dot(perm, padded(v_ref[s])) for s in streams]
    for s in streams:
        kc_ref[s, j] = k_perm[s].astype(BF16)
        vtc_ref[s, j] = v_perm[s].T.astype(BF16)
        qt = padded(q_ref[s] * (0.5 * B_HEAD_DIM ** -0.5)).T
        qtm_ref[s] = jnp.concatenate([jnp.where(row_head == h, qt, 0.0) for h in range(B_HEADS)],
                                     axis=1).astype(BF16)

    carry_ref[...] = jnp.ones(carry_ref.shape, F32)
    acc_ref[...] = jnp.zeros(acc_ref.shape, F32)

    def process(blocks, masked):
        zts = [_dot(kb, qtm_ref[s]) for s, (kb, _) in enumerate(blocks)]
        ws = []
        for s, zt in enumerate(zts):
            for h in range(B_HEADS):
                ls = slice(h * kb_rows, (h + 1) * kb_rows)
                half_t = 0.5 * jnp.tanh(zt[:, ls])
                beta = 0.5 + half_t
                omb = 0.5 - half_t
                if masked:
                    beta = jnp.where(causal, beta, 0.0)
                    omb = jnp.where(causal, omb, 1.0)
                run = jnp.ones((SUBLANES, kb_rows), F32)
                excl = [None] * run_len
                for jj in reversed(range(run_len)):
                    excl[jj] = run
                    run = run * omb[jj * SUBLANES:(jj + 1) * SUBLANES]
                inc = run
                for d in (1, 2, 4):
                    inc = jnp.where(sub + d < SUBLANES, inc * pltpu.roll(inc, SUBLANES - d, 0), inc)
                carry = carry_ref[s, :, ls]
                off = jnp.where(sub < SUBLANES - 1, pltpu.roll(inc, SUBLANES - 1, 0), 1.0) * carry
                ws.append(jnp.concatenate([beta[jj * SUBLANES:(jj + 1) * SUBLANES] * (excl[jj] * off)
                                           for jj in range(run_len)], axis=0).astype(BF16))
                carry_ref[s, :, ls] = carry * jnp.broadcast_to(inc[0:1, :], (SUBLANES, kb_rows))
        for s, (_, vtb) in enumerate(blocks):
            for h in range(B_HEADS):
                hs = slice(h * B_HEAD_DIM, (h + 1) * B_HEAD_DIM)
                acc_ref[s, hs, :] += _dot(vtb[hs, :], ws[s * B_HEADS + h])

    def alive():
        return (jnp.max(carry_ref[...]) > 0.0).astype(jnp.int32)

    process([(kc_ref[s, j], vtc_ref[s, j]) for s in streams], True)

    def cond(state):
        kb, live = state
        return jnp.logical_and(kb >= 0, live > 0)

    def body(state):
        kb, _ = state
        process([(kc_ref[s, kb], vtc_ref[s, kb]) for s in streams], False)
        return kb - 1, alive()

    lax.while_loop(cond, body, (j - 1, alive()))

    for kb in reversed(range(n_past)):
        @pl.when(alive() > 0)
        def _():
            cols = slice(kb * kb_rows, (kb + 1) * kb_rows)
            blocks = []
            for s in streams:
                kt = pk_ref[0, s, :, cols]
                vt = pv_ref[0, s, :, cols]
                blocks.append((lax.dot_general(perm, kt, NT_DIMS, preferred_element_type=F32).astype(BF16),
                               lax.dot_general(vt, perm, NT_DIMS, preferred_element_type=F32).astype(BF16)))
            process(blocks, False)

    for s in streams:
        o = acc_ref[s].T[0:lq, :]
        o_ref[s] = (o * _silu(g_ref[s])).astype(BF16)


def _sb(q, k, v, g, past_t, layer):
    b, l, w = q.shape
    lq = min(KEY_BLOCK, l)
    nblk = l // lq
    ns = min(b, SB_SEQS)
    while b % ns:
        ns -= 1
    p = 0 if past_t is None else past_t[0].shape[3]
    blk = pl.BlockSpec((ns, lq, w), lambda i, j: (i, j, 0))
    in_specs = [blk] * 4
    args = [q, k, v, g]
    if p:
        in_specs += [pl.BlockSpec((1, ns, w, p), lambda i, j: (layer, i, 0, 0))] * 2
        args += list(past_t)
    return pl.pallas_call(
        functools.partial(_sb_kernel, ns=ns, lq=lq, n_past=p // KEY_BLOCK),
        grid=(b // ns, nblk),
        in_specs=in_specs,
        out_specs=blk,
        out_shape=jax.ShapeDtypeStruct((b, l, w), BF16),
        scratch_shapes=[pltpu.VMEM((ns, nblk, KEY_BLOCK, w), BF16),
                        pltpu.VMEM((ns, nblk, w, KEY_BLOCK), BF16),
                        pltpu.VMEM((ns, w, B_HEADS * KEY_BLOCK), BF16),
                        pltpu.VMEM((ns, SUBLANES, B_HEADS * KEY_BLOCK), F32),
                        pltpu.VMEM((ns, w, KEY_BLOCK), F32)],
        compiler_params=_params(2),
        name="sb_attn",
    )(*args)


def _post_kernel(x_ref, a_ref, b_ref, pc_ref, pprev_ref, mk_hbm, mv_hbm, wpool_ref, pscale_ref,
                 woa_ref, wob_ref, woc_ref, npost_ref, nxpre_ref, wxq_ref, wxo_ref, nxpost_ref,
                 xo_ref, pnew_ref, ext_ref, sum_ref, kv_ref, mkb_ref, mvb_ref, sem_ref,
                 *, nb, tl, n_split, past, layer):
    i = pl.program_id(0)
    j = pl.program_id(1)
    n_groups = pl.num_programs(0)
    halo = POOL_STATE + 1
    lead = SUBLANES
    dh = kv_ref.shape[-1]

    def kv_copies(group, slot):
        return [pltpu.make_async_copy(src.at[layer, group * nb + n, :, h, :], kv_ref.at[slot, t, n, h],
                                      sem_ref.at[slot, t])
                for t, src in enumerate((mk_hbm, mv_hbm)) for n in range(nb) for h in range(X_HEADS)]

    @pl.when(j == 0)
    def _():
        slot = i % 2

        @pl.when(i == 0)
        def _():
            for cp in kv_copies(0, 0):
                cp.start()

        @pl.when(i + 1 < n_groups)
        def _():
            for cp in kv_copies(i + 1, 1 - slot):
                cp.start()

        for cp in kv_copies(i, slot):
            cp.wait()
        for n in range(nb):
            ext_ref[n, 0:lead, :] = jnp.zeros((lead, C_WIDTH), F32)
            ext_ref[n, lead:lead + halo, :] = pprev_ref[n]
            for gi in range(len(POOL_WINDOWS) - 1):
                sum_ref[gi, n, 0:lead, :] = jnp.zeros((lead, C_WIDTH), F32)
            for h in range(X_HEADS):
                mkb_ref[n, :, h * dh:(h + 1) * dh] = kv_ref[slot, 0, n, h].astype(BF16)
                mvb_ref[n, :, h * dh:(h + 1) * dh] = kv_ref[slot, 1, n, h].astype(BF16)

    grp = lax.broadcasted_iota(jnp.int32, (tl, C_WIDTH), 1) // C_GROUP_DIM
    pos = past + j * tl + lax.broadcasted_iota(jnp.int32, (tl, C_WIDTH), 0)
    dds = []
    rows = halo + tl
    for n in range(nb):
        cu = pc_ref[n, :, 0:C_WIDTH]
        ext_ref[n, lead + halo:lead + rows, :] = cu
        win = jnp.zeros((tl, C_WIDTH), F32)
        wlen = jnp.zeros((tl, C_WIDTH), jnp.int32)
        src = ext_ref.at[n]
        for gi, w in enumerate(POOL_WINDOWS):
            assert w == 2 ** (gi + 1)
            wsum = src[lead:lead + rows, :] + src[lead - w // 2:lead - w // 2 + rows, :]
            if gi + 1 < len(POOL_WINDOWS):
                sum_ref[gi, n, lead:lead + rows, :] = wsum
                src = sum_ref.at[gi, n]
            win = jnp.where(grp == gi, wsum[halo:, :], win)
            wlen = jnp.where(grp == gi, w, wlen)
        cnt = jnp.minimum(pos + 1, wlen).astype(F32)
        dds.append((win / cnt - cu).astype(BF16))
        pnew_ref[n] = ext_ref[n, lead + tl + 1:lead + tl + halo, :]
        ext_ref[n, lead:lead + halo, :] = ext_ref[n, lead + tl:lead + tl + halo, :]

    seg = tl // n_split if nb == 1 else tl
    groups = [[(0, g * seg)] for g in range(n_split)] if nb == 1 else \
             [[(n, 0) for n in range(g * nb // n_split, (g + 1) * nb // n_split)] for g in range(n_split)]

    def rows(members, ref, w0=0, w1=None):
        return jnp.concatenate([ref[n, r0:r0 + seg, w0:w1] for n, r0 in members], axis=0)

    mixes = []
    for members in groups:
        dd = jnp.concatenate([dds[n][r0:r0 + seg] for n, r0 in members], axis=0)
        y = _dot(dd, wpool_ref[...]) * pscale_ref[...]
        c_out = (y * _silu(rows(members, pc_ref, C_WIDTH, 2 * C_WIDTH))).astype(BF16)
        mixes.append(_dot(rows(members, a_ref), woa_ref[...]) + _dot(rows(members, b_ref), wob_ref[...])
                     + _dot(c_out, woc_ref[...]))
    x1s = [rows(members, x_ref) + _rms(mix, npost_ref[...]) for members, mix in zip(groups, mixes)]

    qs = [_dot(_rms(x1, nxpre_ref[...]).astype(BF16), wxq_ref[...]).astype(BF16) for x1 in x1s]
    pairs = [(g, m, n, h) for g, members in enumerate(groups) for m, (n, _) in enumerate(members)
             for h in range(X_HEADS)]
    scores = [lax.dot_general(qs[g][m * seg:(m + 1) * seg, h * dh:(h + 1) * dh],
                              mkb_ref[n, :, h * dh:(h + 1) * dh], NT_DIMS,
                              preferred_element_type=F32) * (dh ** -0.5) for g, m, n, h in pairs]
    probs = []
    for s in scores:
        ex = jnp.exp(s - jnp.max(s, axis=-1, keepdims=True))
        probs.append((ex / jnp.sum(ex, axis=-1, keepdims=True)).astype(BF16))
    ctx = [_dot(p, mvb_ref[n, :, h * dh:(h + 1) * dh]).astype(BF16) for p, (g, m, n, h) in zip(probs, pairs)]
    xos = []
    for g, members in enumerate(groups):
        seqs = [jnp.concatenate([c for c, (g2, m2, _, _) in zip(ctx, pairs) if g2 == g and m2 == m], axis=1)
                for m in range(len(members))]
        xos.append(_dot(jnp.concatenate(seqs, axis=0), wxo_ref[...]))
    for members, x1, xo in zip(groups, x1s, xos):
        res = x1 + _rms(xo, nxpost_ref[...])
        for m, (n, r0) in enumerate(members):
            xo_ref[n, r0:r0 + seg, :] = res[m * seg:(m + 1) * seg]


def _post(x, a_out, b_out, pc, pool_prev, mk, mv, wts, past, layer):
    b, l, d = x.shape
    n_mem, dh = mk.shape[2], mk.shape[4]
    tl = POST_ROWS if l % POST_ROWS == 0 else CHUNK
    nb = max(1, min(b, POST_ROWS // tl, POST_SEQS))
    while b % nb:
        nb -= 1
    n_split = 2 if (nb * tl) % (2 * LANES) == 0 and (nb == 1 or nb % 2 == 0) else 1
    halo = POOL_STATE + 1
    pprev = jnp.concatenate([jnp.zeros((b, 1, C_WIDTH), F32), pool_prev], axis=1)
    tile = lambda w: pl.BlockSpec((nb, tl, w), lambda i, j: (i, j, 0))
    per_seq = lambda r, w: pl.BlockSpec((nb, r, w), lambda i, j: (i, 0, 0))
    full = lambda a: pl.BlockSpec(a.shape, lambda i, j: (0,) * a.ndim)
    hbm = pl.BlockSpec(memory_space=pl.ANY)
    return pl.pallas_call(
        functools.partial(_post_kernel, nb=nb, tl=tl, n_split=n_split, past=past, layer=layer),
        grid=(b // nb, l // tl),
        in_specs=[tile(d), tile(A_WIDTH), tile(B_WIDTH), tile(2 * C_WIDTH), per_seq(halo, C_WIDTH),
                  hbm, hbm] + [full(a) for a in wts],
        out_specs=[tile(d), per_seq(POOL_STATE, C_WIDTH)],
        out_shape=[jax.ShapeDtypeStruct((b, l, d), F32),
                   jax.ShapeDtypeStruct((b, POOL_STATE, C_WIDTH), F32)],
        scratch_shapes=[pltpu.VMEM((nb, SUBLANES + halo + tl, C_WIDTH), F32),
                        pltpu.VMEM((len(POOL_WINDOWS) - 1, nb, SUBLANES + halo + tl, C_WIDTH), F32),
                        pltpu.VMEM((2, 2, nb, X_HEADS, n_mem, dh), F32),
                        pltpu.VMEM((nb, n_mem, d), BF16),
                        pltpu.VMEM((nb, n_mem, d), BF16),
                        pltpu.SemaphoreType.DMA((2, 2))],
        compiler_params=_params(2),
        name="mix_out",
    )(x, a_out, b_out, pc, pprev, mk, mv, *wts)


def _layer(x, mk, mv, past_t, s0, s0_layer, pool_prev, lw, layer, depth, chain):
    kv_t, s_all = chain
    b, l, d = x.shape
    n = b * l
    proj, kv_t = _proj(x.reshape(n, d), lw["n_pre"], lw["w_in"], lw["lb"], l, layer, depth, kv_t)
    seq = lambda t: t.reshape(b, l, -1)
    gates, (bq, bk, bv, bg, pc) = [seq(t) for t in proj[:5]], proj[5:]
    a_out, s_all = _hgrn(gates, lw["onorm_g"], s0, s0_layer, layer, depth, s_all)
    b_out = _sb(seq(bq), seq(bk), seq(bv), seq(bg), past_t, layer)
    past = 0 if past_t is None else past_t[0].shape[3]
    x_new, pool_new = _post(x, a_out, b_out, seq(pc), pool_prev, mk, mv, lw["post"], past, layer)
    heads = lambda t: t.reshape(b, l, B_HEADS, B_HEAD_DIM)
    return x_new, heads(bk), heads(bv), (kv_t, s_all), pool_new


def kernel(x_prompt, x_sample, mem_prompt, cache_sb_k, cache_sb_v, state_hgrn, state_pool, cache_mem_k, cache_mem_v, norm_mix_pre, norm_mix_post, w_in, hgrn_lb_logits, hgrn_onorm_g, w_pool, pool_scale, w_out, norm_x_pre, norm_x_post, norm_mem, w_xq, w_xk, w_xv, w_xo):
    depth, d = norm_mix_pre.shape
    bp = x_prompt.shape[0]
    bs, past = cache_sb_k.shape[1], cache_sb_k.shape[2]

    lbs = jax.nn.softmax(hgrn_lb_logits.astype(F32), axis=0)
    lower_bounds = jnp.maximum(jnp.cumsum(lbs, axis=0) - lbs[0], 0.0)

    mk_all, mv_all = _memkv(mem_prompt, norm_mem, w_xk.astype(BF16), w_xv.astype(BF16))

    def seq_minor(c):
        return jnp.transpose(c, (0, 1, 3, 4, 2)).reshape(depth, bs, B_WIDTH, past)

    def seq_major(t):
        dp, b, _, l = t.shape
        return jnp.transpose(t.reshape(dp, b, B_HEADS, B_HEAD_DIM, l), (0, 1, 4, 2, 3))

    past_t = (seq_minor(cache_sb_k), seq_minor(cache_sb_v))

    row = lambda v: v.reshape(1, -1)
    s_zero = jnp.zeros((1, bp, A_HEADS, A_HEAD_DIM, A_HEAD_DIM), F32)
    pool_zero = jnp.zeros((bp, POOL_STATE, C_WIDTH), F32)

    xp, xs = x_prompt, x_sample
    p_kn, p_vn, p_pools, s_kn, s_vn, s_pools = [], [], [], [], [], []
    chain_p = chain_s = (None, None)
    for l in range(depth):
        wp = jnp.zeros((C_WIDTH, C_WIDTH), F32)
        for gi in range(C_GROUPS):
            gs = slice(gi * C_GROUP_DIM, (gi + 1) * C_GROUP_DIM)
            wp = wp.at[gs, gs].set(w_pool[l, gi])
        wo = w_out[l].astype(BF16)
        lw = {
            "n_pre": norm_mix_pre[l], "w_in": w_in[l].astype(BF16), "lb": lower_bounds[l],
            "onorm_g": hgrn_onorm_g[l],
            "post": (wp.astype(BF16), row(pool_scale[l]), wo[0:A_WIDTH], wo[A_WIDTH:A_WIDTH + B_WIDTH],
                     wo[A_WIDTH + B_WIDTH:], row(norm_mix_post[l]), row(norm_x_pre[l]),
                     w_xq[l].astype(BF16), w_xo[l].astype(BF16), row(norm_x_post[l])),
        }
        xp, kb, vb, chain_p, pn = _layer(xp, mk_all, mv_all, None, s_zero, 0, pool_zero, lw, l, depth, chain_p)
        p_kn.append(kb); p_vn.append(vb); p_pools.append(pn)
        xs, kb, vb, chain_s, pn = _layer(xs, cache_mem_k, cache_mem_v, past_t, state_hgrn, l, state_pool[l],
                                         lw, l, depth, chain_s)
        s_kn.append(kb); s_vn.append(vb); s_pools.append(pn)

    def new_kv(kvt, nat_k, nat_v):
        if kvt is not None:
            return seq_major(kvt[0]), seq_major(kvt[1])
        return jnp.stack(nat_k), jnp.stack(nat_v)

    p_k, p_v = new_kv(chain_p[0], p_kn, p_vn)
    s_k, s_v = new_kv(chain_s[0], s_kn, s_vn)
    return (xp, xs, p_k, p_v, chain_p[1], jnp.stack(p_pools), mk_all, mv_all, s_k, s_v, chain_s[1],
            jnp.stack(s_pools))
```

```python
import functools

import jax
import jax.numpy as jnp
from jax import lax
from jax.experimental import pallas as pl
from jax.experimental.pallas import tpu as pltpu

F32 = jnp.float32
BF16 = jnp.bfloat16
EPS = 1e-6

A_HEADS, A_HEAD_DIM = 4, 128
A_WIDTH = A_HEADS * A_HEAD_DIM
B_HEADS, B_HEAD_DIM = 4, 64
B_WIDTH = B_HEADS * B_HEAD_DIM
C_GROUPS, C_GROUP_DIM = 4, 64
C_WIDTH = C_GROUPS * C_GROUP_DIM
POOL_WINDOWS = (2, 4, 8, 16)
POOL_STATE = 15
X_HEADS = 4
CHUNK = 64
SUB = 16
LOG2_E = 1.4426950408889634
SAFE_SPAN = 40.0
KEY_BLOCK = 128
SB_SEQS = 4
HGRN_ROWS = 1024
HGRN_SEQS = 8
POST_ROWS = 512
POST_SEQS = 4
SUBLANES = 8
LANES = 128
VMEM_LIMIT = 56 * 1024 * 1024

NT_DIMS = (((1,), (1,)), ((), ()))
TN_DIMS = (((0,), (0,)), ((), ()))


def _rms(x, g):
    ms = jnp.mean(x * x, axis=-1, keepdims=True)
    return x * lax.rsqrt(ms + EPS) * g


def _silu(x):
    return x * (1.0 / (1.0 + jnp.exp(-x)))


def _dot(a, b):
    return jnp.dot(a, b, preferred_element_type=F32)


def _params(n_grid, flags=None):
    return pltpu.CompilerParams(dimension_semantics=("arbitrary",) * n_grid,
                                vmem_limit_bytes=VMEM_LIMIT, flags=flags)


def _memkv_kernel(mem_ref, g_ref, wk_ref, wv_ref, k_ref, v_ref):
    nb, n_mem, d = mem_ref.shape
    dh = d // X_HEADS
    for b in range(nb):
        m = _rms(mem_ref[b], g_ref[0]).astype(BF16)
        k = _dot(m, wk_ref[0])
        v = _dot(m, wv_ref[0])
        for h in range(X_HEADS):
            k_ref[0, b, :, h, :] = k[:, h * dh:(h + 1) * dh]
            v_ref[0, b, :, h, :] = v[:, h * dh:(h + 1) * dh]


def _memkv(mem, norm_mem, wk, wv):
    depth, d = norm_mem.shape
    nb, n_mem, _ = mem.shape
    dh = d // X_HEADS
    out = jax.ShapeDtypeStruct((depth, nb, n_mem, X_HEADS, dh), F32)
    return pl.pallas_call(
        _memkv_kernel,
        grid=(depth,),
        in_specs=[pl.BlockSpec((nb, n_mem, d), lambda l: (0, 0, 0)),
                  pl.BlockSpec((1, 1, d), lambda l: (l, 0, 0)),
                  pl.BlockSpec((1, d, d), lambda l: (l, 0, 0)),
                  pl.BlockSpec((1, d, d), lambda l: (l, 0, 0))],
        out_specs=[pl.BlockSpec((1, nb, n_mem, X_HEADS, dh), lambda l: (l, 0, 0, 0, 0))] * 2,
        out_shape=[out, out],
        compiler_params=_params(1),
        name="mem_kv",
    )(mem, norm_mem.reshape(depth, 1, d), wk, wv)


N_PROJ_IN = 7
PROJ_GROUPS = 4
N_PROJ_OUT = 10


def _proj_kernel(*refs, transposed, aliased):
    x0_ref, xn_ref, g_ref, w_ref, loglb_ref, log1mlb_ref, omlb_ref = refs[:N_PROJ_IN]
    outs = refs[N_PROJ_IN + 2 * aliased:]
    gcum_ref, qa_ref, ka_ref, ai_ref, ag_ref, q_ref, k_ref, v_ref, bg_ref, pc_ref = outs[:N_PROJ_OUT]
    h_ref = outs[-1]
    i = pl.program_id(0)
    a_end = 4 * A_WIDTH
    b_end = a_end + 4 * B_WIDTH
    tm = xn_ref.shape[0]
    n_groups = PROJ_GROUPS if tm % (PROJ_GROUPS * LANES) == 0 else 1
    groups = [(g * tm // n_groups, tm // n_groups) for g in range(n_groups)]
    sub = lax.broadcasted_iota(jnp.int32, (SUBLANES, A_WIDTH), 0)

    @pl.when(i == 0)
    def _():
        h_ref[...] = _rms(x0_ref[...], g_ref[...]).astype(BF16)

    def step():
        h_next = _rms(xn_ref[...], g_ref[...]).astype(BF16)
        hs = [h_ref[r0:r0 + nr, :] for r0, nr in groups]
        qfs = [_dot(h, w_ref[:, 0:2 * A_WIDTH]) for h in hs]
        for (r0, nr), h, qf in zip(groups, hs, qfs):
            rows = slice(r0, r0 + nr)
            aq, af = qf[:, 0:A_WIDTH], qf[:, A_WIDTH:2 * A_WIDTH]
            e = jnp.exp(-jnp.abs(af))
            ope = 1.0 + e
            logsig = jnp.minimum(af, 0.0) - jnp.log(ope)
            r = 1.0 / ope
            sig_neg = jnp.where(af >= 0, e * r, r)
            la = loglb_ref[...]
            lb_ = log1mlb_ref[...] + logsig
            log_f = jnp.maximum(la, lb_) + jnp.log(1.0 + jnp.exp(-jnp.abs(la - lb_)))
            ka_ref[rows, :] = omlb_ref[...] * sig_neg
            qa_ref[rows, :] = _silu(aq) * (A_HEAD_DIM ** -0.5)

            ig = _dot(h, w_ref[:, 2 * A_WIDTH:a_end])
            ai_ref[rows, :] = ig[:, 0:A_WIDTH].astype(BF16)
            ag_ref[rows, :] = ig[:, A_WIDTH:2 * A_WIDTH]
            pb = _dot(h, w_ref[:, a_end:b_end])
            q_ref[rows, :] = pb[:, 0:B_WIDTH]
            k_ref[rows, :] = pb[:, B_WIDTH:2 * B_WIDTH]
            v_ref[rows, :] = pb[:, 2 * B_WIDTH:3 * B_WIDTH]
            bg_ref[rows, :] = pb[:, 3 * B_WIDTH:4 * B_WIDTH]
            pc_ref[rows, :] = _dot(h, w_ref[:, b_end:])
            if transposed:
                kt_ref, vt_ref = outs[N_PROJ_OUT:N_PROJ_OUT + 2]
                kt_ref[0, 0, :, rows] = pb[:, B_WIDTH:2 * B_WIDTH].T
                vt_ref[0, 0, :, rows] = pb[:, 2 * B_WIDTH:3 * B_WIDTH].T
            for c in range(nr // CHUNK):
                carry = None
                for jv in range(CHUNK // SUBLANES):
                    lo = c * CHUNK + jv * SUBLANES
                    s = log_f[lo:lo + SUBLANES, :]
                    for dist in (1, 2, 4):
                        s = s + jnp.where(sub >= dist, pltpu.roll(s, dist, 0), 0.0)
                    if carry is not None:
                        s = s + carry
                    carry = jnp.broadcast_to(s[SUBLANES - 1:SUBLANES, :], s.shape)
                    gcum_ref[r0 + lo:r0 + lo + SUBLANES, :] = s * LOG2_E

        h_ref[...] = h_next

    step()


def _proj(x2d, g, w_in, lb, seq_len, layer, depth, kv_t):
    n, d = x2d.shape
    d_in = w_in.shape[1]
    tm = 512 if n % 512 == 0 else CHUNK
    transposed = seq_len % tm == 0 and tm % LANES == 0
    aliased = transposed and kv_t is not None
    row = lambda width: pl.BlockSpec((tm, width), lambda i: (i, 0))
    vec = pl.BlockSpec((1, A_WIDTH), lambda i: (0, 0))
    shapes = [A_WIDTH] * 5 + [B_WIDTH] * 4 + [2 * C_WIDTH]
    dtypes = [F32, F32, F32, BF16, F32] + [F32] * 5
    lb = lb.reshape(1, A_WIDTH)
    last = n // tm - 1
    in_specs = [pl.BlockSpec((tm, d), lambda i: (0, 0)),
                pl.BlockSpec((tm, d), lambda i: (jnp.minimum(i + 1, last), 0)),
                pl.BlockSpec((1, d), lambda i: (0, 0)), pl.BlockSpec((d, d_in), lambda i: (0, 0)),
                vec, vec, vec]
    args = [x2d, x2d, g.reshape(1, d), w_in, jnp.log(lb), jnp.log1p(-lb), 1.0 - lb]
    out_specs = [row(w) for w in shapes]
    out_shape = [jax.ShapeDtypeStruct((n, w), t) for w, t in zip(shapes, dtypes)]
    aliases = {}
    if transposed:
        tiles = seq_len // tm
        t_spec = pl.BlockSpec((1, 1, B_WIDTH, tm), lambda i: (layer, i // tiles, 0, i % tiles))
        out_specs += [t_spec, t_spec]
        out_shape += [jax.ShapeDtypeStruct((depth, n // seq_len, B_WIDTH, seq_len), F32)] * 2
        if aliased:
            in_specs += [pl.BlockSpec(memory_space=pl.ANY)] * 2
            args += list(kv_t)
            aliases = {N_PROJ_IN: N_PROJ_OUT, N_PROJ_IN + 1: N_PROJ_OUT + 1}
    res = pl.pallas_call(
        functools.partial(_proj_kernel, transposed=transposed, aliased=aliased),
        grid=(n // tm,),
        in_specs=in_specs,
        out_specs=out_specs,
        out_shape=out_shape,
        scratch_shapes=[pltpu.VMEM((tm, d), BF16)],
        input_output_aliases=aliases,
        compiler_params=_params(1),
        name="in_proj",
    )(*args)
    return res[:N_PROJ_OUT], (tuple(res[N_PROJ_OUT:]) if transposed else None)


def _hgrn_kernel(*refs, nb, nch, aliased):
    g_ref, q_ref, k_ref, v_ref, gate_ref, og_ref, s0_ref = refs[:7]
    a_ref, sout_ref, st_ref = refs[7 + aliased:]
    j = pl.program_id(1)

    @pl.when(j == 0)
    def _():
        for n in range(nb):
            for h in range(A_HEADS):
                st_ref[n, h] = s0_ref[0, n, h].T

    rowc = lax.broadcasted_iota(jnp.int32, (CHUNK, CHUNK), 0)
    colc = lax.broadcasted_iota(jnp.int32, (CHUNK, CHUNK), 1)
    col_sub = colc // SUB
    row_k = lax.broadcasted_iota(jnp.int32, (CHUNK, A_HEAD_DIM), 0)
    lane_d = lax.broadcasted_iota(jnp.int32, (SUB, CHUNK), 1)
    n_sub = CHUNK // SUB
    head = [slice(h * A_HEAD_DIM, (h + 1) * A_HEAD_DIM) for h in range(A_HEADS)]

    def sub_starts(g_all):
        return [jnp.zeros((1, A_WIDTH), F32)] + [g_all[SUB * s - 1:SUB * s, :] for s in range(1, n_sub)]

    span = jnp.zeros((1, A_WIDTH), F32)
    for n in range(nb):
        for c in range(nch):
            for s in range(n_sub):
                last = c * CHUNK + SUB * s + SUB - 1
                end = g_ref[n, last:last + 1, :]
                span = jnp.maximum(span, -end if s == 0 else g_ref[n, last - SUB:last - SUB + 1, :] - end)
    small_decay = jnp.max(span) < SAFE_SPAN * LOG2_E

    def att_factored(g_all, q_all, k_all, qg_all):
        starts = sub_starts(g_all)
        operands = []
        for sl in head:
            gh, qh, kh = g_all[:, sl], q_all[:, sl], k_all[:, sl]
            parts, krows = [qg_all[:, sl]], []
            for s in range(n_sub):
                rs = slice(SUB * s, SUB * (s + 1))
                st = starts[s][:, sl]
                krows.append(kh[rs] * jnp.exp2(st - gh[rs]))
                if s:
                    parts.append(qh[SUB * s:, :] * jnp.exp2(gh[SUB * s:, :] - st))
            operands.append((jnp.concatenate(parts, axis=0).astype(BF16),
                             jnp.concatenate(krows, axis=0).astype(BF16)))
        rrs = [lax.dot_general(qr, kr, NT_DIMS, preferred_element_type=F32) for qr, kr in operands]
        atts = []
        for rr in rrs:
            att = rr[0:CHUNK, :]
            r0 = CHUNK
            for s in range(1, n_sub):
                rows = CHUNK - SUB * s
                block = jnp.concatenate([jnp.zeros((SUB * s, CHUNK), F32), rr[r0:r0 + rows, :]], axis=0)
                att = jnp.where(col_sub == s, block, att)
                r0 += rows
            atts.append(jnp.where(rowc >= colc, att, 0.0))
        return atts

    def att_exact(g_all, q_all, k_all, qg_all):
        def one_head(sl):
            gh, qh, kh = g_all[:, sl], q_all[:, sl], k_all[:, sl]
            parts, refrows = [], []
            for jsub in range(n_sub):
                ref = gh[SUB * jsub + SUB - 1:SUB * jsub + SUB, :]
                refrows.append(jnp.broadcast_to(ref, (SUB, A_HEAD_DIM)))
                if jsub < n_sub - 1:
                    later = row_k >= SUB * (jsub + 1)
                    parts.append(jnp.where(later, qh * jnp.exp2(jnp.minimum(gh - ref, 0.0)), 0.0))
            q_rel = jnp.concatenate(parts, axis=0).astype(BF16)
            k_rel = (kh * jnp.exp2(jnp.minimum(jnp.concatenate(refrows, axis=0) - gh, 0.0))).astype(BF16)
            rr = lax.dot_general(q_rel, k_rel, NT_DIMS, preferred_element_type=F32)
            att = jnp.zeros((CHUNK, CHUNK), F32)
            for jsub in range(n_sub - 1):
                att = jnp.where(col_sub == jsub, rr[jsub * CHUNK:(jsub + 1) * CHUNK, :], att)

            dparts = []
            for csub in range(n_sub):
                rs = slice(SUB * csub, SUB * (csub + 1))
                gc, qc, kc = gh[rs], qh[rs], kh[rs]
                accd = jnp.zeros((SUB, CHUNK), F32)
                for s in range(SUB):
                    dec = jnp.exp2(jnp.minimum(gc - gc[s:s + 1, :], 0.0))
                    col = jnp.sum(dec * qc * kc[s:s + 1, :], axis=-1, keepdims=True)
                    accd = jnp.where(lane_d == SUB * csub + s, col, accd)
                dparts.append(accd)
            return att + jnp.where(rowc >= colc, jnp.concatenate(dparts, axis=0), 0.0)

        return [one_head(sl) for sl in head]

    def chunk(n, r0, sts, att_fn):
        g_all = g_ref[n, pl.ds(r0, CHUNK), :]
        q_all = q_ref[n, pl.ds(r0, CHUNK), :]
        k_all = k_ref[n, pl.ds(r0, CHUNK), :]
        v_all = v_ref[n, pl.ds(r0, CHUNK), :]
        ag = gate_ref[n, pl.ds(r0, CHUNK), :]
        qg_all = q_all * jnp.exp2(g_all)
        gl_all = g_all[CHUNK - 1:CHUNK, :]
        kd_all = (k_all * jnp.exp2(gl_all - g_all)).astype(BF16)
        qg_bf = qg_all.astype(BF16)
        atts = att_fn(g_all, q_all, k_all, qg_all)

        o_state = [lax.dot_general(qg_bf[:, sl], st.astype(BF16), NT_DIMS, preferred_element_type=F32)
                   for sl, st in zip(head, sts)]
        o_pairs = [_dot(att.astype(BF16), v_all[:, sl]) for sl, att in zip(head, atts)]
        updates = [lax.dot_general(v_all[:, sl], kd_all[:, sl], TN_DIMS, preferred_element_type=F32)
                   for sl in head]
        new_sts = [st * jnp.exp2(gl_all[:, sl]) + up for sl, st, up in zip(head, sts, updates)]
        o = jnp.concatenate([a + b for a, b in zip(o_state, o_pairs)], axis=1)
        sq = o * o
        inv = jnp.concatenate(
            [jnp.broadcast_to(lax.rsqrt(jnp.mean(sq[:, sl], axis=-1, keepdims=True) + EPS), (CHUNK, A_HEAD_DIM))
             for sl in head], axis=1)
        a_ref[n, pl.ds(r0, CHUNK), :] = (o * inv * og_ref[...] * _silu(ag)).astype(BF16)
        return new_sts

    @pl.when(small_decay)
    def _():
        for n in range(nb):
            sts = [st_ref[n, h] for h in range(A_HEADS)]
            for c in range(nch):
                sts = chunk(n, c * CHUNK, sts, att_factored)
            for h in range(A_HEADS):
                st_ref[n, h] = sts[h]

    @pl.when(jnp.logical_not(small_decay))
    def _():
        for n in range(nb):
            def body(c, carry):
                sts = chunk(n, pl.multiple_of(c * CHUNK, CHUNK), [st_ref[n, h] for h in range(A_HEADS)], att_exact)
                for h in range(A_HEADS):
                    st_ref[n, h] = sts[h]
                return carry
            lax.fori_loop(0, nch, body, 0)

    @pl.when(j == pl.num_programs(1) - 1)
    def _():
        for n in range(nb):
            for h in range(A_HEADS):
                sout_ref[0, n, h] = st_ref[n, h].T


def _hgrn(gates, onorm_g, s0, s0_layer, layer, depth, s_all):
    b, l, _ = gates[0].shape
    tl = HGRN_ROWS if l % HGRN_ROWS == 0 else CHUNK
    nb = max(1, min(b, HGRN_ROWS // tl, HGRN_SEQS))
    while b % nb:
        nb -= 1
    tile = pl.BlockSpec((nb, tl, A_WIDTH), lambda i, j: (i, j, 0))
    st_block = (1, nb, A_HEADS, A_HEAD_DIM, A_HEAD_DIM)
    aliased = s_all is not None
    in_specs = [tile] * 5 + [pl.BlockSpec((1, A_WIDTH), lambda i, j: (0, 0)),
                             pl.BlockSpec(st_block, lambda i, j: (s0_layer, i, 0, 0, 0))]
    args = list(gates) + [onorm_g.reshape(1, A_WIDTH), s0]
    if aliased:
        in_specs.append(pl.BlockSpec(memory_space=pl.ANY))
        args.append(s_all)
    return pl.pallas_call(
        functools.partial(_hgrn_kernel, nb=nb, nch=tl // CHUNK, aliased=aliased),
        grid=(b // nb, l // tl),
        in_specs=in_specs,
        out_specs=[pl.BlockSpec((nb, tl, A_WIDTH), lambda i, j: (i, j, 0)),
                   pl.BlockSpec(st_block, lambda i, j: (layer, i, 0, 0, 0))],
        out_shape=[jax.ShapeDtypeStruct((b, l, A_WIDTH), BF16),
                   jax.ShapeDtypeStruct((depth, b, A_HEADS, A_HEAD_DIM, A_HEAD_DIM), F32)],
        scratch_shapes=[pltpu.VMEM((nb, A_HEADS, A_HEAD_DIM, A_HEAD_DIM), F32)],
        input_output_aliases={7: 1} if aliased else {},
        compiler_params=_params(2),
        name="hgrn2",
    )(*args)


def _sb_kernel(*refs, ns, lq, n_past):
    if n_past:
        q_ref, k_ref, v_ref, g_ref, pk_ref, pv_ref, o_ref = refs[:7]
        scratch = refs[7:]
    else:
        q_ref, k_ref, v_ref, g_ref, o_ref = refs[:5]
        pk_ref = pv_ref = None
        scratch = refs[5:]
    kc_ref, vtc_ref, qtm_ref, carry_ref, acc_ref = scratch
    j = pl.program_id(1)
    kb_rows = KEY_BLOCK
    run_len = kb_rows // SUBLANES
    width = B_WIDTH
    streams = range(ns)

    row = lax.broadcasted_iota(jnp.int32, (kb_rows, kb_rows), 0)
    lane = lax.broadcasted_iota(jnp.int32, (kb_rows, kb_rows), 1)
    key_of_row = (row % SUBLANES) * run_len + row // SUBLANES
    causal = key_of_row < lane
    perm = (lane == key_of_row).astype(F32)
    sub = lax.broadcasted_iota(jnp.int32, (SUBLANES, kb_rows), 0)

    def padded(rows):
        if rows.shape[0] == kb_rows:
            return rows
        return jnp.concatenate([rows, jnp.zeros((kb_rows - rows.shape[0], rows.shape[1]), rows.dtype)], axis=0)

    row_head = lax.broadcasted_iota(jnp.int32, (width, kb_rows), 0) // B_HEAD_DIM
    k_perm = [_dot(perm, padded(k_ref[s])) for s in streams]
    v_perm = [_dot(perm, padded(v_ref[s])) for s in streams]
    for s in streams:
        kc_ref[s, j] = k_perm[s].astype(BF16)
        vtc_ref[s, j] = v_perm[s].T.astype(BF16)
        qt = padded(q_ref[s] * (0.5 * B_HEAD_DIM ** -0.5)).T
        qtm_ref[s] = jnp.concatenate([jnp.where(row_head == h, qt, 0.0) for h in range(B_HEADS)],
                                     axis=1).astype(BF16)

    carry_ref[...] = jnp.ones(carry_ref.shape, F32)
    acc_ref[...] = jnp.zeros(acc_ref.shape, F32)

    def process(blocks, masked):
        zts = [_dot(kb, qtm_ref[s]) for s, (kb, _) in enumerate(blocks)]
        ws = []
        for s, zt in enumerate(zts):
            for h in range(B_HEADS):
                ls = slice(h * kb_rows, (h + 1) * kb_rows)
                half_t = 0.5 * jnp.tanh(zt[:, ls])
                beta = 0.5 + half_t
                omb = 0.5 - half_t
                if masked:
                    beta = jnp.where(causal, beta, 0.0)
                    omb = jnp.where(causal, omb, 1.0)
                run = jnp.ones((SUBLANES, kb_rows), F32)
                excl = [None] * run_len
                for jj in reversed(range(run_len)):
                    excl[jj] = run
                    run = run * omb[jj * SUBLANES:(jj + 1) * SUBLANES]
                inc = run
                for d in (1, 2, 4):
                    inc = jnp.where(sub + d < SUBLANES, inc * pltpu.roll(inc, SUBLANES - d, 0), inc)
                carry = carry_ref[s, :, ls]
                off = jnp.where(sub < SUBLANES - 1, pltpu.roll(inc, SUBLANES - 1, 0), 1.0) * carry
                ws.append(jnp.concatenate([beta[jj * SUBLANES:(jj + 1) * SUBLANES] * (excl[jj] * off)
                                           for jj in range(run_len)], axis=0).astype(BF16))
                carry_ref[s, :, ls] = carry * jnp.broadcast_to(inc[0:1, :], (SUBLANES, kb_rows))
        for s, (_, vtb) in enumerate(blocks):
            for h in range(B_HEADS):
                hs = slice(h * B_HEAD_DIM, (h + 1) * B_HEAD_DIM)
                acc_ref[s, hs, :] += _dot(vtb[hs, :], ws[s * B_HEADS + h])

    def alive():
        return (jnp.max(carry_ref[...]) > 0.0).astype(jnp.int32)

    process([(kc_ref[s, j], vtc_ref[s, j]) for s in streams], True)

    def cond(state):
        kb, live = state
        return jnp.logical_and(kb >= 0, live > 0)

    def body(state):
        kb, _ = state
        process([(kc_ref[s, kb], vtc_ref[s, kb]) for s in streams], False)
        return kb - 1, alive()

    lax.while_loop(cond, body, (j - 1, alive()))

    for kb in reversed(range(n_past)):
        @pl.when(alive() > 0)
        def _():
            cols = slice(kb * kb_rows, (kb + 1) * kb_rows)
            blocks = []
            for s in streams:
                kt = pk_ref[0, s, :, cols]
                vt = pv_ref[0, s, :, cols]
                blocks.append((lax.dot_general(perm, kt, NT_DIMS, preferred_element_type=F32).astype(BF16),
                               lax.dot_general(vt, perm, NT_DIMS, preferred_element_type=F32).astype(BF16)))
            process(blocks, False)

    for s in streams:
        o = acc_ref[s].T[0:lq, :]
        o_ref[s] = (o * _silu(g_ref[s])).astype(BF16)


def _sb(q, k, v, g, past_t, layer):
    b, l, w = q.shape
    lq = min(KEY_BLOCK, l)
    nblk = l // lq
    ns = min(b, SB_SEQS)
    while b % ns:
        ns -= 1
    p = 0 if past_t is None else past_t[0].shape[3]
    blk = pl.BlockSpec((ns, lq, w), lambda i, j: (i, j, 0))
    in_specs = [blk] * 4
    args = [q, k, v, g]
    if p:
        in_specs += [pl.BlockSpec((1, ns, w, p), lambda i, j: (layer, i, 0, 0))] * 2
        args += list(past_t)
    return pl.pallas_call(
        functools.partial(_sb_kernel, ns=ns, lq=lq, n_past=p // KEY_BLOCK),
        grid=(b // ns, nblk),
        in_specs=in_specs,
        out_specs=blk,
        out_shape=jax.ShapeDtypeStruct((b, l, w), BF16),
        scratch_shapes=[pltpu.VMEM((ns, nblk, KEY_BLOCK, w), BF16),
                        pltpu.VMEM((ns, nblk, w, KEY_BLOCK), BF16),
                        pltpu.VMEM((ns, w, B_HEADS * KEY_BLOCK), BF16),
                        pltpu.VMEM((ns, SUBLANES, B_HEADS * KEY_BLOCK), F32),
                        pltpu.VMEM((ns, w, KEY_BLOCK), F32)],
        compiler_params=_params(2),
        name="sb_attn",
    )(*args)


def _post_kernel(x_ref, a_ref, b_ref, pc_ref, pprev_ref, mk_hbm, mv_hbm, wpool_ref, pscale_ref,
                 woa_ref, wob_ref, woc_ref, npost_ref, nxpre_ref, wxq_ref, wxo_ref, nxpost_ref,
                 xo_ref, pnew_ref, ext_ref, sum_ref, kv_ref, mkb_ref, mvb_ref, sem_ref,
                 *, nb, tl, n_split, past, layer):
    i = pl.program_id(0)
    j = pl.program_id(1)
    n_groups = pl.num_programs(0)
    halo = POOL_STATE + 1
    lead = SUBLANES
    dh = kv_ref.shape[-1]

    def kv_copies(group, slot):
        return [pltpu.make_async_copy(src.at[layer, group * nb + n, :, h, :], kv_ref.at[slot, t, n, h],
                                      sem_ref.at[slot, t])
                for t, src in enumerate((mk_hbm, mv_hbm)) for n in range(nb) for h in range(X_HEADS)]

    @pl.when(j == 0)
    def _():
        slot = i % 2

        @pl.when(i == 0)
        def _():
            for cp in kv_copies(0, 0):
                cp.start()

        @pl.when(i + 1 < n_groups)
        def _():
            for cp in kv_copies(i + 1, 1 - slot):
                cp.start()

        for cp in kv_copies(i, slot):
            cp.wait()
        for n in range(nb):
            ext_ref[n, 0:lead, :] = jnp.zeros((lead, C_WIDTH), F32)
            ext_ref[n, lead:lead + halo, :] = pprev_ref[n]
            for gi in range(len(POOL_WINDOWS) - 1):
                sum_ref[gi, n, 0:lead, :] = jnp.zeros((lead, C_WIDTH), F32)
            for h in range(X_HEADS):
                mkb_ref[n, :, h * dh:(h + 1) * dh] = kv_ref[slot, 0, n, h].astype(BF16)
                mvb_ref[n, :, h * dh:(h + 1) * dh] = kv_ref[slot, 1, n, h].astype(BF16)

    grp = lax.broadcasted_iota(jnp.int32, (tl, C_WIDTH), 1) // C_GROUP_DIM
    pos = past + j * tl + lax.broadcasted_iota(jnp.int32, (tl, C_WIDTH), 0)
    dds = []
    rows = halo + tl
    for n in range(nb):
        cu = pc_ref[n, :, 0:C_WIDTH]
        ext_ref[n, lead + halo:lead + rows, :] = cu
        win = jnp.zeros((tl, C_WIDTH), F32)
        wlen = jnp.zeros((tl, C_WIDTH), jnp.int32)
        src = ext_ref.at[n]
        for gi, w in enumerate(POOL_WINDOWS):
            assert w == 2 ** (gi + 1)
            wsum = src[lead:lead + rows, :] + src[lead - w // 2:lead - w // 2 + rows, :]
            if gi + 1 < len(POOL_WINDOWS):
                sum_ref[gi, n, lead:lead + rows, :] = wsum
                src = sum_ref.at[gi, n]
            win = jnp.where(grp == gi, wsum[halo:, :], win)
            wlen = jnp.where(grp == gi, w, wlen)
        cnt = jnp.minimum(pos + 1, wlen).astype(F32)
        dds.append((win / cnt - cu).astype(BF16))
        pnew_ref[n] = ext_ref[n, lead + tl + 1:lead + tl + halo, :]
        ext_ref[n, lead:lead + halo, :] = ext_ref[n, lead + tl:lead + tl + halo, :]

    seg = tl // n_split if nb == 1 else tl
    groups = [[(0, g * seg)] for g in range(n_split)] if nb == 1 else \
             [[(n, 0) for n in range(g * nb // n_split, (g + 1) * nb // n_split)] for g in range(n_split)]

    def rows(members, ref, w0=0, w1=None):
        return jnp.concatenate([ref[n, r0:r0 + seg, w0:w1] for n, r0 in members], axis=0)

    mixes = []
    for members in groups:
        dd = jnp.concatenate([dds[n][r0:r0 + seg] for n, r0 in members], axis=0)
        y = _dot(dd, wpool_ref[...]) * pscale_ref[...]
        c_out = (y * _silu(rows(members, pc_ref, C_WIDTH, 2 * C_WIDTH))).astype(BF16)
        mixes.append(_dot(rows(members, a_ref), woa_ref[...]) + _dot(rows(members, b_ref), wob_ref[...])
                     + _dot(c_out, woc_ref[...]))
    x1s = [rows(members, x_ref) + _rms(mix, npost_ref[...]) for members, mix in zip(groups, mixes)]

    qs = [_dot(_rms(x1, nxpre_ref[...]).astype(BF16), wxq_ref[...]).astype(BF16) for x1 in x1s]
    pairs = [(g, m, n, h) for g, members in enumerate(groups) for m, (n, _) in enumerate(members)
             for h in range(X_HEADS)]
    scores = [lax.dot_general(qs[g][m * seg:(m + 1) * seg, h * dh:(h + 1) * dh],
                              mkb_ref[n, :, h * dh:(h + 1) * dh], NT_DIMS,
                              preferred_element_type=F32) * (dh ** -0.5) for g, m, n, h in pairs]
    probs = []
    for s in scores:
        ex = jnp.exp(s - jnp.max(s, axis=-1, keepdims=True))
        probs.append((ex / jnp.sum(ex, axis=-1, keepdims=True)).astype(BF16))
    ctx = [_dot(p, mvb_ref[n, :, h * dh:(h + 1) * dh]).astype(BF16) for p, (g, m, n, h) in zip(probs, pairs)]
    xos = []
    for g, members in enumerate(groups):
        seqs = [jnp.concatenate([c for c, (g2, m2, _, _) in zip(ctx, pairs) if g2 == g and m2 == m], axis=1)
                for m in range(len(members))]
        xos.append(_dot(jnp.concatenate(seqs, axis=0), wxo_ref[...]))
    for members, x1, xo in zip(groups, x1s, xos):
        res = x1 + _rms(xo, nxpost_ref[...])
        for m, (n, r0) in enumerate(members):
            xo_ref[n, r0:r0 + seg, :] = res[m * seg:(m + 1) * seg]


def _post(x, a_out, b_out, pc, pool_prev, mk, mv, wts, past, layer):
    b, l, d = x.shape
    n_mem, dh = mk.shape[2], mk.shape[4]
    tl = POST_ROWS if l % POST_ROWS == 0 else CHUNK
    nb = max(1, min(b, POST_ROWS // tl, POST_SEQS))
    while b % nb:
        nb -= 1
    n_split = 2 if (nb * tl) % (2 * LANES) == 0 and (nb == 1 or nb % 2 == 0) else 1
    halo = POOL_STATE + 1
    pprev = jnp.concatenate([jnp.zeros((b, 1, C_WIDTH), F32), pool_prev], axis=1)
    tile = lambda w: pl.BlockSpec((nb, tl, w), lambda i, j: (i, j, 0))
    per_seq = lambda r, w: pl.BlockSpec((nb, r, w), lambda i, j: (i, 0, 0))
    full = lambda a: pl.BlockSpec(a.shape, lambda i, j: (0,) * a.ndim)
    hbm = pl.BlockSpec(memory_space=pl.ANY)
    return pl.pallas_call(
        functools.partial(_post_kernel, nb=nb, tl=tl, n_split=n_split, past=past, layer=layer),
        grid=(b // nb, l // tl),
        in_specs=[tile(d), tile(A_WIDTH), tile(B_WIDTH), tile(2 * C_WIDTH), per_seq(halo, C_WIDTH),
                  hbm, hbm] + [full(a) for a in wts],
        out_specs=[tile(d), per_seq(POOL_STATE, C_WIDTH)],
        out_shape=[jax.ShapeDtypeStruct((b, l, d), F32),
                   jax.ShapeDtypeStruct((b, POOL_STATE, C_WIDTH), F32)],
        scratch_shapes=[pltpu.VMEM((nb, SUBLANES + halo + tl, C_WIDTH), F32),
                        pltpu.VMEM((len(POOL_WINDOWS) - 1, nb, SUBLANES + halo + tl, C_WIDTH), F32),
                        pltpu.VMEM((2, 2, nb, X_HEADS, n_mem, dh), F32),
                        pltpu.VMEM((nb, n_mem, d), BF16),
                        pltpu.VMEM((nb, n_mem, d), BF16),
                        pltpu.SemaphoreType.DMA((2, 2))],
        compiler_params=_params(2),
        name="mix_out",
    )(x, a_out, b_out, pc, pprev, mk, mv, *wts)


def _layer(x, mk, mv, past_t, s0, s0_layer, pool_prev, lw, layer, depth, chain):
    kv_t, s_all = chain
    b, l, d = x.shape
    n = b * l
    proj, kv_t = _proj(x.reshape(n, d), lw["n_pre"], lw["w_in"], lw["lb"], l, layer, depth, kv_t)
    seq = lambda t: t.reshape(b, l, -1)
    gates, (bq, bk, bv, bg, pc) = [seq(t) for t in proj[:5]], proj[5:]
    a_out, s_all = _hgrn(gates, lw["onorm_g"], s0, s0_layer, layer, depth, s_all)
    b_out = _sb(seq(bq), seq(bk), seq(bv), seq(bg), past_t, layer)
    past = 0 if past_t is None else past_t[0].shape[3]
    x_new, pool_new = _post(x, a_out, b_out, seq(pc), pool_prev, mk, mv, lw["post"], past, layer)
    heads = lambda t: t.reshape(b, l, B_HEADS, B_HEAD_DIM)
    return x_new, heads(bk), heads(bv), (kv_t, s_all), pool_new


def kernel(x_prompt, x_sample, mem_prompt, cache_sb_k, cache_sb_v, state_hgrn, state_pool, cache_mem_k, cache_mem_v, norm_mix_pre, norm_mix_post, w_in, hgrn_lb_logits, hgrn_onorm_g, w_pool, pool_scale, w_out, norm_x_pre, norm_x_post, norm_mem, w_xq, w_xk, w_xv, w_xo):
    depth, d = norm_mix_pre.shape
    bp = x_prompt.shape[0]
    bs, past = cache_sb_k.shape[1], cache_sb_k.shape[2]

    lbs = jax.nn.softmax(hgrn_lb_logits.astype(F32), axis=0)
    lower_bounds = jnp.maximum(jnp.cumsum(lbs, axis=0) - lbs[0], 0.0)

    mk_all, mv_all = _memkv(mem_prompt, norm_mem, w_xk.astype(BF16), w_xv.astype(BF16))

    def seq_minor(c):
        return jnp.transpose(c, (0, 1, 3, 4, 2)).reshape(depth, bs, B_WIDTH, past)

    def seq_major(t):
        dp, b, _, l = t.shape
        return jnp.transpose(t.reshape(dp, b, B_HEADS, B_HEAD_DIM, l), (0, 1, 4, 2, 3))

    past_t = (seq_minor(cache_sb_k), seq_minor(cache_sb_v))

    row = lambda v: v.reshape(1, -1)
    s_zero = jnp.zeros((1, bp, A_HEADS, A_HEAD_DIM, A_HEAD_DIM), F32)
    pool_zero = jnp.zeros((bp, POOL_STATE, C_WIDTH), F32)

    xp, xs = x_prompt, x_sample
    p_kn, p_vn, p_pools, s_kn, s_vn, s_pools = [], [], [], [], [], []
    chain_p = chain_s = (None, None)
    for l in range(depth):
        wp = jnp.zeros((C_WIDTH, C_WIDTH), F32)
        for gi in range(C_GROUPS):
            gs = slice(gi * C_GROUP_DIM, (gi + 1) * C_GROUP_DIM)
            wp = wp.at[gs, gs].set(w_pool[l, gi])
        wo = w_out[l].astype(BF16)
        lw = {
            "n_pre": norm_mix_pre[l], "w_in": w_in[l].astype(BF16), "lb": lower_bounds[l],
            "onorm_g": hgrn_onorm_g[l],
            "post": (wp.astype(BF16), row(pool_scale[l]), wo[0:A_WIDTH], wo[A_WIDTH:A_WIDTH + B_WIDTH],
                     wo[A_WIDTH + B_WIDTH:], row(norm_mix_post[l]), row(norm_x_pre[l]),
                     w_xq[l].astype(BF16), w_xo[l].astype(BF16), row(norm_x_post[l])),
        }
        xp, kb, vb, chain_p, pn = _layer(xp, mk_all, mv_all, None, s_zero, 0, pool_zero, lw, l, depth, chain_p)
        p_kn.append(kb); p_vn.append(vb); p_pools.append(pn)
        xs, kb, vb, chain_s, pn = _layer(xs, cache_mem_k, cache_mem_v, past_t, state_hgrn, l, state_pool[l],
                                         lw, l, depth, chain_s)
        s_kn.append(kb); s_vn.append(vb); s_pools.append(pn)

    def new_kv(kvt, nat_k, nat_v):
        if kvt is not None:
            return seq_major(kvt[0]), seq_major(kvt[1])
        return jnp.stack(nat_k), jnp.stack(nat_v)

    p_k, p_v = new_kv(chain_p[0], p_kn, p_vn)
    s_k, s_v = new_kv(chain_s[0], s_kn, s_vn)
    return (xp, xs, p_k, p_v, chain_p[1], jnp.stack(p_pools), mk_all, mv_all, s_k, s_v, chain_s[1],
            jnp.stack(s_pools))
```

```python
import functools

import jax
import jax.numpy as jnp
from jax import lax
from jax.experimental import pallas as pl
from jax.experimental.pallas import tpu as pltpu

F32 = jnp.float32
BF16 = jnp.bfloat16
EPS = 1e-6

A_HEADS, A_HEAD_DIM = 4, 128
A_WIDTH = A_HEADS * A_HEAD_DIM
B_HEADS, B_HEAD_DIM = 4, 64
B_WIDTH = B_HEADS * B_HEAD_DIM
C_GROUPS, C_GROUP_DIM = 4, 64
C_WIDTH = C_GROUPS * C_GROUP_DIM
POOL_WINDOWS = (2, 4, 8, 16)
POOL_STATE = 15
X_HEADS = 4
CHUNK = 64
SUB = 16
LOG2_E = 1.4426950408889634
SAFE_SPAN = 40.0
KEY_BLOCK = 128
SB_SEQS = 4
HGRN_ROWS = 1024
HGRN_SEQS = 8
POST_ROWS = 512
POST_SEQS = 4
SUBLANES = 8
LANES = 128
VMEM_LIMIT = 56 * 1024 * 1024

NT_DIMS = (((1,), (1,)), ((), ()))
TN_DIMS = (((0,), (0,)), ((), ()))


def _rms(x, g):
    ms = jnp.mean(x * x, axis=-1, keepdims=True)
    return x * lax.rsqrt(ms + EPS) * g


def _silu(x):
    return x * (1.0 / (1.0 + jnp.exp(-x)))


def _dot(a, b):
    return jnp.dot(a, b, preferred_element_type=F32)


def _params(n_grid, flags=None):
    return pltpu.CompilerParams(dimension_semantics=("arbitrary",) * n_grid,
                                vmem_limit_bytes=VMEM_LIMIT, flags=flags)


def _memkv_kernel(mem_ref, g_ref, wk_ref, wv_ref, k_ref, v_ref):
    nb, n_mem, d = mem_ref.shape
    dh = d // X_HEADS
    for b in range(nb):
        m = _rms(mem_ref[b], g_ref[0]).astype(BF16)
        k = _dot(m, wk_ref[0])
        v = _dot(m, wv_ref[0])
        for h in range(X_HEADS):
            k_ref[0, b, :, h, :] = k[:, h * dh:(h + 1) * dh]
            v_ref[0, b, :, h, :] = v[:, h * dh:(h + 1) * dh]


def _memkv(mem, norm_mem, wk, wv):
    depth, d = norm_mem.shape
    nb, n_mem, _ = mem.shape
    dh = d // X_HEADS
    out = jax.ShapeDtypeStruct((depth, nb, n_mem, X_HEADS, dh), F32)
    return pl.pallas_call(
        _memkv_kernel,
        grid=(depth,),
        in_specs=[pl.BlockSpec((nb, n_mem, d), lambda l: (0, 0, 0)),
                  pl.BlockSpec((1, 1, d), lambda l: (l, 0, 0)),
                  pl.BlockSpec((1, d, d), lambda l: (l, 0, 0)),
                  pl.BlockSpec((1, d, d), lambda l: (l, 0, 0))],
        out_specs=[pl.BlockSpec((1, nb, n_mem, X_HEADS, dh), lambda l: (l, 0, 0, 0, 0))] * 2,
        out_shape=[out, out],
        compiler_params=_params(1),
        name="mem_kv",
    )(mem, norm_mem.reshape(depth, 1, d), wk, wv)


N_PROJ_IN = 7
PROJ_GROUPS = 4
N_PROJ_OUT = 10


def _proj_kernel(*refs, transposed, aliased):
    x0_ref, xn_ref, g_ref, w_ref, loglb_ref, log1mlb_ref, omlb_ref = refs[:N_PROJ_IN]
    outs = refs[N_PROJ_IN + 2 * aliased:]
    gcum_ref, qa_ref, ka_ref, ai_ref, ag_ref, q_ref, k_ref, v_ref, bg_ref, pc_ref = outs[:N_PROJ_OUT]
    h_ref = outs[-1]
    i = pl.program_id(0)
    a_end = 4 * A_WIDTH
    b_end = a_end + 4 * B_WIDTH
    tm = xn_ref.shape[0]
    n_groups = PROJ_GROUPS if tm % (PROJ_GROUPS * LANES) == 0 else 1
    groups = [(g * tm // n_groups, tm // n_groups) for g in range(n_groups)]
    sub = lax.broadcasted_iota(jnp.int32, (SUBLANES, A_WIDTH), 0)

    @pl.when(i == 0)
    def _():
        h_ref[...] = _rms(x0_ref[...], g_ref[...]).astype(BF16)

    def step():
        h_next = _rms(xn_ref[...], g_ref[...]).astype(BF16)
        hs = [h_ref[r0:r0 + nr, :] for r0, nr in groups]
        qfs = [_dot(h, w_ref[:, 0:2 * A_WIDTH]) for h in hs]
        for (r0, nr), h, qf in zip(groups, hs, qfs):
            rows = slice(r0, r0 + nr)
            aq, af = qf[:, 0:A_WIDTH], qf[:, A_WIDTH:2 * A_WIDTH]
            e = jnp.exp(-jnp.abs(af))
            ope = 1.0 + e
            logsig = jnp.minimum(af, 0.0) - jnp.log(ope)
            r = 1.0 / ope
            sig_neg = jnp.where(af >= 0, e * r, r)
            la = loglb_ref[...]
            lb_ = log1mlb_ref[...] + logsig
            log_f = jnp.maximum(la, lb_) + jnp.log(1.0 + jnp.exp(-jnp.abs(la - lb_)))
            ka_ref[rows, :] = omlb_ref[...] * sig_neg
            qa_ref[rows, :] = _silu(aq) * (A_HEAD_DIM ** -0.5)

            ig = _dot(h, w_ref[:, 2 * A_WIDTH:a_end])
            ai_ref[rows, :] = ig[:, 0:A_WIDTH].astype(BF16)
            ag_ref[rows, :] = ig[:, A_WIDTH:2 * A_WIDTH]
            pb = _dot(h, w_ref[:, a_end:b_end])
            q_ref[rows, :] = pb[:, 0:B_WIDTH]
            k_ref[rows, :] = pb[:, B_WIDTH:2 * B_WIDTH]
            v_ref[rows, :] = pb[:, 2 * B_WIDTH:3 * B_WIDTH]
            bg_ref[rows, :] = pb[:, 3 * B_WIDTH:4 * B_WIDTH]
            pc_ref[rows, :] = _dot(h, w_ref[:, b_end:])
            if transposed:
                kt_ref, vt_ref = outs[N_PROJ_OUT:N_PROJ_OUT + 2]
                kt_ref[0, 0, :, rows] = pb[:, B_WIDTH:2 * B_WIDTH].T
                vt_ref[0, 0, :, rows] = pb[:, 2 * B_WIDTH:3 * B_WIDTH].T
            for c in range(nr // CHUNK):
                carry = None
                for jv in range(CHUNK // SUBLANES):
                    lo = c * CHUNK + jv * SUBLANES
                    s = log_f[lo:lo + SUBLANES, :]
                    for dist in (1, 2, 4):
                        s = s + jnp.where(sub >= dist, pltpu.roll(s, dist, 0), 0.0)
                    if carry is not None:
                        s = s + carry
                    carry = jnp.broadcast_to(s[SUBLANES - 1:SUBLANES, :], s.shape)
                    gcum_ref[r0 + lo:r0 + lo + SUBLANES, :] = s * LOG2_E

        h_ref[...] = h_next

    step()


def _proj(x2d, g, w_in, lb, seq_len, layer, depth, kv_t):
    n, d = x2d.shape
    d_in = w_in.shape[1]
    tm = 512 if n % 512 == 0 else CHUNK
    transposed = seq_len % tm == 0 and tm % LANES == 0
    aliased = transposed and kv_t is not None
    row = lambda width: pl.BlockSpec((tm, width), lambda i: (i, 0))
    vec = pl.BlockSpec((1, A_WIDTH), lambda i: (0, 0))
    shapes = [A_WIDTH] * 5 + [B_WIDTH] * 4 + [2 * C_WIDTH]
    dtypes = [F32, F32, F32, BF16, F32] + [F32] * 5
    lb = lb.reshape(1, A_WIDTH)
    last = n // tm - 1
    in_specs = [pl.BlockSpec((tm, d), lambda i: (0, 0)),
                pl.BlockSpec((tm, d), lambda i: (jnp.minimum(i + 1, last), 0)),
                pl.BlockSpec((1, d), lambda i: (0, 0)), pl.BlockSpec((d, d_in), lambda i: (0, 0)),
                vec, vec, vec]
    args = [x2d, x2d, g.reshape(1, d), w_in, jnp.log(lb), jnp.log1p(-lb), 1.0 - lb]
    out_specs = [row(w) for w in shapes]
    out_shape = [jax.ShapeDtypeStruct((n, w), t) for w, t in zip(shapes, dtypes)]
    aliases = {}
    if transposed:
        tiles = seq_len // tm
        t_spec = pl.BlockSpec((1, 1, B_WIDTH, tm), lambda i: (layer, i // tiles, 0, i % tiles))
        out_specs += [t_spec, t_spec]
        out_shape += [jax.ShapeDtypeStruct((depth, n // seq_len, B_WIDTH, seq_len), F32)] * 2
        if aliased:
            in_specs += [pl.BlockSpec(memory_space=pl.ANY)] * 2
            args += list(kv_t)
            aliases = {N_PROJ_IN: N_PROJ_OUT, N_PROJ_IN + 1: N_PROJ_OUT + 1}
    res = pl.pallas_call(
        functools.partial(_proj_kernel, transposed=transposed, aliased=aliased),
        grid=(n // tm,),
        in_specs=in_specs,
        out_specs=out_specs,
        out_shape=out_shape,
        scratch_shapes=[pltpu.VMEM((tm, d), BF16)],
        input_output_aliases=aliases,
        compiler_params=_params(1),
        name="in_proj",
    )(*args)
    return res[:N_PROJ_OUT], (tuple(res[N_PROJ_OUT:]) if transposed else None)


def _hgrn_kernel(*refs, nb, nch, aliased):
    g_ref, q_ref, k_ref, v_ref, gate_ref, og_ref, s0_ref = refs[:7]
    a_ref, sout_ref, st_ref = refs[7 + aliased:]
    j = pl.program_id(1)

    @pl.when(j == 0)
    def _():
        for n in range(nb):
            for h in range(A_HEADS):
                st_ref[n, h] = s0_ref[0, n, h].T

    rowc = lax.broadcasted_iota(jnp.int32, (CHUNK, CHUNK), 0)
    colc = lax.broadcasted_iota(jnp.int32, (CHUNK, CHUNK), 1)
    col_sub = colc // SUB
    row_k = lax.broadcasted_iota(jnp.int32, (CHUNK, A_HEAD_DIM), 0)
    lane_d = lax.broadcasted_iota(jnp.int32, (SUB, CHUNK), 1)
    n_sub = CHUNK // SUB
    head = [slice(h * A_HEAD_DIM, (h + 1) * A_HEAD_DIM) for h in range(A_HEADS)]

    def sub_starts(g_all):
        return [jnp.zeros((1, A_WIDTH), F32)] + [g_all[SUB * s - 1:SUB * s, :] for s in range(1, n_sub)]

    span = jnp.zeros((1, A_WIDTH), F32)
    for n in range(nb):
        for c in range(nch):
            for s in range(n_sub):
                last = c * CHUNK + SUB * s + SUB - 1
                end = g_ref[n, last:last + 1, :]
                span = jnp.maximum(span, -end if s == 0 else g_ref[n, last - SUB:last - SUB + 1, :] - end)
    small_decay = jnp.max(span) < SAFE_SPAN * LOG2_E

    def att_factored(g_all, q_all, k_all, qg_all):
        starts = sub_starts(g_all)
        operands = []
        for sl in head:
            gh, qh, kh = g_all[:, sl], q_all[:, sl], k_all[:, sl]
            parts, krows = [qg_all[:, sl]], []
            for s in range(n_sub):
                rs = slice(SUB * s, SUB * (s + 1))
                st = starts[s][:, sl]
                krows.append(kh[rs] * jnp.exp2(st - gh[rs]))
                if s:
                    parts.append(qh[SUB * s:, :] * jnp.exp2(gh[SUB * s:, :] - st))
            operands.append((jnp.concatenate(parts, axis=0).astype(BF16),
                             jnp.concatenate(krows, axis=0).astype(BF16)))
        rrs = [lax.dot_general(qr, kr, NT_DIMS, preferred_element_type=F32) for qr, kr in operands]
        atts = []
        for rr in rrs:
            att = rr[0:CHUNK, :]
            r0 = CHUNK
            for s in range(1, n_sub):
                rows = CHUNK - SUB * s
                block = jnp.concatenate([jnp.zeros((SUB * s, CHUNK), F32), rr[r0:r0 + rows, :]], axis=0)
                att = jnp.where(col_sub == s, block, att)
                r0 += rows
            atts.append(jnp.where(rowc >= colc, att, 0.0))
        return atts

    def att_exact(g_all, q_all, k_all, qg_all):
        def one_head(sl):
            gh, qh, kh = g_all[:, sl], q_all[:, sl], k_all[:, sl]
            parts, refrows = [], []
            for jsub in range(n_sub):
                ref = gh[SUB * jsub + SUB - 1:SUB * jsub + SUB, :]
                refrows.append(jnp.broadcast_to(ref, (SUB, A_HEAD_DIM)))
                if jsub < n_sub - 1:
                    later = row_k >= SUB * (jsub + 1)
                    parts.append(jnp.where(later, qh * jnp.exp2(jnp.minimum(gh - ref, 0.0)), 0.0))
            q_rel = jnp.concatenate(parts, axis=0).astype(BF16)
            k_rel = (kh * jnp.exp2(jnp.minimum(jnp.concatenate(refrows, axis=0) - gh, 0.0))).astype(BF16)
            rr = lax.dot_general(q_rel, k_rel, NT_DIMS, preferred_element_type=F32)
            att = jnp.zeros((CHUNK, CHUNK), F32)
            for jsub in range(n_sub - 1):
                att = jnp.where(col_sub == jsub, rr[jsub * CHUNK:(jsub + 1) * CHUNK, :], att)

            dparts = []
            for csub in range(n_sub):
                rs = slice(SUB * csub, SUB * (csub + 1))
                gc, qc, kc = gh[rs], qh[rs], kh[rs]
                accd = jnp.zeros((SUB, CHUNK), F32)
                for s in range(SUB):
                    dec = jnp.exp2(jnp.minimum(gc - gc[s:s + 1, :], 0.0))
                    col = jnp.sum(dec * qc * kc[s:s + 1, :], axis=-1, keepdims=True)
                    accd = jnp.where(lane_d == SUB * csub + s, col, accd)
                dparts.append(accd)
            return att + jnp.where(rowc >= colc, jnp.concatenate(dparts, axis=0), 0.0)

        return [one_head(sl) for sl in head]

    def chunk(n, r0, sts, att_fn):
        g_all = g_ref[n, pl.ds(r0, CHUNK), :]
        q_all = q_ref[n, pl.ds(r0, CHUNK), :]
        k_all = k_ref[n, pl.ds(r0, CHUNK), :]
        v_all = v_ref[n, pl.ds(r0, CHUNK), :]
        ag = gate_ref[n, pl.ds(r0, CHUNK), :]
        qg_all = q_all * jnp.exp2(g_all)
        gl_all = g_all[CHUNK - 1:CHUNK, :]
        kd_all = (k_all * jnp.exp2(gl_all - g_all)).astype(BF16)
        qg_bf = qg_all.astype(BF16)
        atts = att_fn(g_all, q_all, k_all, qg_all)

        o_state = [lax.dot_general(qg_bf[:, sl], st.astype(BF16), NT_DIMS, preferred_element_type=F32)
                   for sl, st in zip(head, sts)]
        o_pairs = [_dot(att.astype(BF16), v_all[:, sl]) for sl, att in zip(head, atts)]
        updates = [lax.dot_general(v_all[:, sl], kd_all[:, sl], TN_DIMS, preferred_element_type=F32)
                   for sl in head]
        new_sts = [st * jnp.exp2(gl_all[:, sl]) + up for sl, st, up in zip(head, sts, updates)]
        o = jnp.concatenate([a + b for a, b in zip(o_state, o_pairs)], axis=1)
        sq = o * o
        inv = jnp.concatenate(
            [jnp.broadcast_to(lax.rsqrt(jnp.mean(sq[:, sl], axis=-1, keepdims=True) + EPS), (CHUNK, A_HEAD_DIM))
             for sl in head], axis=1)
        a_ref[n, pl.ds(r0, CHUNK), :] = (o * inv * og_ref[...] * _silu(ag)).astype(BF16)
        return new_sts

    @pl.when(small_decay)
    def _():
        for n in range(nb):
            sts = [st_ref[n, h] for h in range(A_HEADS)]
            for c in range(nch):
                sts = chunk(n, c * CHUNK, sts, att_factored)
            for h in range(A_HEADS):
                st_ref[n, h] = sts[h]

    @pl.when(jnp.logical_not(small_decay))
    def _():
        for n in range(nb):
            def body(c, carry):
                sts = chunk(n, pl.multiple_of(c * CHUNK, CHUNK), [st_ref[n, h] for h in range(A_HEADS)], att_exact)
                for h in range(A_HEADS):
                    st_ref[n, h] = sts[h]
                return carry
            lax.fori_loop(0, nch, body, 0)

    @pl.when(j == pl.num_programs(1) - 1)
    def _():
        for n in range(nb):
            for h in range(A_HEADS):
                sout_ref[0, n, h] = st_ref[n, h].T


def _hgrn(gates, onorm_g, s0, s0_layer, layer, depth, s_all):
    b, l, _ = gates[0].shape
    tl = HGRN_ROWS if l % HGRN_ROWS == 0 else CHUNK
    nb = max(1, min(b, HGRN_ROWS // tl, HGRN_SEQS))
    while b % nb:
        nb -= 1
    tile = pl.BlockSpec((nb, tl, A_WIDTH), lambda i, j: (i, j, 0))
    st_block = (1, nb, A_HEADS, A_HEAD_DIM, A_HEAD_DIM)
    aliased = s_all is not None
    in_specs = [tile] * 5 + [pl.BlockSpec((1, A_WIDTH), lambda i, j: (0, 0)),
                             pl.BlockSpec(st_block, lambda i, j: (s0_layer, i, 0, 0, 0))]
    args = list(gates) + [onorm_g.reshape(1, A_WIDTH), s0]
    if aliased:
        in_specs.append(pl.BlockSpec(memory_space=pl.ANY))
        args.append(s_all)
    return pl.pallas_call(
        functools.partial(_hgrn_kernel, nb=nb, nch=tl // CHUNK, aliased=aliased),
        grid=(b // nb, l // tl),
        in_specs=in_specs,
        out_specs=[pl.BlockSpec((nb, tl, A_WIDTH), lambda i, j: (i, j, 0)),
                   pl.BlockSpec(st_block, lambda i, j: (layer, i, 0, 0, 0))],
        out_shape=[jax.ShapeDtypeStruct((b, l, A_WIDTH), BF16),
                   jax.ShapeDtypeStruct((depth, b, A_HEADS, A_HEAD_DIM, A_HEAD_DIM), F32)],
        scratch_shapes=[pltpu.VMEM((nb, A_HEADS, A_HEAD_DIM, A_HEAD_DIM), F32)],
        input_output_aliases={7: 1} if aliased else {},
        compiler_params=_params(2),
        name="hgrn2",
    )(*args)


def _sb_kernel(*refs, ns, lq, n_past):
    if n_past:
        q_ref, k_ref, v_ref, g_ref, pk_ref, pv_ref, o_ref = refs[:7]
        scratch = refs[7:]
    else:
        q_ref, k_ref, v_ref, g_ref, o_ref = refs[:5]
        pk_ref = pv_ref = None
        scratch = refs[5:]
    kc_ref, vtc_ref, qtm_ref, carry_ref, acc_ref = scratch
    j = pl.program_id(1)
    kb_rows = KEY_BLOCK
    run_len = kb_rows // SUBLANES
    width = B_WIDTH
    streams = range(ns)

    row = lax.broadcasted_iota(jnp.int32, (kb_rows, kb_rows), 0)
    lane = lax.broadcasted_iota(jnp.int32, (kb_rows, kb_rows), 1)
    key_of_row = (row % SUBLANES) * run_len + row // SUBLANES
    causal = key_of_row < lane
    perm = (lane == key_of_row).astype(F32)
    sub = lax.broadcasted_iota(jnp.int32, (SUBLANES, kb_rows), 0)

    def padded(rows):
        if rows.shape[0] == kb_rows:
            return rows
        return jnp.concatenate([rows, jnp.zeros((kb_rows - rows.shape[0], rows.shape[1]), rows.dtype)], axis=0)

    row_head = lax.broadcasted_iota(jnp.int32, (width, kb_rows), 0) // B_HEAD_DIM
    k_perm = [_dot(perm, padded(k_ref[s])) for s in streams]
    v_perm = [_dot(perm, padded(v_ref[s])) for s in streams]
    for s in streams:
        kc_ref[s, j] = k_perm[s].astype(BF16)
        vtc_ref[s, j] = v_perm[s].T.astype(BF16)
        qt = padded(q_ref[s] * (0.5 * B_HEAD_DIM ** -0.5)).T
        qtm_ref[s] = jnp.concatenate([jnp.where(row_head == h, qt, 0.0) for h in range(B_HEADS)],
                                     axis=1).astype(BF16)

    carry_ref[...] = jnp.ones(carry_ref.shape, F32)
    acc_ref[...] = jnp.zeros(acc_ref.shape, F32)

    def process(blocks, masked):
        zts = [_dot(kb, qtm_ref[s]) for s, (kb, _) in enumerate(blocks)]
        ws = []
        for s, zt in enumerate(zts):
            for h in range(B_HEADS):
                ls = slice(h * kb_rows, (h + 1) * kb_rows)
                half_t = 0.5 * jnp.tanh(zt[:, ls])
                beta = 0.5 + half_t
                omb = 0.5 - half_t
                if masked:
                    beta = jnp.where(causal, beta, 0.0)
                    omb = jnp.where(causal, omb, 1.0)
                run = jnp.ones((SUBLANES, kb_rows), F32)
                excl = [None] * run_len
                for jj in reversed(range(run_len)):
                    excl[jj] = run
                    run = run * omb[jj * SUBLANES:(jj + 1) * SUBLANES]
                inc = run
                for d in (1, 2, 4):
                    inc = jnp.where(sub + d < SUBLANES, inc * pltpu.roll(inc, SUBLANES - d, 0), inc)
                carry = carry_ref[s, :, ls]
                off = jnp.where(sub < SUBLANES - 1, pltpu.roll(inc, SUBLANES - 1, 0), 1.0) * carry
                ws.append(jnp.concatenate([beta[jj * SUBLANES:(jj + 1) * SUBLANES] * (excl[jj] * off)
                                           for jj in range(run_len)], axis=0).astype(BF16))
                carry_ref[s, :, ls] = carry * jnp.broadcast_to(inc[0:1, :], (SUBLANES, kb_rows))
        for s, (_, vtb) in enumerate(blocks):
            for h in range(B_HEADS):
                hs = slice(h * B_HEAD_DIM, (h + 1) * B_HEAD_DIM)
                acc_ref[s, hs, :] += _dot(vtb[hs, :], ws[s * B_HEADS + h])

    def alive():
        return (jnp.max(carry_ref[...]) > 0.0).astype(jnp.int32)

    process([(kc_ref[s, j], vtc_ref[s, j]) for s in streams], True)

    def cond(state):
        kb, live = state
        return jnp.logical_and(kb >= 0, live > 0)

    def body(state):
        kb, _ = state
        process([(kc_ref[s, kb], vtc_ref[s, kb]) for s in streams], False)
        return kb - 1, alive()

    lax.while_loop(cond, body, (j - 1, alive()))

    for kb in reversed(range(n_past)):
        @pl.when(alive() > 0)
        def _():
            cols = slice(kb * kb_rows, (kb + 1) * kb_rows)
            blocks = []
            for s in streams:
                kt = pk_ref[0, s, :, cols]
                vt = pv_ref[0, s, :, cols]
                blocks.append((lax.dot_general(perm, kt, NT_DIMS, preferred_element_type=F32).astype(BF16),
                               lax.dot_general(vt, perm, NT_DIMS, preferred_element_type=F32).astype(BF16)))
            process(blocks, False)

    for s in streams:
        o = acc_ref[s].T[0:lq, :]
        o_ref[s] = (o * _silu(g_ref[s])).astype(BF16)


def _sb(q, k, v, g, past_t, layer):
    b, l, w = q.shape
    lq = min(KEY_BLOCK, l)
    nblk = l // lq
    ns = min(b, SB_SEQS)
    while b % ns:
        ns -= 1
    p = 0 if past_t is None else past_t[0].shape[3]
    blk = pl.BlockSpec((ns, lq, w), lambda i, j: (i, j, 0))
    in_specs = [blk] * 4
    args = [q, k, v, g]
    if p:
        in_specs += [pl.BlockSpec((1, ns, w, p), lambda i, j: (layer, i, 0, 0))] * 2
        args += list(past_t)
    return pl.pallas_call(
        functools.partial(_sb_kernel, ns=ns, lq=lq, n_past=p // KEY_BLOCK),
        grid=(b // ns, nblk),
        in_specs=in_specs,
        out_specs=blk,
        out_shape=jax.ShapeDtypeStruct((b, l, w), BF16),
        scratch_shapes=[pltpu.VMEM((ns, nblk, KEY_BLOCK, w), BF16),
                        pltpu.VMEM((ns, nblk, w, KEY_BLOCK), BF16),
                        pltpu.VMEM((ns, w, B_HEADS * KEY_BLOCK), BF16),
                        pltpu.VMEM((ns, SUBLANES, B_HEADS * KEY_BLOCK), F32),
                        pltpu.VMEM((ns, w, KEY_BLOCK), F32)],
        compiler_params=_params(2),
        name="sb_attn",
    )(*args)


def _post_kernel(x_ref, a_ref, b_ref, pc_ref, pprev_ref, mk_hbm, mv_hbm, wpool_ref, pscale_ref,
                 woa_ref, wob_ref, woc_ref, npost_ref, nxpre_ref, wxq_ref, wxo_ref, nxpost_ref,
                 xo_ref, pnew_ref, ext_ref, sum_ref, kv_ref, mkb_ref, mvb_ref, sem_ref,
                 *, nb, tl, n_split, past, layer):
    i = pl.program_id(0)
    j = pl.program_id(1)
    n_groups = pl.num_programs(0)
    halo = POOL_STATE + 1
    lead = SUBLANES
    dh = kv_ref.shape[-1]

    def kv_copies(group, slot):
        return [pltpu.make_async_copy(src.at[layer, group * nb + n, :, h, :], kv_ref.at[slot, t, n, h],
                                      sem_ref.at[slot, t])
                for t, src in enumerate((mk_hbm, mv_hbm)) for n in range(nb) for h in range(X_HEADS)]

    @pl.when(j == 0)
    def _():
        slot = i % 2

        @pl.when(i == 0)
        def _():
            for c, cp in enumerate(kv_copies(0, 0)):
                cp.start(priority=c % 2)

        @pl.when(i + 1 < n_groups)
        def _():
            for c, cp in enumerate(kv_copies(i + 1, 1 - slot)):
                cp.start(priority=c % 2)

        for cp in kv_copies(i, slot):
            cp.wait()
        for n in range(nb):
            ext_ref[n, 0:lead, :] = jnp.zeros((lead, C_WIDTH), F32)
            ext_ref[n, lead:lead + halo, :] = pprev_ref[n]
            for gi in range(len(POOL_WINDOWS) - 1):
                sum_ref[gi, n, 0:lead, :] = jnp.zeros((lead, C_WIDTH), F32)
            for h in range(X_HEADS):
                mkb_ref[n, :, h * dh:(h + 1) * dh] = kv_ref[slot, 0, n, h].astype(BF16)
                mvb_ref[n, :, h * dh:(h + 1) * dh] = kv_ref[slot, 1, n, h].astype(BF16)

    grp = lax.broadcasted_iota(jnp.int32, (tl, C_WIDTH), 1) // C_GROUP_DIM
    pos = past + j * tl + lax.broadcasted_iota(jnp.int32, (tl, C_WIDTH), 0)
    dds = []
    rows = halo + tl
    for n in range(nb):
        cu = pc_ref[n, :, 0:C_WIDTH]
        ext_ref[n, lead + halo:lead + rows, :] = cu
        win = jnp.zeros((tl, C_WIDTH), F32)
        wlen = jnp.zeros((tl, C_WIDTH), jnp.int32)
        src = ext_ref.at[n]
        for gi, w in enumerate(POOL_WINDOWS):
            assert w == 2 ** (gi + 1)
            wsum = src[lead:lead + rows, :] + src[lead - w // 2:lead - w // 2 + rows, :]
            if gi + 1 < len(POOL_WINDOWS):
                sum_ref[gi, n, lead:lead + rows, :] = wsum
                src = sum_ref.at[gi, n]
            win = jnp.where(grp == gi, wsum[halo:, :], win)
            wlen = jnp.where(grp == gi, w, wlen)
        cnt = jnp.minimum(pos + 1, wlen).astype(F32)
        dds.append((win / cnt - cu).astype(BF16))
        pnew_ref[n] = ext_ref[n, lead + tl + 1:lead + tl + halo, :]
        ext_ref[n, lead:lead + halo, :] = ext_ref[n, lead + tl:lead + tl + halo, :]

    seg = tl // n_split if nb == 1 else tl
    groups = [[(0, g * seg)] for g in range(n_split)] if nb == 1 else \
             [[(n, 0) for n in range(g * nb // n_split, (g + 1) * nb // n_split)] for g in range(n_split)]

    def rows(members, ref, w0=0, w1=None):
        return jnp.concatenate([ref[n, r0:r0 + seg, w0:w1] for n, r0 in members], axis=0)

    mixes = []
    for members in groups:
        dd = jnp.concatenate([dds[n][r0:r0 + seg] for n, r0 in members], axis=0)
        y = _dot(dd, wpool_ref[...]) * pscale_ref[...]
        c_out = (y * _silu(rows(members, pc_ref, C_WIDTH, 2 * C_WIDTH))).astype(BF16)
        mixes.append(_dot(rows(members, a_ref), woa_ref[...]) + _dot(rows(members, b_ref), wob_ref[...])
                     + _dot(c_out, woc_ref[...]))
    x1s = [rows(members, x_ref) + _rms(mix, npost_ref[...]) for members, mix in zip(groups, mixes)]

    qs = [_dot(_rms(x1, nxpre_ref[...]).astype(BF16), wxq_ref[...]).astype(BF16) for x1 in x1s]
    pairs = [(g, m, n, h) for g, members in enumerate(groups) for m, (n, _) in enumerate(members)
             for h in range(X_HEADS)]
    scores = [lax.dot_general(qs[g][m * seg:(m + 1) * seg, h * dh:(h + 1) * dh],
                              mkb_ref[n, :, h * dh:(h + 1) * dh], NT_DIMS,
                              preferred_element_type=F32) * (dh ** -0.5) for g, m, n, h in pairs]
    probs = []
    for s in scores:
        ex = jnp.exp(s - jnp.max(s, axis=-1, keepdims=True))
        probs.append((ex / jnp.sum(ex, axis=-1, keepdims=True)).astype(BF16))
    ctx = [_dot(p, mvb_ref[n, :, h * dh:(h + 1) * dh]).astype(BF16) for p, (g, m, n, h) in zip(probs, pairs)]
    xos = []
    for g, members in enumerate(groups):
        seqs = [jnp.concatenate([c for c, (g2, m2, _, _) in zip(ctx, pairs) if g2 == g and m2 == m], axis=1)
                for m in range(len(members))]
        xos.append(_dot(jnp.concatenate(seqs, axis=0), wxo_ref[...]))
    for members, x1, xo in zip(groups, x1s, xos):
        res = x1 + _rms(xo, nxpost_ref[...])
        for m, (n, r0) in enumerate(members):
            xo_ref[n, r0:r0 + seg, :] = res[m * seg:(m + 1) * seg]


def _post(x, a_out, b_out, pc, pool_prev, mk, mv, wts, past, layer):
    b, l, d = x.shape
    n_mem, dh = mk.shape[2], mk.shape[4]
    tl = POST_ROWS if l % POST_ROWS == 0 else CHUNK
    nb = max(1, min(b, POST_ROWS // tl, POST_SEQS))
    while b % nb:
        nb -= 1
    n_split = 2 if (nb * tl) % (2 * LANES) == 0 and (nb == 1 or nb % 2 == 0) else 1
    halo = POOL_STATE + 1
    pprev = jnp.concatenate([jnp.zeros((b, 1, C_WIDTH), F32), pool_prev], axis=1)
    tile = lambda w: pl.BlockSpec((nb, tl, w), lambda i, j: (i, j, 0))
    per_seq = lambda r, w: pl.BlockSpec((nb, r, w), lambda i, j: (i, 0, 0))
    full = lambda a: pl.BlockSpec(a.shape, lambda i, j: (0,) * a.ndim)
    hbm = pl.BlockSpec(memory_space=pl.ANY)
    return pl.pallas_call(
        functools.partial(_post_kernel, nb=nb, tl=tl, n_split=n_split, past=past, layer=layer),
        grid=(b // nb, l // tl),
        in_specs=[tile(d), tile(A_WIDTH), tile(B_WIDTH), tile(2 * C_WIDTH), per_seq(halo, C_WIDTH),
                  hbm, hbm] + [full(a) for a in wts],
        out_specs=[tile(d), per_seq(POOL_STATE, C_WIDTH)],
        out_shape=[jax.ShapeDtypeStruct((b, l, d), F32),
                   jax.ShapeDtypeStruct((b, POOL_STATE, C_WIDTH), F32)],
        scratch_shapes=[pltpu.VMEM((nb, SUBLANES + halo + tl, C_WIDTH), F32),
                        pltpu.VMEM((len(POOL_WINDOWS) - 1, nb, SUBLANES + halo + tl, C_WIDTH), F32),
                        pltpu.VMEM((2, 2, nb, X_HEADS, n_mem, dh), F32),
                        pltpu.VMEM((nb, n_mem, d), BF16),
                        pltpu.VMEM((nb, n_mem, d), BF16),
                        pltpu.SemaphoreType.DMA((2, 2))],
        compiler_params=_params(2),
        name="mix_out",
    )(x, a_out, b_out, pc, pprev, mk, mv, *wts)


def _layer(x, mk, mv, past_t, s0, s0_layer, pool_prev, lw, layer, depth, chain):
    kv_t, s_all = chain
    b, l, d = x.shape
    n = b * l
    proj, kv_t = _proj(x.reshape(n, d), lw["n_pre"], lw["w_in"], lw["lb"], l, layer, depth, kv_t)
    seq = lambda t: t.reshape(b, l, -1)
    gates, (bq, bk, bv, bg, pc) = [seq(t) for t in proj[:5]], proj[5:]
    a_out, s_all = _hgrn(gates, lw["onorm_g"], s0, s0_layer, layer, depth, s_all)
    b_out = _sb(seq(bq), seq(bk), seq(bv), seq(bg), past_t, layer)
    past = 0 if past_t is None else past_t[0].shape[3]
    x_new, pool_new = _post(x, a_out, b_out, seq(pc), pool_prev, mk, mv, lw["post"], past, layer)
    heads = lambda t: t.reshape(b, l, B_HEADS, B_HEAD_DIM)
    return x_new, heads(bk), heads(bv), (kv_t, s_all), pool_new


def kernel(x_prompt, x_sample, mem_prompt, cache_sb_k, cache_sb_v, state_hgrn, state_pool, cache_mem_k, cache_mem_v, norm_mix_pre, norm_mix_post, w_in, hgrn_lb_logits, hgrn_onorm_g, w_pool, pool_scale, w_out, norm_x_pre, norm_x_post, norm_mem, w_xq, w_xk, w_xv, w_xo):
    depth, d = norm_mix_pre.shape
    bp = x_prompt.shape[0]
    bs, past = cache_sb_k.shape[1], cache_sb_k.shape[2]

    lbs = jax.nn.softmax(hgrn_lb_logits.astype(F32), axis=0)
    lower_bounds = jnp.maximum(jnp.cumsum(lbs, axis=0) - lbs[0], 0.0)

    mk_all, mv_all = _memkv(mem_prompt, norm_mem, w_xk.astype(BF16), w_xv.astype(BF16))

    def seq_minor(c):
        return jnp.transpose(c, (0, 1, 3, 4, 2)).reshape(depth, bs, B_WIDTH, past)

    def seq_major(t):
        dp, b, _, l = t.shape
        return jnp.transpose(t.reshape(dp, b, B_HEADS, B_HEAD_DIM, l), (0, 1, 4, 2, 3))

    past_t = (seq_minor(cache_sb_k), seq_minor(cache_sb_v))

    row = lambda v: v.reshape(1, -1)
    s_zero = jnp.zeros((1, bp, A_HEADS, A_HEAD_DIM, A_HEAD_DIM), F32)
    pool_zero = jnp.zeros((bp, POOL_STATE, C_WIDTH), F32)

    xp, xs = x_prompt, x_sample
    p_kn, p_vn, p_pools, s_kn, s_vn, s_pools = [], [], [], [], [], []
    chain_p = chain_s = (None, None)
    for l in range(depth):
        wp = jnp.zeros((C_WIDTH, C_WIDTH), F32)
        for gi in range(C_GROUPS):
            gs = slice(gi * C_GROUP_DIM, (gi + 1) * C_GROUP_DIM)
            wp = wp.at[gs, gs].set(w_pool[l, gi])
        wo = w_out[l].astype(BF16)
        lw = {
            "n_pre": norm_mix_pre[l], "w_in": w_in[l].astype(BF16), "lb": lower_bounds[l],
            "onorm_g": hgrn_onorm_g[l],
            "post": (wp.astype(BF16), row(pool_scale[l]), wo[0:A_WIDTH], wo[A_WIDTH:A_WIDTH + B_WIDTH],
                     wo[A_WIDTH + B_WIDTH:], row(norm_mix_post[l]), row(norm_x_pre[l]),
                     w_xq[l].astype(BF16), w_xo[l].astype(BF16), row(norm_x_post[l])),
        }
        xp, kb, vb, chain_p, pn = _layer(xp, mk_all, mv_all, None, s_zero, 0, pool_zero, lw, l, depth, chain_p)
        p_kn.append(kb); p_vn.append(vb); p_pools.append(pn)
        xs, kb, vb, chain_s, pn = _layer(xs, cache_mem_k, cache_mem_v, past_t, state_hgrn, l, state_pool[l],
                                         lw, l, depth, chain_s)
        s_kn.append(kb); s_vn.append(vb); s_pools.append(pn)

    def new_kv(kvt, nat_k, nat_v):
        if kvt is not None:
            return seq_major(kvt[0]), seq_major(kvt[1])
        return jnp.stack(nat_k), jnp.stack(nat_v)

    p_k, p_v = new_kv(chain_p[0], p_kn, p_vn)
    s_k, s_v = new_kv(chain_s[0], s_kn, s_vn)
    return (xp, xs, p_k, p_v, chain_p[1], jnp.stack(p_pools), mk_all, mv_all, s_k, s_v, chain_s[1],
            jnp.stack(s_pools))
```
